```python
import math
import functools
import jax
import jax.numpy as jnp
from jax import lax
import numpy as np

D_MODEL = 1024
BATCH = 8
SEQ = 2048
DEPTH = 2

GRID_W = 64
CTX_LEN = 256
HEAD_DIM = 64
ROPE_BASE = 10000.0
EPS = 1e-6

NA_HEADS = 8
NA_WIN_ROWS = 8
NA_WIN_COLS = 16
NA_QCOLS = 16
NA_KCOLS = 32
NA_WIDTH = NA_HEADS * HEAD_DIM

SSM_HEADS = 16
SSM_HEADDIM = 64
SSM_D_INNER = SSM_HEADS * SSM_HEADDIM
SSM_STATE = 128
SSM_GROUPS = 2
SSM_CONV = 5
SSM_CONV_CH = SSM_D_INNER + 2 * SSM_GROUPS * SSM_STATE
SCAN_CHUNK = 128

SWA_HEADS = 8
SWA_KV_HEADS = 2
SWA_WINDOW = 128
SWA_BLOCK = 128
SWA_Q = SWA_HEADS * HEAD_DIM
SWA_KV = SWA_KV_HEADS * HEAD_DIM

RET_HEADS = 8
RET_QK_DIM = 64
RET_V_DIM = 128
RET_QK = RET_HEADS * RET_QK_DIM
RET_V = RET_HEADS * RET_V_DIM

EVEN_IN = 3 * NA_WIDTH + SSM_D_INNER + SSM_CONV_CH + 2 * SSM_HEADS
EVEN_OUT = NA_WIDTH + SSM_D_INNER
ODD_IN = SWA_Q + 2 * SWA_KV + 2 * RET_QK + 2 * RET_V
ODD_OUT = SWA_Q + RET_V

D_FF = 2816
N_EXPERTS = 8
TOP_K = 2
D_FF_EXPERT = 3584

kernel_name = "hybrid_diffusion_prefix_backbone"


def split_cols(t, sizes):
    return jnp.split(t, np.cumsum(sizes)[:-1].tolist(), axis=-1)


def to_heads(t, n_heads):
    return t.reshape(t.shape[:-1] + (n_heads, t.shape[-1] // n_heads))


def rmsnorm(x, g):
    xf = x.astype(jnp.float32)
    y = xf * lax.rsqrt(jnp.mean(jnp.square(xf), axis=-1, keepdims=True) + EPS)
    return (y * g.astype(jnp.float32)).astype(x.dtype)


def modulate(h, shift, scale):
    return h * (1 + scale) + shift


def axial_rope(n_tokens, dim):
    t = jnp.arange(n_tokens)
    row = (t // GRID_W).astype(jnp.float32)
    col = (t % GRID_W).astype(jnp.float32)
    n_freq = dim // 4
    inv = ROPE_BASE ** (-jnp.arange(n_freq, dtype=jnp.float32) / n_freq)
    ang = jnp.concatenate([row[:, None] * inv, col[:, None] * inv], axis=-1)
    return jnp.cos(ang)[:, None, :], jnp.sin(ang)[:, None, :]


def apply_rope(x, cos, sin):
    x1, x2 = jnp.split(x.astype(jnp.float32), 2, axis=-1)
    return jnp.concatenate([x1 * cos - x2 * sin, x2 * cos + x1 * sin], axis=-1).astype(x.dtype)


def centred_depthwise_conv(x, w, bias):
    pad = (w.shape[0] - 1) // 2
    y = lax.conv_general_dilated(x, w, window_strides=(1,), padding=[(pad, pad)],
                                 dimension_numbers=('NWC', 'WIO', 'NWC'),
                                 feature_group_count=x.shape[-1])
    return y + bias


def chunked_scan(x, a, Bm, Cm, h0, need_y):
    b, L, H, P = x.shape
    G, N = Bm.shape[2], Bm.shape[3]
    hg = H // G
    Q = SCAN_CHUNK
    nc = L // Q
    xr = x.reshape(b, nc, Q, G, hg, P)
    Br = Bm.reshape(b, nc, Q, G, N)
    Cr = Cm.reshape(b, nc, Q, G, N)
    a_cum = jnp.cumsum(a.astype(jnp.float32).reshape(b, nc, Q, G, hg), axis=2)
    a_tot = a_cum[:, :, -1]
    w_state = jnp.exp(a_tot[:, :, None] - a_cum).astype(x.dtype)
    states = jnp.einsum('bcjgn,bcjgh,bcjghp->bcghpn', Br, w_state, xr)

    def step(h, inp):
        decay, s = inp
        return jnp.exp(decay)[..., None, None].astype(h.dtype) * h + s, h

    h_final, h_prev = lax.scan(step, h0.reshape(b, G, hg, P, N),
                               (jnp.moveaxis(a_tot, 1, 0), jnp.moveaxis(states, 1, 0)))
    h_final = h_final.reshape(b, H, P, N)
    if not need_y:
        return None, h_final
    h_prev = jnp.moveaxis(h_prev, 0, 1)
    diff = a_cum[:, :, :, None] - a_cum[:, :, None, :]
    causal = np.tril(np.ones((Q, Q), dtype=bool))[:, :, None, None]
    decay_mat = jnp.exp(jnp.where(causal, diff, -jnp.inf)).astype(x.dtype)
    cb = jnp.einsum('bcign,bcjgn->bcijg', Cr, Br)
    y_diag = jnp.einsum('bcijg,bcijgh,bcjghp->bcighp', cb, decay_mat, xr)
    y_off = jnp.einsum('bcign,bcghpn,bcigh->bcighp', Cr, h_prev, jnp.exp(a_cum).astype(x.dtype))
    return (y_diag + y_off).reshape(b, L, H, P), h_final


def scan_from_context(xc, ac, Bc, Cc, xl, al, Bl, Cl, reverse, need_ctx):
    if reverse:
        xc, ac, Bc, Cc, xl, al, Bl, Cl = [jnp.flip(t, axis=1) for t in (xc, ac, Bc, Cc, xl, al, Bl, Cl)]
    b, _, H, P = xc.shape
    h0 = jnp.zeros((b, H, P, Bc.shape[-1]), xc.dtype)
    yc, hc = chunked_scan(xc, ac, Bc, Cc, h0, need_ctx)
    yl, _ = chunked_scan(xl, al, Bl, Cl, hc, True)
    if reverse:
        yl = jnp.flip(yl, axis=1)
        yc = jnp.flip(yc, axis=1) if need_ctx else None
    return yc, yl


def context_attention(q, k, v, sink):
    b, lc, H, D = q.shape
    kvh = k.shape[2]
    g = H // kvh
    s = jnp.einsum('bqkgd,bskd->bkgqs', q.reshape(b, lc, kvh, g, D), k) * D ** -0.5
    if sink is not None:
        s_sink = jnp.broadcast_to(sink.reshape(1, kvh, g, 1, 1).astype(s.dtype), (b, kvh, g, lc, 1))
        s = jnp.concatenate([s, s_sink], axis=-1)
    p = jax.nn.softmax(s.astype(jnp.float32), axis=-1).astype(v.dtype)
    o = jnp.einsum('bkgqs,bskd->bqkgd', p[..., :lc], v)
    return o.reshape(b, lc, H * D)


def neighbourhood_attention(q, k, v, k_ctx, v_ctx, rpb):
    b, L, H, D = q.shape
    rows = L // GRID_W
    kr = min(NA_WIN_ROWS, rows)
    ncb = GRID_W // NA_QCOLS
    scale = D ** -0.5
    qcol = np.arange(GRID_W).reshape(ncb, NA_QCOLS)
    kstart = np.clip(np.arange(ncb) * NA_QCOLS - NA_WIN_COLS // 2, 0, GRID_W - NA_KCOLS)
    kcol = kstart[:, None] + np.arange(NA_KCOLS)[None, :]
    c0 = np.clip(qcol - NA_WIN_COLS // 2, 0, GRID_W - NA_WIN_COLS)
    col_ok = (kcol[:, None, :] >= c0[:, :, None]) & (kcol[:, None, :] < c0[:, :, None] + NA_WIN_COLS)
    col_idx = np.clip(kcol[:, None, :] - qcol[:, :, None] + NA_WIN_COLS - 1, 0, 2 * NA_WIN_COLS - 2)
    qg = q.reshape(b, rows, ncb, NA_QCOLS, H, D)
    kg = k.reshape(b, rows, GRID_W, H, D)
    vg = v.reshape(b, rows, GRID_W, H, D)
    n_nb = kr * NA_KCOLS

    def one_row(r):
        r0 = jnp.clip(r - kr // 2, 0, rows - kr)
        q_r = lax.dynamic_index_in_dim(qg, r, axis=1, keepdims=False)
        k_blk = lax.dynamic_slice_in_dim(kg, r0, kr, axis=1)[:, :, kcol]
        v_blk = lax.dynamic_slice_in_dim(vg, r0, kr, axis=1)[:, :, kcol]
        drow = r0 + jnp.arange(kr) - r + NA_WIN_ROWS - 1
        bias = jnp.moveaxis(rpb[:, drow[:, None, None, None], col_idx[None]], 1, 3)
        s_nb = jnp.einsum('bjqhd,brjkhd->bhjqrk', q_r, k_blk) * scale + bias[None]
        s_nb = jnp.where(col_ok[:, :, None, :], s_nb, -jnp.inf)
        s_ctx = jnp.einsum('bjqhd,bkhd->bhjqk', q_r, k_ctx) * scale
        s = jnp.concatenate([s_nb.reshape(b, H, ncb, NA_QCOLS, n_nb), s_ctx.astype(s_nb.dtype)], axis=-1)
        p = jax.nn.softmax(s.astype(jnp.float32), axis=-1).astype(v.dtype)
        p_nb = p[..., :n_nb].reshape(b, H, ncb, NA_QCOLS, kr, NA_KCOLS)
        o = (jnp.einsum('bhjqrk,brjkhd->bjqhd', p_nb, v_blk)
             + jnp.einsum('bhjqk,bkhd->bjqhd', p[..., n_nb:], v_ctx))
        return o.reshape(b, GRID_W, H * D)

    out = lax.map(one_row, jnp.arange(rows))
    return jnp.moveaxis(out, 0, 1).reshape(b, L, H * D)


def sliding_window_attention(q, k, v, k_ctx, v_ctx, sink):
    b, L, H, D = q.shape
    kvh = k.shape[2]
    g = H // kvh
    blk = SWA_BLOCK
    nb = L // blk
    lc = k_ctx.shape[1]
    nw = 3 * blk
    scale = D ** -0.5
    qb = q.reshape(b, nb, blk, kvh, g, D)
    pad = ((0, 0), (blk, blk), (0, 0), (0, 0))
    kp = jnp.pad(k, pad).reshape(b, nb + 2, blk, kvh, D)
    vp = jnp.pad(v, pad).reshape(b, nb + 2, blk, kvh, D)
    k_win = jnp.concatenate([kp[:, :-2], kp[:, 1:-1], kp[:, 2:]], axis=2)
    v_win = jnp.concatenate([vp[:, :-2], vp[:, 1:-1], vp[:, 2:]], axis=2)
    qpos = np.arange(L).reshape(nb, blk)
    kpos = np.arange(nb)[:, None] * blk - blk + np.arange(nw)[None, :]
    ok = ((kpos[:, None, :] >= 0) & (kpos[:, None, :] < L)
          & (np.abs(kpos[:, None, :] - qpos[:, :, None]) <= SWA_WINDOW))
    s_loc = jnp.where(ok, jnp.einsum('bnqkgd,bnskd->bkgnqs', qb, k_win) * scale, -jnp.inf)
    s_ctx = jnp.einsum('bnqkgd,bskd->bkgnqs', qb, k_ctx) * scale
    s_sink = jnp.broadcast_to(sink.reshape(1, kvh, g, 1, 1, 1).astype(s_ctx.dtype), (b, kvh, g, nb, blk, 1))
    s = jnp.concatenate([s_loc, s_ctx, s_sink], axis=-1)
    p = jax.nn.softmax(s.astype(jnp.float32), axis=-1).astype(v.dtype)
    o = (jnp.einsum('bkgnqs,bnskd->bnqkgd', p[..., :nw], v_win)
         + jnp.einsum('bkgnqs,bskd->bnqkgd', p[..., nw:nw + lc], v_ctx))
    return o.reshape(b, L, H * D)


def ssm_inputs(xbc, dt_raw, conv_w, conv_b, dt_bias, a_log):
    b, L, _ = xbc.shape
    xbc = jax.nn.silu(centred_depthwise_conv(xbc, conv_w, conv_b))
    xs, Bm, Cm = split_cols(xbc, [SSM_D_INNER, SSM_GROUPS * SSM_STATE, SSM_GROUPS * SSM_STATE])
    xs = xs.reshape(b, L, SSM_HEADS, SSM_HEADDIM)
    Bm = Bm.reshape(b, L, SSM_GROUPS, SSM_STATE)
    Cm = Cm.reshape(b, L, SSM_GROUPS, SSM_STATE)
    dt = jax.nn.softplus(dt_raw.reshape(b, L, 2, SSM_HEADS) + dt_bias)
    a = dt * (-jnp.exp(a_log))
    xdt = xs[:, :, None] * dt[..., None]
    return xs, Bm, Cm, xdt, a


def ssm_output(y, xs, z, d_skip, g):
    b, L = y.shape[:2]
    y = (y + xs * d_skip[:, None]).reshape(b, L, SSM_D_INNER)
    yz = (y * jax.nn.silu(z)).reshape(b, L, SSM_GROUPS, SSM_D_INNER // SSM_GROUPS).astype(jnp.float32)
    yz = yz * lax.rsqrt(jnp.mean(jnp.square(yz), axis=-1, keepdims=True) + EPS)
    return (yz.reshape(b, L, SSM_D_INNER) * g.astype(jnp.float32)).astype(y.dtype)


def head_layernorm(y, g, bias):
    b, L, H, P = y.shape
    yf = y.astype(jnp.float32)
    mu = jnp.mean(yf, axis=-1, keepdims=True)
    var = jnp.mean(jnp.square(yf - mu), axis=-1, keepdims=True)
    yn = ((yf - mu) * lax.rsqrt(var + EPS)).reshape(b, L, H * P)
    return (yn * g.astype(jnp.float32) + bias.astype(jnp.float32)).astype(y.dtype)


def even_mixer(hc, hl, w_in, w_out, rpb, conv_w, conv_b, dt_bias, a_log, d_skip, gn_g, need_ctx):
    sizes = [NA_WIDTH, NA_WIDTH, NA_WIDTH, SSM_D_INNER, SSM_CONV_CH, 2 * SSM_HEADS]
    qc, kc, vc, zc, xbcc, dtc = split_cols(hc @ w_in, sizes)
    ql, kl, vl, zl, xbcl, dtl = split_cols(hl @ w_in, sizes)
    kc = to_heads(kc, NA_HEADS)
    vc = to_heads(vc, NA_HEADS)
    ya_l = neighbourhood_attention(to_heads(ql, NA_HEADS), to_heads(kl, NA_HEADS), to_heads(vl, NA_HEADS), kc, vc, rpb)
    xs_c, Bc, Cc, xdt_c, a_c = ssm_inputs(xbcc, dtc, conv_w, conv_b, dt_bias, a_log)
    xs_l, Bl, Cl, xdt_l, a_l = ssm_inputs(xbcl, dtl, conv_w, conv_b, dt_bias, a_log)
    yf_c, yf_l = scan_from_context(xdt_c[:, :, 0], a_c[:, :, 0], Bc, Cc,
                                   xdt_l[:, :, 0], a_l[:, :, 0], Bl, Cl, False, need_ctx)
    yb_c, yb_l = scan_from_context(xdt_c[:, :, 1], a_c[:, :, 1], Bc, Cc,
                                   xdt_l[:, :, 1], a_l[:, :, 1], Bl, Cl, True, need_ctx)
    ys_l = ssm_output(yf_l + yb_l, xs_l, zl, d_skip, gn_g)
    y_l = jnp.concatenate([ya_l, ys_l], axis=-1) @ w_out
    if not need_ctx:
        return None, y_l
    ya_c = context_attention(to_heads(qc, NA_HEADS), kc, vc, None)
    ys_c = ssm_output(yf_c + yb_c, xs_c, zc, d_skip, gn_g)
    y_c = jnp.concatenate([ya_c, ys_c], axis=-1) @ w_out
    return y_c, y_l


def odd_mixer(hc, hl, w_in, w_out, sink, log_decay, gn_g, gn_b, need_ctx):
    sizes = [SWA_Q, SWA_KV, SWA_KV, RET_QK, RET_QK, RET_V, RET_V]
    qc, kc, vc, rqc, rkc, rvc, rgc = split_cols(hc @ w_in, sizes)
    ql, kl, vl, rql, rkl, rvl, rgl = split_cols(hl @ w_in, sizes)
    b, L, _ = hl.shape
    lc = hc.shape[1]
    cos, sin = axial_rope(L, HEAD_DIM)
    kc = to_heads(kc, SWA_KV_HEADS)
    vc = to_heads(vc, SWA_KV_HEADS)
    yw_l = sliding_window_attention(apply_rope(to_heads(ql, SWA_HEADS), cos, sin),
                                    apply_rope(to_heads(kl, SWA_KV_HEADS), cos, sin),
                                    to_heads(vl, SWA_KV_HEADS), kc, vc, sink)
    kscale = RET_QK_DIM ** -0.5
    rq_l = apply_rope(to_heads(rql, RET_HEADS), cos, sin)
    rk_l = apply_rope(to_heads(rkl, RET_HEADS), cos, sin) * kscale
    rv_l = to_heads(rvl, RET_HEADS)
    rq_c = to_heads(rqc, RET_HEADS)
    rk_c = to_heads(rkc, RET_HEADS) * kscale
    rv_c = to_heads(rvc, RET_HEADS)
    log_gamma = jnp.log1p(-jnp.exp(log_decay.astype(jnp.float32)))
    ga_c = [jnp.broadcast_to(log_gamma[d], (b, lc, RET_HEADS)) for d in (0, 1)]
    ga_l = [jnp.broadcast_to(log_gamma[d], (b, L, RET_HEADS)) for d in (0, 1)]
    rf_c, rf_l = scan_from_context(rv_c, ga_c[0], rk_c, rq_c, rv_l, ga_l[0], rk_l, rq_l, False, need_ctx)
    rb_c, rb_l = scan_from_context(rv_c, ga_c[1], rk_c, rq_c, rv_l, ga_l[1], rk_l, rq_l, True, need_ctx)
    yr_l = jax.nn.silu(rgl) * head_layernorm(rf_l + rb_l, gn_g, gn_b)
    y_l = jnp.concatenate([yw_l, yr_l], axis=-1) @ w_out
    if not need_ctx:
        return None, y_l
    yw_c = context_attention(to_heads(qc, SWA_HEADS), kc, vc, sink)
    yr_c = jax.nn.silu(rgc) * head_layernorm(rf_c + rb_c, gn_g, gn_b)
    y_c = jnp.concatenate([yw_c, yr_c], axis=-1) @ w_out
    return y_c, y_l


def swiglu(h, w1, w3, w2):
    return (jax.nn.silu(h @ w1) * (h @ w3)) @ w2


def moe_swiglu(h, router, w1, w3, w2):
    logits = (h @ router).astype(jnp.float32)
    top_v, top_i = lax.top_k(logits, TOP_K)
    gates = jax.nn.softmax(top_v, axis=-1)
    dense_gate = jnp.sum(jax.nn.one_hot(top_i, N_EXPERTS, dtype=jnp.float32) * gates[..., None], axis=-2).astype(h.dtype)
    out = jnp.zeros_like(h)
    for e in range(N_EXPERTS):
        out = out + dense_gate[..., e:e + 1] * swiglu(h, w1[e], w3[e], w2[e])
    return out


def setup_inputs(seed: int = 0) -> dict:
    key = jax.random.key(seed)
    keys = iter(jax.random.split(key, 48))
    f32 = jnp.float32
    n_even = (DEPTH + 1) // 2
    n_odd = DEPTH // 2
    D = D_MODEL

    def normal(shape, scale):
        return jax.random.normal(next(keys), shape, f32) * scale

    def gain(shape):
        return 1.0 + normal(shape, 0.05)

    a_log = jnp.log(jax.random.uniform(next(keys), (n_even, 2, SSM_HEADS), f32, 1.0, 16.0))
    dt0 = jnp.exp(jax.random.uniform(next(keys), (n_even, 2, SSM_HEADS), f32, math.log(1e-3), math.log(1e-1)))
    dt_bias = dt0 + jnp.log(-jnp.expm1(-dt0))
    base_decay = (-5.0 - jnp.arange(RET_HEADS, dtype=f32)) * math.log(2.0)
    return {
        'x': normal((BATCH, SEQ, D), 1.0),
        'c': normal((BATCH, D), 1.0),
        'ctx': normal((BATCH, CTX_LEN, D), 1.0),
        'c_ctx': normal((D,), 1.0),
        'ada_w': normal((DEPTH, D, 6 * D), D ** -0.5),
        'ada_b': normal((DEPTH, 6 * D), 0.02),
        'norm_attn_g': gain((DEPTH, D)),
        'norm_ffn_g': gain((DEPTH, D)),
        'ev_w_in': normal((n_even, D, EVEN_IN), D ** -0.5),
        'ev_w_out': normal((n_even, EVEN_OUT, D), EVEN_OUT ** -0.5),
        'na_rpb': normal((n_even, NA_HEADS, 2 * NA_WIN_ROWS - 1, 2 * NA_WIN_COLS - 1), 0.02),
        'ssm_conv_w': normal((n_even, SSM_CONV, 1, SSM_CONV_CH), SSM_CONV ** -0.5),
        'ssm_conv_b': normal((n_even, SSM_CONV_CH), 0.02),
        'ssm_dt_bias': dt_bias,
        'ssm_a_log': a_log,
        'ssm_d': gain((n_even, SSM_HEADS)),
        'ssm_norm_g': gain((n_even, SSM_D_INNER)),
        'ffn_w1': normal((n_even, D, D_FF), D ** -0.5),
        'ffn_w3': normal((n_even, D, D_FF), D ** -0.5),
        'ffn_w2': normal((n_even, D_FF, D), D_FF ** -0.5),
        'od_w_in': normal((n_odd, D, ODD_IN), D ** -0.5),
        'od_w_out': normal((n_odd, ODD_OUT, D), ODD_OUT ** -0.5),
        'swa_sink': normal((n_odd, SWA_HEADS), 1.0),
        'ret_log_decay': base_decay + normal((n_odd, 2, RET_HEADS), 0.05),
        'ret_gn_g': gain((n_odd, RET_V)),
        'ret_gn_b': normal((n_odd, RET_V), 0.02),
        'moe_router': normal((n_odd, D, N_EXPERTS), D ** -0.5),
        'moe_w1': normal((n_odd, N_EXPERTS, D, D_FF_EXPERT), D ** -0.5),
        'moe_w3': normal((n_odd, N_EXPERTS, D, D_FF_EXPERT), D ** -0.5),
        'moe_w2': normal((n_odd, N_EXPERTS, D_FF_EXPERT, D), D_FF_EXPERT ** -0.5),
        'final_g': gain((D,)),
    }


def reference(x, c, ctx, c_ctx, ada_w, ada_b, norm_attn_g, norm_ffn_g,
              ev_w_in, ev_w_out, na_rpb, ssm_conv_w, ssm_conv_b, ssm_dt_bias, ssm_a_log, ssm_d, ssm_norm_g,
              ffn_w1, ffn_w3, ffn_w2,
              od_w_in, od_w_out, swa_sink, ret_log_decay, ret_gn_g, ret_gn_b,
              moe_router, moe_w1, moe_w3, moe_w2, final_g):
    xl, xc = x, ctx
    for l in range(DEPTH):
        i = l // 2
        need_ctx = l < DEPTH - 1
        mod_l = (jax.nn.silu(c) @ ada_w[l] + ada_b[l])[:, None, :]
        mod_c = (jax.nn.silu(c_ctx) @ ada_w[l] + ada_b[l])[None, None, :]
        sh1_l, sc1_l, g1_l, sh2_l, sc2_l, g2_l = jnp.split(mod_l, 6, axis=-1)
        sh1_c, sc1_c, g1_c, sh2_c, sc2_c, g2_c = jnp.split(mod_c, 6, axis=-1)
        hl = modulate(rmsnorm(xl, norm_attn_g[l]), sh1_l, sc1_l)
        hc = modulate(rmsnorm(xc, norm_attn_g[l]), sh1_c, sc1_c)
        if l % 2 == 0:
            yc, yl = even_mixer(hc, hl, ev_w_in[i], ev_w_out[i], na_rpb[i], ssm_conv_w[i], ssm_conv_b[i],
                                ssm_dt_bias[i], ssm_a_log[i], ssm_d[i], ssm_norm_g[i], need_ctx)
            ffn = functools.partial(swiglu, w1=ffn_w1[i], w3=ffn_w3[i], w2=ffn_w2[i])
        else:
            yc, yl = odd_mixer(hc, hl, od_w_in[i], od_w_out[i], swa_sink[i], ret_log_decay[i],
                               ret_gn_g[i], ret_gn_b[i], need_ctx)
            ffn = functools.partial(moe_swiglu, router=moe_router[i], w1=moe_w1[i], w3=moe_w3[i], w2=moe_w2[i])
        xl = xl + g1_l * yl
        xl = xl + g2_l * ffn(modulate(rmsnorm(xl, norm_ffn_g[l]), sh2_l, sc2_l))
        if need_ctx:
            xc = xc + g1_c * yc
            xc = xc + g2_c * ffn(modulate(rmsnorm(xc, norm_ffn_g[l]), sh2_c, sc2_c))
    return rmsnorm(xl, final_g)
```

```python
import functools
import math

import numpy as np
import jax
import jax.numpy as jnp
from jax import lax
from jax.experimental import pallas as pl
from jax.experimental.pallas import tpu as pltpu

F32 = jnp.float32
BF16 = jnp.bfloat16

D = 1024
SEQ = 2048
CTX = 256
GRID_W = 64
HD = 64
EPS = 1e-6
ROPE_BASE = 10000.0
NEG = -1e30

NA_HEADS = 8
NA_ROWS = 8
NA_COLS = 16
NA_QROWS = 8
NA_KROWS = 16
SSM_HEADS = 16
SSM_INNER = 1024
SSM_STATE = 128
SSM_CONV = 5
SSM_CONV_CH = 1536
CHUNK = 128
SWA_HEADS = 8
SWA_WINDOW = 128
SWA_QT = 256
SWA_KT = 512
RET_HEADS = 8
D_FF = 2816
N_EXPERTS = 8
D_FF_EXPERT = 3584

LANES = 128
VMEM_LIMIT = 56 * 1024 * 1024


def _cp(sem, vmem=VMEM_LIMIT):
    return pltpu.CompilerParams(dimension_semantics=sem, vmem_limit_bytes=vmem)


def _sigmoid(x):
    return 1.0 / (1.0 + jnp.exp(-x))


def _silu(x):
    return x * _sigmoid(x)


def _mod_row(start, nctx):
    return jnp.where(start < nctx, 0, 1 + (start - nctx) // SEQ)


def _mod_kernel(c_ref, w_ref, b_ref, o_ref):
    c = c_ref[...]
    h = _silu(c).astype(BF16)
    o_ref[0] = jnp.dot(h, w_ref[0].astype(BF16), preferred_element_type=F32) + b_ref[0]


def _modulation(cin, ada_w, ada_b):
    depth, _, n6 = ada_w.shape
    rp = cin.shape[0]
    tn = 1024
    return pl.pallas_call(
        _mod_kernel,
        out_shape=jax.ShapeDtypeStruct((depth, rp, n6), F32),
        grid=(depth, n6 // tn),
        in_specs=[
            pl.BlockSpec((rp, D), lambda l, j: (0, 0)),
            pl.BlockSpec((1, D, tn), lambda l, j: (l, 0, j)),
            pl.BlockSpec((1, 1, tn), lambda l, j: (l, 0, j)),
        ],
        out_specs=pl.BlockSpec((1, rp, tn), lambda l, j: (l, 0, j)),
        compiler_params=_cp(("arbitrary", "arbitrary")),
        name="adaln_mod",
    )(cin, ada_w, ada_b.reshape(depth, 1, n6))


def _norm_mod(x, g, m, k):
    ms = jnp.mean(x * x, axis=-1, keepdims=True)
    y = x * lax.rsqrt(ms + EPS) * g
    return y * (1.0 + m[k + 1:k + 2]) + m[k:k + 1]


def _swap32(r):
    lane = lax.broadcasted_iota(jnp.int32, r.shape, 1)
    return jnp.where((lane % 64) < 32, pltpu.roll(r, 96, 1), pltpu.roll(r, 32, 1))


def _proj_kernel(x_ref, g_ref, mod_ref, w_ref, *rest, rope):
    if rope:
        c_ref, s_ref, o_ref, h_ref = rest
    else:
        o_ref, h_ref = rest

    @pl.when(pl.program_id(1) == 0)
    def _():
        h_ref[...] = _norm_mod(x_ref[...], g_ref[...], mod_ref[0], 0).astype(BF16)

    r = jnp.dot(h_ref[...], w_ref[...], preferred_element_type=F32)
    if rope:
        c = c_ref[...]
        s = s_ref[...]
        for k in range(r.shape[1] // LANES):
            rk = r[:, k * LANES:(k + 1) * LANES]
            o_ref[:, k * LANES:(k + 1) * LANES] = (rk * c + _swap32(rk) * s).astype(o_ref.dtype)
    else:
        o_ref[...] = r.astype(o_ref.dtype)


def _proj(x, g, mod, w, out_dtype, nctx, tm, tn, rope=None, name="proj"):
    rows = x.shape[0]
    n = w.shape[1]
    in_specs = [
        pl.BlockSpec((tm, D), lambda i, j: (i, 0)),
        pl.BlockSpec((1, D), lambda i, j: (0, 0)),
        pl.BlockSpec((1, 6, D), lambda i, j: (_mod_row(i * tm, nctx), 0, 0)),
        pl.BlockSpec((D, tn), lambda i, j: (0, j)),
    ]
    args = [x, g.reshape(1, D), mod, w]
    if rope is not None:
        nct = nctx // tm
        per = SEQ // tm

        def tab_map(i, j):
            return (jnp.where(i < nct, 0, 1 + (i - nct) % per), 0)
        in_specs += [pl.BlockSpec((tm, LANES), tab_map), pl.BlockSpec((tm, LANES), tab_map)]
        args += list(rope)
    return pl.pallas_call(
        functools.partial(_proj_kernel, rope=rope is not None),
        out_shape=jax.ShapeDtypeStruct((rows, n), out_dtype),
        grid=(rows // tm, n // tn),
        in_specs=in_specs,
        out_specs=pl.BlockSpec((tm, tn), lambda i, j: (i, j)),
        scratch_shapes=[pltpu.VMEM((tm, D), BF16)],
        compiler_params=_cp(("parallel", "arbitrary")),
        name=name,
    )(*args)


def _head_mask(x, a):
    lane = lax.broadcasted_iota(jnp.int32, (1, LANES), 1)
    keep = (lane < HD) if a == 0 else (lane >= HD)
    return jnp.where(keep, x, jnp.zeros_like(x))


def _nt(a, b):
    return lax.dot_general(a, b, (((1,), (1,)), ((), ())), preferred_element_type=F32)


def _tn(a, b):
    return lax.dot_general(a, b, (((0,), (0,)), ((), ())), preferred_element_type=F32)


def _two_part_attention(q, kc, vc, kl, vl, bias_fn, sink_fn):
    lane = lax.broadcasted_iota(jnp.int32, (1, LANES), 1)
    outs = []
    for a in range(2):
        qa = _head_mask(q, a) * jnp.asarray(HD ** -0.5, q.dtype)
        s_c = _nt(qa, kc)
        s_l = bias_fn(a, _nt(qa, kl))
        m = jnp.maximum(jnp.max(s_c, axis=-1, keepdims=True), jnp.max(s_l, axis=-1, keepdims=True))
        sink = sink_fn(a)
        if sink is not None:
            m = jnp.maximum(m, sink)
        p_c = jnp.exp(s_c - m)
        p_l = jnp.exp(s_l - m)
        den = jnp.sum(p_c, axis=-1, keepdims=True) + jnp.sum(p_l, axis=-1, keepdims=True)
        if sink is not None:
            den = den + jnp.exp(sink - m)
        o = (jnp.dot(p_c.astype(BF16), vc, preferred_element_type=F32)
             + jnp.dot(p_l.astype(BF16), vl, preferred_element_type=F32))
        outs.append(o / den)
    return jnp.where(lane < HD, outs[0], outs[1])


def _na_window_start(t):
    return jnp.clip(NA_QROWS * t - NA_ROWS // 2, 0, SEQ // GRID_W - NA_KROWS)


def _na_kernel(q_ref, kl_ref, vl_ref, kc_ref, vc_ref, bias_ref, o_ref):
    t = pl.program_id(1)
    w0 = pl.multiple_of(_na_window_start(t) * GRID_W, GRID_W)
    nk = NA_KROWS * GRID_W
    kl = kl_ref[pl.ds(w0, nk), :]
    vl = vl_ref[pl.ds(w0, nk), :]
    o = _two_part_attention(q_ref[...], kc_ref[...], vc_ref[...], kl, vl,
                            lambda a, s: s + bias_ref[a, 0], lambda a: None)
    o_ref[...] = o.astype(o_ref.dtype)


def _na_bias_table(rpb):
    rows = SEQ // GRID_W
    nt = rows // NA_QROWS
    t = np.arange(nt)[:, None, None, None, None]
    rq = np.arange(NA_QROWS)[None, :, None, None, None]
    c = np.arange(GRID_W)[None, None, :, None, None]
    rk = np.arange(NA_KROWS)[None, None, None, :, None]
    kc = np.arange(GRID_W)[None, None, None, None, :]
    r = NA_QROWS * t + rq
    w0 = np.clip(NA_QROWS * t - NA_ROWS // 2, 0, rows - NA_KROWS)
    kr = w0 + rk
    r0 = np.clip(r - NA_ROWS // 2, 0, rows - NA_ROWS)
    c0 = np.clip(c - NA_COLS // 2, 0, GRID_W - NA_COLS)
    ok = (kr >= r0) & (kr < r0 + NA_ROWS) & (kc >= c0) & (kc < c0 + NA_COLS)
    drow = np.clip(kr - r + NA_ROWS - 1, 0, 2 * NA_ROWS - 2)
    dcol = np.clip(kc - c + NA_COLS - 1, 0, 2 * NA_COLS - 2)
    shape = (nt, NA_QROWS, GRID_W, NA_KROWS, GRID_W)
    ok = np.broadcast_to(ok, shape).reshape(nt, NA_QROWS * GRID_W, NA_KROWS * GRID_W)
    flat = np.broadcast_to(drow * (2 * NA_COLS - 1) + dcol, shape).reshape(ok.shape)
    tab = jnp.take(rpb.reshape(NA_HEADS, -1), jnp.asarray(flat.astype(np.int32)), axis=1)
    return jnp.where(jnp.asarray(ok)[None], tab, NEG)


def _na_attention(big, bias, nb, nctx):
    rows = big.shape[0]
    tq = NA_QROWS * GRID_W
    ntile = SEQ // tq
    nct_q = nctx // tq
    lat0 = nctx // SEQ
    return pl.pallas_call(
        _na_kernel,
        out_shape=jax.ShapeDtypeStruct((rows, NA_HEADS * HD), BF16),
        grid=(NA_HEADS // 2, ntile, nb),
        in_specs=[
            pl.BlockSpec((tq, LANES), lambda h, t, b: (nct_q + b * ntile + t, 8 + h)),
            pl.BlockSpec((SEQ, LANES), lambda h, t, b: (lat0 + b, 12 + h)),
            pl.BlockSpec((SEQ, LANES), lambda h, t, b: (lat0 + b, 16 + h)),
            pl.BlockSpec((CTX, LANES), lambda h, t, b: (b, 12 + h)),
            pl.BlockSpec((CTX, LANES), lambda h, t, b: (b, 16 + h)),
            pl.BlockSpec((2, 1, tq, NA_KROWS * GRID_W), lambda h, t, b: (h, t, 0, 0)),
        ],
        out_specs=pl.BlockSpec((tq, LANES), lambda h, t, b: (nct_q + b * ntile + t, h)),
        compiler_params=_cp(("arbitrary", "arbitrary", "arbitrary")),
        name="na_attention",
    )(big, big, big, big, big, bias)


def _ctx_attn_kernel(q_ref, k_ref, v_ref, prev_ref, o_ref):
    del prev_ref
    lane = lax.broadcasted_iota(jnp.int32, (1, LANES), 1)
    q = q_ref[...]
    k = k_ref[...]
    v = v_ref[...]
    outs = []
    for a in range(2):
        qa = _head_mask(q, a) * jnp.asarray(HD ** -0.5, q.dtype)
        s = _nt(qa, k)
        m = jnp.max(s, axis=-1, keepdims=True)
        p = jnp.exp(s - m)
        den = jnp.sum(p, axis=-1, keepdims=True)
        outs.append(jnp.dot(p.astype(BF16), v, preferred_element_type=F32) / den)
    o_ref[...] = jnp.where(lane < HD, outs[0], outs[1]).astype(o_ref.dtype)


def _ctx_attention(big, attn_out, nb):
    return pl.pallas_call(
        _ctx_attn_kernel,
        out_shape=jax.ShapeDtypeStruct(attn_out.shape, attn_out.dtype),
        grid=(nb, NA_HEADS // 2),
        in_specs=[
            pl.BlockSpec((CTX, LANES), lambda b, h: (b, 8 + h)),
            pl.BlockSpec((CTX, LANES), lambda b, h: (b, 12 + h)),
            pl.BlockSpec((CTX, LANES), lambda b, h: (b, 16 + h)),
            pl.BlockSpec(memory_space=pl.ANY),
        ],
        out_specs=pl.BlockSpec((CTX, LANES), lambda b, h: (b, h)),
        input_output_aliases={3: 0},
        compiler_params=_cp(("arbitrary", "arbitrary")),
        name="ctx_attention",
    )(big, big, big, attn_out)


def _swa_kernel(q_ref, kl_ref, vl_ref, kc_ref, vc_ref, sink_ref, o_ref):
    h2 = pl.program_id(1)
    n = pl.program_id(2)
    start = pl.multiple_of(jnp.clip(SWA_QT * n - SWA_WINDOW, 0, SEQ - SWA_KT), SWA_WINDOW)
    kl = kl_ref[pl.ds(start, SWA_KT), :]
    vl = vl_ref[pl.ds(start, SWA_KT), :]
    qpos = SWA_QT * n + lax.broadcasted_iota(jnp.int32, (SWA_QT, 1), 0)
    kpos = start + lax.broadcasted_iota(jnp.int32, (1, SWA_KT), 1)
    ok = jnp.abs(kpos - qpos) <= SWA_WINDOW

    def sink(a):
        return sink_ref[pl.ds(2 * h2 + a, 1), :][:, :1]

    o = _two_part_attention(q_ref[...], kc_ref[...], vc_ref[...], kl, vl,
                            lambda a, s: jnp.where(ok, s, NEG), sink)
    o_ref[...] = o.astype(o_ref.dtype)


def _swa_attention(rp, pp, sink, nb, nctx):
    nq = SEQ // SWA_QT
    nct_q = nctx // SWA_QT
    lat0 = nctx // SEQ
    sinkv = jnp.broadcast_to(sink.astype(F32)[:, None], (SWA_HEADS, LANES))
    return pl.pallas_call(
        _swa_kernel,
        out_shape=jax.ShapeDtypeStruct((nb * SEQ, SWA_HEADS * HD), BF16),
        grid=(nb, SWA_HEADS // 2, nq),
        in_specs=[
            pl.BlockSpec((SWA_QT, LANES), lambda b, h, n: (nct_q + b * nq + n, h)),
            pl.BlockSpec((SEQ, LANES), lambda b, h, n: (lat0 + b, 4 + h // 2)),
            pl.BlockSpec((SEQ, LANES), lambda b, h, n: (lat0 + b, h // 2)),
            pl.BlockSpec((CTX, LANES), lambda b, h, n: (b, 4 + h // 2)),
            pl.BlockSpec((CTX, LANES), lambda b, h, n: (b, h // 2)),
            pl.BlockSpec((SWA_HEADS, LANES), lambda b, h, n: (0, 0)),
        ],
        out_specs=pl.BlockSpec((SWA_QT, LANES), lambda b, h, n: (b * nq + n, h)),
        compiler_params=_cp(("arbitrary", "arbitrary", "arbitrary")),
        name="swa_attention",
    )(rp, rp, pp, rp, pp, sinkv)


def _conv_kernel(xp_ref, x_ref, xn_ref, w_ref, b_ref, o_ref, *, nctx_blocks, per_seq):
    i = pl.program_id(0)
    tb = x_ref.shape[0]
    k = i - nctx_blocks
    has_prev = jnp.logical_and(i >= nctx_blocks, k % per_seq != 0)
    has_next = jnp.logical_and(i >= nctx_blocks, k % per_seq != per_seq - 1)
    x = x_ref[...].astype(F32)
    xp = jnp.where(has_prev, xp_ref[...].astype(F32), 0.0)
    xn = jnp.where(has_next, xn_ref[...].astype(F32), 0.0)
    row = lax.broadcasted_iota(jnp.int32, (tb, 1), 0)
    pad = (SSM_CONV - 1) // 2
    acc = x * w_ref[pad:pad + 1, :] + b_ref[...]
    for s in range(-pad, pad + 1):
        if s == 0:
            continue
        sh = (-s) % tb
        other = xp if s < 0 else xn
        inside = jnp.logical_and(row + s >= 0, row + s < tb)
        tap = jnp.where(inside, pltpu.roll(x, sh, 0), pltpu.roll(other, sh, 0))
        acc = acc + tap * w_ref[pad + s:pad + s + 1, :]
    o_ref[...] = _silu(acc).astype(o_ref.dtype)


def _conv_silu(big, conv_w, conv_b, nctx):
    rows = big.shape[0]
    tb = CTX
    tc = 512
    nblk = rows // tb
    return pl.pallas_call(
        functools.partial(_conv_kernel, nctx_blocks=nctx // tb, per_seq=SEQ // tb),
        out_shape=jax.ShapeDtypeStruct((rows, SSM_CONV_CH), BF16),
        grid=(nblk, SSM_CONV_CH // tc),
        in_specs=[
            pl.BlockSpec((tb, tc), lambda i, c: (jnp.maximum(i - 1, 0), 5 + c)),
            pl.BlockSpec((tb, tc), lambda i, c: (i, 5 + c)),
            pl.BlockSpec((tb, tc), lambda i, c: (jnp.minimum(i + 1, nblk - 1), 5 + c)),
            pl.BlockSpec((SSM_CONV, tc), lambda i, c: (0, c)),
            pl.BlockSpec((1, tc), lambda i, c: (0, c)),
        ],
        out_specs=pl.BlockSpec((tb, tc), lambda i, c: (i, c)),
        compiler_params=_cp(("arbitrary", "arbitrary")),
        name="conv_silu",
    )(big, big, big, conv_w.reshape(SSM_CONV, SSM_CONV_CH), conv_b.reshape(1, SSM_CONV_CH))


def _chunk_block(b, d, s, nctx):
    ncc = CTX // CHUNK
    nlc = SEQ // CHUNK
    cc = jnp.where(d == 0, s, ncc - 1 - s)
    lc = jnp.where(d == 0, s - ncc, nlc + ncc - 1 - s)
    return jnp.where(s < ncc, ncc * b + cc, nctx // CHUNK + nlc * b + lc)


def _split3(a):
    hi = a.astype(BF16)
    r1 = a - hi.astype(F32)
    mid = r1.astype(BF16)
    lo = (r1 - mid.astype(F32)).astype(BF16)
    return hi, mid, lo


def _softplus(x):
    return jnp.maximum(x, 0.0) + jnp.log(1.0 + jnp.exp(-jnp.abs(x)))


def _ssd_kernel(xs_ref, b_ref, c_ref, dt_ref, dtt_ref, bias_ref, biast_ref, alog_ref, alogt_ref,
                o_ref, s_ref):
    d = pl.program_id(1)
    step = pl.program_id(2)
    q = CHUNK

    @pl.when(step == 0)
    def _():
        s_ref[...] = jnp.zeros_like(s_ref)

    ii = lax.broadcasted_iota(jnp.int32, (q, q), 0)
    jj = lax.broadcasted_iota(jnp.int32, (q, q), 1)
    dist = jnp.where(d == 0, ii - jj, jj - ii)
    causal = dist >= 0
    tri = jnp.where(causal, 1.0, 0.0).astype(BF16)
    trit = jnp.where(dist <= 0, 1.0, 0.0).astype(BF16)

    dt = _softplus(dt_ref[0] + bias_ref[0])
    a = dt * (-jnp.exp(alog_ref[0]))
    dtt = _softplus(dtt_ref[0] + biast_ref[0])
    at = dtt * (-jnp.exp(alogt_ref[0]))
    acum = sum(jnp.dot(tri, p, preferred_element_type=F32) for p in _split3(a))
    acumt = sum(jnp.dot(p, trit, preferred_element_type=F32) for p in _split3(at))
    atot = jnp.sum(a, axis=0, keepdims=True)

    lane = lax.broadcasted_iota(jnp.int32, (1, LANES), 1)
    first = lane < HD
    ngroups = b_ref.shape[1] // SSM_STATE
    pairs_per_group = SSM_HEADS // 2 // ngroups
    for g in range(ngroups):
        bg = b_ref[:, g * SSM_STATE:(g + 1) * SSM_STATE]
        cg = c_ref[:, g * SSM_STATE:(g + 1) * SSM_STATE]
        cb = _nt(cg, bg)
        for pp in range(pairs_per_group):
            p = g * pairs_per_group + pp
            ha, hb = 2 * p, 2 * p + 1
            x = xs_ref[:, p * LANES:(p + 1) * LANES].astype(F32)
            xdt = x * jnp.where(first, dt[:, ha:ha + 1], dt[:, hb:hb + 1])
            xdt16 = xdt.astype(BF16)
            ys = []
            for h in (ha, hb):
                seg = acum[:, h:h + 1] - acumt[h:h + 1, :]
                dec = jnp.exp(jnp.where(causal, seg, NEG))
                ys.append(jnp.dot((dec * cb).astype(BF16), xdt16, preferred_element_type=F32))
            ydiag = jnp.where(first, ys[0], ys[1])
            st = s_ref[p]
            grow = jnp.where(first, jnp.exp(acum[:, ha:ha + 1]), jnp.exp(acum[:, hb:hb + 1]))
            yoff = jnp.dot(cg, st.astype(BF16), preferred_element_type=F32) * grow
            o_ref[0, :, p * LANES:(p + 1) * LANES] = (ydiag + yoff).astype(o_ref.dtype)
            wst = jnp.where(first, jnp.exp(atot[:, ha:ha + 1] - acum[:, ha:ha + 1]),
                            jnp.exp(atot[:, hb:hb + 1] - acum[:, hb:hb + 1]))
            keep = jnp.where(first, jnp.exp(atot[:, ha:ha + 1]), jnp.exp(atot[:, hb:hb + 1]))
            s_ref[p] = st * keep + _tn(bg, (xdt * wst).astype(BF16))


def _ssd_scan(xbc, dt_raw, dt_bias, a_log, nb, nctx):
    rows = xbc.shape[0]
    nsteps = (CTX + SEQ) // CHUNK
    dtd = dt_raw.reshape(rows, 2, SSM_HEADS).transpose(1, 0, 2)
    dtt = dtd.transpose(0, 2, 1)
    blk = functools.partial(_chunk_block, nctx=nctx)
    par = lambda shape: pl.BlockSpec(shape, lambda b, d, s: (d, 0, 0))
    return pl.pallas_call(
        _ssd_kernel,
        out_shape=jax.ShapeDtypeStruct((2, rows, SSM_INNER), BF16),
        grid=(nb, 2, nsteps),
        in_specs=[
            pl.BlockSpec((CHUNK, SSM_INNER), lambda b, d, s: (blk(b, d, s), 0)),
            pl.BlockSpec((CHUNK, 2 * SSM_STATE), lambda b, d, s: (blk(b, d, s), 4)),
            pl.BlockSpec((CHUNK, 2 * SSM_STATE), lambda b, d, s: (blk(b, d, s), 5)),
            pl.BlockSpec((1, CHUNK, SSM_HEADS), lambda b, d, s: (d, blk(b, d, s), 0)),
            pl.BlockSpec((1, SSM_HEADS, CHUNK), lambda b, d, s: (d, 0, blk(b, d, s))),
            par((1, 1, SSM_HEADS)), par((1, SSM_HEADS, 1)),
            par((1, 1, SSM_HEADS)), par((1, SSM_HEADS, 1)),
        ],
        out_specs=pl.BlockSpec((1, CHUNK, SSM_INNER), lambda b, d, s: (d, blk(b, d, s), 0)),
        scratch_shapes=[pltpu.VMEM((SSM_HEADS // 2, SSM_STATE, LANES), F32)],
        compiler_params=_cp(("arbitrary", "arbitrary", "arbitrary")),
        name="ssd_scan",
    )(xbc, xbc, xbc, dtd, dtt,
      dt_bias.reshape(2, 1, SSM_HEADS), dt_bias.reshape(2, SSM_HEADS, 1),
      a_log.reshape(2, 1, SSM_HEADS), a_log.reshape(2, SSM_HEADS, 1))


def _ret_kernel(c_ref, b_ref, x_ref, ld_ref, o_ref, s_ref):
    d = pl.program_id(1)
    step = pl.program_id(2)
    q = CHUNK

    @pl.when(step == 0)
    def _():
        s_ref[...] = jnp.zeros_like(s_ref)

    lg = jnp.log(1.0 - jnp.exp(ld_ref[0]))
    ii = lax.broadcasted_iota(jnp.int32, (q, q), 0)
    jj = lax.broadcasted_iota(jnp.int32, (q, q), 1)
    dist = jnp.where(d == 0, ii - jj, jj - ii)
    causal = dist >= 0
    distf = dist.astype(F32)
    ri = lax.broadcasted_iota(jnp.int32, (q, 1), 0)
    npos = jnp.where(d == 0, ri + 1, q - ri).astype(F32)
    for h in range(RET_HEADS):
        lgh = lg[:, h:h + 1]
        pb, a = h // 2, h % 2
        ch = _head_mask(c_ref[:, pb * LANES:(pb + 1) * LANES], a) * jnp.asarray(HD ** -0.5, BF16)
        bh = _head_mask(b_ref[:, pb * LANES:(pb + 1) * LANES], a)
        cb = _nt(ch, bh)
        dec = jnp.exp(jnp.where(causal, distf * lgh, NEG))
        x = x_ref[:, h * LANES:(h + 1) * LANES]
        ydiag = jnp.dot((dec * cb).astype(BF16), x, preferred_element_type=F32)
        st = s_ref[h]
        yoff = jnp.dot(ch, st.astype(BF16), preferred_element_type=F32) * jnp.exp(npos * lgh)
        o_ref[0, :, h * LANES:(h + 1) * LANES] = (ydiag + yoff).astype(o_ref.dtype)
        xw = (x.astype(F32) * jnp.exp((q - npos) * lgh)).astype(BF16)
        s_ref[h] = st * jnp.exp(q * lgh) + _tn(bh, xw)


def _retention_scan(rp, pp, log_decay, nb, nctx):
    nsteps = (CTX + SEQ) // CHUNK
    ncc = CTX // CHUNK
    nlc = SEQ // CHUNK
    blk = functools.partial(_chunk_block, nctx=nctx)

    def oblk(b, d, s):
        lc = jnp.where(d == 0, s - ncc, nlc + ncc - 1 - s)
        return nlc * b + jnp.clip(lc, 0, nlc - 1)

    w = RET_HEADS * HD
    return pl.pallas_call(
        _ret_kernel,
        out_shape=jax.ShapeDtypeStruct((2, nb * SEQ, RET_HEADS * LANES), BF16),
        grid=(nb, 2, nsteps),
        in_specs=[
            pl.BlockSpec((CHUNK, w), lambda b, d, s: (blk(b, d, s), 2)),
            pl.BlockSpec((CHUNK, w), lambda b, d, s: (blk(b, d, s), 3)),
            pl.BlockSpec((CHUNK, RET_HEADS * LANES), lambda b, d, s: (blk(b, d, s), 1)),
            pl.BlockSpec((1, 1, RET_HEADS), lambda b, d, s: (d, 0, 0)),
        ],
        out_specs=pl.BlockSpec((1, CHUNK, RET_HEADS * LANES), lambda b, d, s: (d, oblk(b, d, s), 0)),
        scratch_shapes=[pltpu.VMEM((RET_HEADS, LANES, LANES), F32)],
        compiler_params=_cp(("arbitrary", "arbitrary", "arbitrary")),
        name="retention_scan",
    )(rp, rp, pp, log_decay.astype(F32).reshape(2, 1, RET_HEADS))


def _ssm_out_kernel(yf_ref, yb_ref, xs_ref, z_ref, dsk_ref, g_ref, o_ref):
    y = yf_ref[0].astype(F32) + yb_ref[0].astype(F32) + xs_ref[...].astype(F32) * dsk_ref[...]
    yz = y * _silu(z_ref[...].astype(F32))
    half = SSM_INNER // 2
    for g in range(2):
        seg = yz[:, g * half:(g + 1) * half]
        ms = jnp.mean(seg * seg, axis=-1, keepdims=True)
        o_ref[:, g * half:(g + 1) * half] = (
            seg * lax.rsqrt(ms + EPS) * g_ref[:, g * half:(g + 1) * half]).astype(o_ref.dtype)


def _ssm_out(y2, xbc, big, d_skip, gn_g, tm):
    rows = xbc.shape[0]
    dsk = jnp.repeat(d_skip.astype(F32), HD).reshape(1, SSM_INNER)
    return pl.pallas_call(
        _ssm_out_kernel,
        out_shape=jax.ShapeDtypeStruct((rows, SSM_INNER), BF16),
        grid=(rows // tm,),
        in_specs=[
            pl.BlockSpec((1, tm, SSM_INNER), lambda i: (0, i, 0)),
            pl.BlockSpec((1, tm, SSM_INNER), lambda i: (1, i, 0)),
            pl.BlockSpec((tm, SSM_INNER), lambda i: (i, 0)),
            pl.BlockSpec((tm, SSM_INNER), lambda i: (i, 0)),
            pl.BlockSpec((1, SSM_INNER), lambda i: (0, 0)),
            pl.BlockSpec((1, SSM_INNER), lambda i: (0, 0)),
        ],
        out_specs=pl.BlockSpec((tm, SSM_INNER), lambda i: (i, 0)),
        compiler_params=_cp(("parallel",)),
        name="ssm_out",
    )(y2, y2, xbc, big, dsk, gn_g.reshape(1, SSM_INNER))


def _ret_out_kernel(rf_ref, rb_ref, rg_ref, g_ref, b_ref, o_ref):
    y = rf_ref[0].astype(F32) + rb_ref[0].astype(F32)
    for h in range(RET_HEADS):
        sl = slice(h * LANES, (h + 1) * LANES)
        seg = y[:, sl]
        mu = jnp.mean(seg, axis=-1, keepdims=True)
        cen = seg - mu
        var = jnp.mean(cen * cen, axis=-1, keepdims=True)
        yn = cen * lax.rsqrt(var + EPS) * g_ref[:, sl] + b_ref[:, sl]
        o_ref[:, sl] = (_silu(rg_ref[:, sl].astype(F32)) * yn).astype(o_ref.dtype)


def _ret_out(y2, pp, gn_g, gn_b, nb, nctx, tm):
    rows = nb * SEQ
    w = RET_HEADS * LANES
    off = nctx // tm
    return pl.pallas_call(
        _ret_out_kernel,
        out_shape=jax.ShapeDtypeStruct((rows, w), BF16),
        grid=(rows // tm,),
        in_specs=[
            pl.BlockSpec((1, tm, w), lambda i: (0, i, 0)),
            pl.BlockSpec((1, tm, w), lambda i: (1, i, 0)),
            pl.BlockSpec((tm, w), lambda i: (off + i, 2)),
            pl.BlockSpec((1, w), lambda i: (0, 0)),
            pl.BlockSpec((1, w), lambda i: (0, 0)),
        ],
        out_specs=pl.BlockSpec((tm, w), lambda i: (i, 0)),
        compiler_params=_cp(("parallel",)),
        name="ret_out",
    )(y2, y2, pp, gn_g.reshape(1, w), gn_b.reshape(1, w))


def _outproj_kernel(x_ref, a1_ref, a2_ref, w1_ref, w2_ref, mod_ref, o_ref):
    y = (jnp.dot(a1_ref[...], w1_ref[...], preferred_element_type=F32)
         + jnp.dot(a2_ref[...], w2_ref[...], preferred_element_type=F32))
    o_ref[...] = x_ref[...] + mod_ref[0][2:3] * y


def _outproj(x, a1, a2, w_out, mod, nctx, tm, x_off):
    rows = a1.shape[0]
    k1, k2 = a1.shape[1], a2.shape[1]
    w1 = w_out[:k1].astype(BF16)
    w2 = w_out[k1:].astype(BF16)
    return pl.pallas_call(
        _outproj_kernel,
        out_shape=jax.ShapeDtypeStruct((rows, D), F32),
        grid=(rows // tm,),
        in_specs=[
            pl.BlockSpec((tm, D), lambda i: (i + x_off, 0)),
            pl.BlockSpec((tm, k1), lambda i: (i, 0)),
            pl.BlockSpec((tm, k2), lambda i: (i, 0)),
            pl.BlockSpec((k1, D), lambda i: (0, 0)),
            pl.BlockSpec((k2, D), lambda i: (0, 0)),
            pl.BlockSpec((1, 6, D), lambda i: (_mod_row((i + x_off) * tm, nctx), 0, 0)),
        ],
        out_specs=pl.BlockSpec((tm, D), lambda i: (i, 0)),
        compiler_params=_cp(("parallel",)),
        name="outproj",
    )(x, a1, a2, w1, w2, mod)


def _top2_gates(logits):
    lane = lax.broadcasted_iota(jnp.int32, logits.shape, 1)
    valid = lane < N_EXPERTS
    l0 = jnp.where(valid, logits, NEG)
    m1 = jnp.max(l0, axis=-1, keepdims=True)
    i1 = jnp.min(jnp.where(l0 == m1, lane, LANES), axis=-1, keepdims=True)
    l1 = jnp.where(lane == i1, NEG, l0)
    m2 = jnp.max(l1, axis=-1, keepdims=True)
    i2 = jnp.min(jnp.where(l1 == m2, lane, LANES), axis=-1, keepdims=True)
    e2 = jnp.exp(m2 - m1)
    g1 = 1.0 / (1.0 + e2)
    g2 = e2 / (1.0 + e2)
    return jnp.where(lane == i1, g1, 0.0) + jnp.where(lane == i2, g2, 0.0)


def _ffn_kernel(x_ref, g_ref, mod_ref, *rest, routed, final):
    rest = list(rest)
    router_ref = rest.pop(0) if routed else None
    w1_ref, w3_ref, w2_ref = rest[:3]
    rest = rest[3:]
    fg_ref = rest.pop(0) if final else None
    o_ref, h_ref, acc_ref = rest[:3]
    gate_ref = rest[3] if routed else None
    e = pl.program_id(1)
    f = pl.program_id(2)
    first = jnp.logical_and(e == 0, f == 0)
    last = jnp.logical_and(e == pl.num_programs(1) - 1, f == pl.num_programs(2) - 1)

    @pl.when(first)
    def _():
        h = _norm_mod(x_ref[...], g_ref[...], mod_ref[0], 3)
        h16 = h.astype(BF16)
        h_ref[...] = h16
        acc_ref[...] = jnp.zeros_like(acc_ref)
        if routed:
            hl = (h - h16.astype(F32)).astype(BF16)
            r = router_ref[...]
            rh = r.astype(BF16)
            rl = (r - rh.astype(F32)).astype(BF16)
            logits = (jnp.dot(h16, rh, preferred_element_type=F32)
                      + jnp.dot(h16, rl, preferred_element_type=F32)
                      + jnp.dot(hl, rh, preferred_element_type=F32))
            gate_ref[...] = _top2_gates(logits)

    h16 = h_ref[...]
    u = _silu(jnp.dot(h16, w1_ref[0], preferred_element_type=F32)) * jnp.dot(
        h16, w3_ref[0], preferred_element_type=F32)
    if routed:
        lane = lax.broadcasted_iota(jnp.int32, gate_ref.shape, 1)
        ge = jnp.sum(jnp.where(lane == e, gate_ref[...], 0.0), axis=-1, keepdims=True)
        u = u * ge
    acc_ref[...] += jnp.dot(u.astype(BF16), w2_ref[0], preferred_element_type=F32)

    @pl.when(last)
    def _():
        y = x_ref[...] + mod_ref[0][5:6] * acc_ref[...]
        if final:
            ms = jnp.mean(y * y, axis=-1, keepdims=True)
            y = y * lax.rsqrt(ms + EPS) * fg_ref[...]
        o_ref[...] = y


def _ffn(x, g, mod, w1, w3, w2, nctx, tm, tf, x_off=0, rows=None, router=None, final_g=None):
    rows = x.shape[0] if rows is None else rows
    ne, _, dff = w1.shape
    routed = router is not None
    final = final_g is not None
    in_specs = [
        pl.BlockSpec((tm, D), lambda i, e, f: (i + x_off, 0)),
        pl.BlockSpec((1, D), lambda i, e, f: (0, 0)),
        pl.BlockSpec((1, 6, D), lambda i, e, f: (_mod_row((i + x_off) * tm, nctx), 0, 0)),
    ]
    args = [x, g.reshape(1, D), mod]
    if routed:
        in_specs.append(pl.BlockSpec((D, LANES), lambda i, e, f: (0, 0)))
        args.append(jnp.pad(router.astype(F32), ((0, 0), (0, LANES - ne))))
    in_specs += [
        pl.BlockSpec((1, D, tf), lambda i, e, f: (e, 0, f)),
        pl.BlockSpec((1, D, tf), lambda i, e, f: (e, 0, f)),
        pl.BlockSpec((1, tf, D), lambda i, e, f: (e, f, 0)),
    ]
    args += [w1, w3, w2]
    if final:
        in_specs.append(pl.BlockSpec((1, D), lambda i, e, f: (0, 0)))
        args.append(final_g.reshape(1, D))
    scratch = [pltpu.VMEM((tm, D), BF16), pltpu.VMEM((tm, D), F32)]
    if routed:
        scratch.append(pltpu.VMEM((tm, LANES), F32))
    return pl.pallas_call(
        functools.partial(_ffn_kernel, routed=routed, final=final),
        out_shape=jax.ShapeDtypeStruct((rows, D), F32),
        grid=(rows // tm, ne, dff // tf),
        in_specs=in_specs,
        out_specs=pl.BlockSpec((tm, D), lambda i, e, f: (i, 0)),
        scratch_shapes=scratch,
        compiler_params=_cp(("parallel", "arbitrary", "arbitrary")),
        name="moe_ffn" if routed else "dense_ffn",
    )(*args)


def _rope_tables(tm):
    t = np.arange(SEQ)
    n_freq = HD // 4
    inv = ROPE_BASE ** (-np.arange(n_freq, dtype=np.float64) / n_freq)
    ang = np.concatenate([(t // GRID_W)[:, None] * inv, (t % GRID_W)[:, None] * inv], axis=-1)
    cos = np.tile(np.cos(ang), (1, 4))
    sin = np.tile(np.concatenate([-np.sin(ang), np.sin(ang)], axis=-1), (1, 2))
    cos = np.concatenate([np.ones((tm, LANES)), cos], axis=0)
    sin = np.concatenate([np.zeros((tm, LANES)), sin], axis=0)
    return jnp.asarray(cos, F32), jnp.asarray(sin, F32)


def _dup_heads(w, nheads):
    w = w.reshape(D, nheads, 1, HD)
    return jnp.broadcast_to(w, (D, nheads, 2, HD)).reshape(D, nheads * 2 * HD)


def kernel(x, c, ctx, c_ctx, ada_w, ada_b, norm_attn_g, norm_ffn_g, ev_w_in, ev_w_out, na_rpb,
           ssm_conv_w, ssm_conv_b, ssm_dt_bias, ssm_a_log, ssm_d, ssm_norm_g, ffn_w1, ffn_w3, ffn_w2,
           od_w_in, od_w_out, swa_sink, ret_log_decay, ret_gn_g, ret_gn_b, moe_router, moe_w1,
           moe_w3, moe_w2, final_g):
    nb = x.shape[0]
    nctx = -(-(nb * CTX) // SEQ) * SEQ
    tm = 1024
    xs = jnp.concatenate([ctx.reshape(nb * CTX, D), jnp.zeros((nctx - nb * CTX, D), ctx.dtype),
                          x.reshape(nb * SEQ, D)], axis=0).astype(F32)

    rp = -(-(nb + 1) // 8) * 8
    cin = jnp.zeros((rp, D), F32).at[0].set(c_ctx).at[1:nb + 1].set(c)
    mod = _modulation(cin, ada_w, ada_b).reshape(2, rp, 6, D)

    w_in = ev_w_in[0]
    q_, k_, v_, z_, xbc_, dt_ = jnp.split(w_in, [512, 1024, 1536, 2560, 4096], axis=1)
    w_big = jnp.concatenate([z_, q_, k_, v_, xbc_], axis=1).astype(BF16)
    big = _proj(xs, norm_attn_g[0], mod[0], w_big, BF16, nctx, tm, 512, name="even_in_proj")
    dt_raw = _proj(xs, norm_attn_g[0], mod[0], dt_.astype(BF16), F32, nctx, tm, 2 * SSM_HEADS,
                   name="even_dt_proj")
    attn = _na_attention(big, _na_bias_table(na_rpb[0]), nb, nctx)
    attn = _ctx_attention(big, attn, nb)
    xbc = _conv_silu(big, ssm_conv_w[0], ssm_conv_b[0], nctx)
    y2 = _ssd_scan(xbc, dt_raw, ssm_dt_bias[0], ssm_a_log[0], nb, nctx)
    ys = _ssm_out(y2, xbc, big, ssm_d[0], ssm_norm_g[0], tm)
    xs = _outproj(xs, attn, ys, ev_w_out[0], mod[0], nctx, tm, 0)
    xs = _ffn(xs, norm_ffn_g[0], mod[0], ffn_w1.astype(BF16), ffn_w3.astype(BF16),
              ffn_w2.astype(BF16), nctx, tm, 256)

    w_in = od_w_in[0]
    q_, k_, v_, rq_, rk_, rv_, rg_ = jnp.split(w_in, [512, 640, 768, 1280, 1792, 2816], axis=1)
    w_rope = jnp.concatenate([q_, _dup_heads(k_, 2), jnp.zeros((D, 256), F32), rq_, rk_],
                             axis=1).astype(BF16)
    w_plain = jnp.concatenate([_dup_heads(v_, 2), jnp.zeros((D, 768), F32), rv_, rg_],
                              axis=1).astype(BF16)
    rope_tabs = _rope_tables(tm)
    rpj = _proj(xs, norm_attn_g[1], mod[1], w_rope, BF16, nctx, tm, 512, rope=rope_tabs,
                name="odd_rope_proj")
    ppj = _proj(xs, norm_attn_g[1], mod[1], w_plain, BF16, nctx, tm, 512, name="odd_plain_proj")
    yw = _swa_attention(rpj, ppj, swa_sink[0], nb, nctx)
    r2 = _retention_scan(rpj, ppj, ret_log_decay[0], nb, nctx)
    yr = _ret_out(r2, ppj, ret_gn_g[0], ret_gn_b[0], nb, nctx, tm)
    x_off = nctx // tm
    xl = _outproj(xs, yw, yr, od_w_out[0], mod[1], nctx, tm, x_off)
    out = _ffn(xl, norm_ffn_g[1], mod[1], moe_w1[0].astype(BF16), moe_w3[0].astype(BF16),
               moe_w2[0].astype(BF16), 0, tm, 512, router=moe_router[0], final_g=final_g)
    return out.reshape(nb, SEQ, D).astype(x.dtype)
```

```python
import functools
import math

import numpy as np
import jax
import jax.numpy as jnp
from jax import lax
from jax.experimental import pallas as pl
from jax.experimental.pallas import tpu as pltpu

F32 = jnp.float32
BF16 = jnp.bfloat16

D = 1024
SEQ = 2048
CTX = 256
GRID_W = 64
HD = 64
EPS = 1e-6
ROPE_BASE = 10000.0
NEG = -1e30

NA_HEADS = 8
NA_ROWS = 8
NA_COLS = 16
NA_QROWS = 8
NA_KROWS = 16
SSM_HEADS = 16
SSM_INNER = 1024
SSM_STATE = 128
SSM_CONV = 5
SSM_CONV_CH = 1536
CHUNK = 128
SWA_HEADS = 8
SWA_WINDOW = 128
SWA_QT = 256
SWA_KT = 512
RET_HEADS = 8
D_FF = 2816
N_EXPERTS = 8
D_FF_EXPERT = 3584

LANES = 128
VMEM_LIMIT = 56 * 1024 * 1024


def _cp(sem, vmem=VMEM_LIMIT):
    return pltpu.CompilerParams(dimension_semantics=sem, vmem_limit_bytes=vmem)


def _sigmoid(x):
    return 1.0 / (1.0 + jnp.exp(-x))


def _silu(x):
    return x * _sigmoid(x)


def _mod_row(start, nctx):
    return jnp.where(start < nctx, 0, 1 + (start - nctx) // SEQ)


def _mod_kernel(c_ref, w_ref, b_ref, o_ref):
    c = c_ref[...]
    h = _silu(c).astype(BF16)
    o_ref[0] = jnp.dot(h, w_ref[0].astype(BF16), preferred_element_type=F32) + b_ref[0]


def _modulation(cin, ada_w, ada_b):
    depth, _, n6 = ada_w.shape
    rp = cin.shape[0]
    tn = 1024
    return pl.pallas_call(
        _mod_kernel,
        out_shape=jax.ShapeDtypeStruct((depth, rp, n6), F32),
        grid=(depth, n6 // tn),
        in_specs=[
            pl.BlockSpec((rp, D), lambda l, j: (0, 0)),
            pl.BlockSpec((1, D, tn), lambda l, j: (l, 0, j)),
            pl.BlockSpec((1, 1, tn), lambda l, j: (l, 0, j)),
        ],
        out_specs=pl.BlockSpec((1, rp, tn), lambda l, j: (l, 0, j)),
        compiler_params=_cp(("arbitrary", "arbitrary")),
        name="adaln_mod",
    )(cin, ada_w, ada_b.reshape(depth, 1, n6))


def _norm_mod(x, g, m, k):
    ms = jnp.mean(x * x, axis=-1, keepdims=True)
    y = x * lax.rsqrt(ms + EPS) * g
    return y * (1.0 + m[k + 1:k + 2]) + m[k:k + 1]


def _swap32(r):
    lane = lax.broadcasted_iota(jnp.int32, r.shape, 1)
    return jnp.where((lane % 64) < 32, pltpu.roll(r, 96, 1), pltpu.roll(r, 32, 1))


def _proj_kernel(x_ref, g_ref, mod_ref, w_ref, *rest, rope):
    if rope:
        c_ref, s_ref, o_ref, h_ref = rest
    else:
        o_ref, h_ref = rest

    @pl.when(pl.program_id(1) == 0)
    def _():
        h_ref[...] = _norm_mod(x_ref[...], g_ref[...], mod_ref[0], 0).astype(BF16)

    r = jnp.dot(h_ref[...], w_ref[...], preferred_element_type=F32)
    if rope:
        c = c_ref[...]
        s = s_ref[...]
        for k in range(r.shape[1] // LANES):
            rk = r[:, k * LANES:(k + 1) * LANES]
            o_ref[:, k * LANES:(k + 1) * LANES] = (rk * c + _swap32(rk) * s).astype(o_ref.dtype)
    else:
        o_ref[...] = r.astype(o_ref.dtype)


def _proj(x, g, mod, w, out_dtype, nctx, tm, tn, rope=None, name="proj"):
    rows = x.shape[0]
    n = w.shape[1]
    in_specs = [
        pl.BlockSpec((tm, D), lambda i, j: (i, 0)),
        pl.BlockSpec((1, D), lambda i, j: (0, 0)),
        pl.BlockSpec((1, 6, D), lambda i, j: (_mod_row(i * tm, nctx), 0, 0)),
        pl.BlockSpec((D, tn), lambda i, j: (0, j)),
    ]
    args = [x, g.reshape(1, D), mod, w]
    if rope is not None:
        nct = nctx // tm
        per = SEQ // tm

        def tab_map(i, j):
            return (jnp.where(i < nct, 0, 1 + (i - nct) % per), 0)
        in_specs += [pl.BlockSpec((tm, LANES), tab_map), pl.BlockSpec((tm, LANES), tab_map)]
        args += list(rope)
    return pl.pallas_call(
        functools.partial(_proj_kernel, rope=rope is not None),
        out_shape=jax.ShapeDtypeStruct((rows, n), out_dtype),
        grid=(rows // tm, n // tn),
        in_specs=in_specs,
        out_specs=pl.BlockSpec((tm, tn), lambda i, j: (i, j)),
        scratch_shapes=[pltpu.VMEM((tm, D), BF16)],
        compiler_params=_cp(("parallel", "arbitrary")),
        name=name,
    )(*args)


def _head_mask(x, a):
    lane = lax.broadcasted_iota(jnp.int32, (1, LANES), 1)
    keep = (lane < HD) if a == 0 else (lane >= HD)
    return jnp.where(keep, x, jnp.zeros_like(x))


def _nt(a, b):
    return lax.dot_general(a, b, (((1,), (1,)), ((), ())), preferred_element_type=F32)


def _tn(a, b):
    return lax.dot_general(a, b, (((0,), (0,)), ((), ())), preferred_element_type=F32)


def _two_part_attention(q, kc, vc, kl, vl, bias_fn, sink_fn):
    lane = lax.broadcasted_iota(jnp.int32, (1, LANES), 1)
    outs = []
    for a in range(2):
        qa = _head_mask(q, a) * jnp.asarray(HD ** -0.5, q.dtype)
        s_c = _nt(qa, kc)
        s_l = bias_fn(a, _nt(qa, kl))
        m = jnp.maximum(jnp.max(s_c, axis=-1, keepdims=True), jnp.max(s_l, axis=-1, keepdims=True))
        sink = sink_fn(a)
        if sink is not None:
            m = jnp.maximum(m, sink)
        p_c = jnp.exp(s_c - m)
        p_l = jnp.exp(s_l - m)
        den = jnp.sum(p_c, axis=-1, keepdims=True) + jnp.sum(p_l, axis=-1, keepdims=True)
        if sink is not None:
            den = den + jnp.exp(sink - m)
        o = (jnp.dot(p_c.astype(BF16), vc, preferred_element_type=F32)
             + jnp.dot(p_l.astype(BF16), vl, preferred_element_type=F32))
        outs.append(o / den)
    return jnp.where(lane < HD, outs[0], outs[1])


def _na_window_start(t):
    return jnp.clip(NA_QROWS * t - NA_ROWS // 2, 0, SEQ // GRID_W - NA_KROWS)


NA_NDROW = 2 * NA_ROWS - 1


def _na_bias_tile(toe_ref, a, t):
    lane = lax.broadcasted_iota(jnp.int32, (1, LANES), 1)
    nrows = SEQ // GRID_W
    w0r = _na_window_start(t)
    tile_rows = []
    for rq in range(NA_QROWS):
        r = NA_QROWS * t + rq
        r0 = jnp.clip(r - NA_ROWS // 2, 0, nrows - NA_ROWS)
        blocks = []
        for j in range(NA_KROWS // 2):
            kr = w0r + 2 * j
            e = jnp.clip(kr - r + NA_ROWS, 0, NA_NDROW)
            pen0 = jnp.where(jnp.logical_and(kr >= r0, kr < r0 + NA_ROWS), 0.0, NEG)
            pen1 = jnp.where(jnp.logical_and(kr + 1 >= r0, kr + 1 < r0 + NA_ROWS), 0.0, NEG)
            blocks.append(toe_ref[a, e] + jnp.where(lane < GRID_W, pen0, pen1))
        tile_rows.append(jnp.concatenate(blocks, axis=1))
    return jnp.concatenate(tile_rows, axis=0)


def _na_kernel(q_ref, kl_ref, vl_ref, kc_ref, vc_ref, toe_ref, o_ref):
    t = pl.program_id(1)
    w0 = pl.multiple_of(_na_window_start(t) * GRID_W, GRID_W)
    nk = NA_KROWS * GRID_W
    kl = kl_ref[pl.ds(w0, nk), :]
    vl = vl_ref[pl.ds(w0, nk), :]
    o = _two_part_attention(q_ref[...], kc_ref[...], vc_ref[...], kl, vl,
                            lambda a, s: s + _na_bias_tile(toe_ref, a, t), lambda a: None)
    o_ref[...] = o.astype(o_ref.dtype)


def _na_bias_kernel(rpb_ref, o_ref):
    c = lax.broadcasted_iota(jnp.int32, (GRID_W, LANES), 0)
    lane = lax.broadcasted_iota(jnp.int32, (GRID_W, LANES), 1)
    kc = lane % GRID_W
    c0 = jnp.clip(c - NA_COLS // 2, 0, GRID_W - NA_COLS)
    col_ok = jnp.logical_and(kc >= c0, kc < c0 + NA_COLS)
    neg = jnp.full((GRID_W, LANES), NEG, F32)

    def toeplitz(d, shift):
        v = jnp.broadcast_to(rpb_ref[0, d:d + 1, :], (GRID_W, LANES))
        return pltpu.roll(v, shift % LANES, 1, stride=1, stride_axis=0)

    for e in range(NA_NDROW + 1):
        lo = toeplitz(e - 1, -(NA_COLS - 1)) if e >= 1 else neg
        hi = toeplitz(e, GRID_W - (NA_COLS - 1)) if e < NA_NDROW else neg
        o_ref[0, e] = jnp.where(col_ok, jnp.where(lane < GRID_W, lo, hi), NEG)


def _na_bias_blocks(rpb):
    nd, ncol = rpb.shape[1], rpb.shape[2]
    padded = jnp.pad(rpb.astype(F32), ((0, 0), (0, NA_NDROW + 1 - nd), (0, LANES - ncol)))
    return pl.pallas_call(
        _na_bias_kernel,
        out_shape=jax.ShapeDtypeStruct((NA_HEADS, NA_NDROW + 1, GRID_W, LANES), F32),
        grid=(NA_HEADS,),
        in_specs=[pl.BlockSpec((1, NA_NDROW + 1, LANES), lambda h: (h, 0, 0))],
        out_specs=pl.BlockSpec((1, NA_NDROW + 1, GRID_W, LANES), lambda h: (h, 0, 0, 0)),
        compiler_params=_cp(("arbitrary",)),
        name="na_bias_blocks",
    )(padded)


def _na_attention(big, bias, nb, nctx):
    tq = NA_QROWS * GRID_W
    ntile = SEQ // tq
    nct_q = nctx // tq
    lat0 = nctx // SEQ
    return pl.pallas_call(
        _na_kernel,
        out_shape=jax.ShapeDtypeStruct((nb * SEQ, NA_HEADS * HD), BF16),
        grid=(NA_HEADS // 2, ntile, nb),
        in_specs=[
            pl.BlockSpec((tq, LANES), lambda h, t, b: (nct_q + b * ntile + t, 8 + h)),
            pl.BlockSpec((SEQ, LANES), lambda h, t, b: (lat0 + b, 12 + h)),
            pl.BlockSpec((SEQ, LANES), lambda h, t, b: (lat0 + b, 16 + h)),
            pl.BlockSpec((CTX, LANES), lambda h, t, b: (b, 12 + h)),
            pl.BlockSpec((CTX, LANES), lambda h, t, b: (b, 16 + h)),
            pl.BlockSpec((2, NA_NDROW + 1, GRID_W, LANES), lambda h, t, b: (h, 0, 0, 0)),
        ],
        out_specs=pl.BlockSpec((tq, LANES), lambda h, t, b: (b * ntile + t, h)),
        compiler_params=_cp(("arbitrary", "arbitrary", "arbitrary")),
        name="na_attention",
    )(big, big, big, big, big, bias)


def _ctx_attn_kernel(q_ref, k_ref, v_ref, o_ref):
    lane = lax.broadcasted_iota(jnp.int32, (1, LANES), 1)
    q = q_ref[...]
    k = k_ref[...]
    v = v_ref[...]
    outs = []
    for a in range(2):
        qa = _head_mask(q, a) * jnp.asarray(HD ** -0.5, q.dtype)
        s = _nt(qa, k)
        m = jnp.max(s, axis=-1, keepdims=True)
        p = jnp.exp(s - m)
        den = jnp.sum(p, axis=-1, keepdims=True)
        outs.append(jnp.dot(p.astype(BF16), v, preferred_element_type=F32) / den)
    o_ref[...] = jnp.where(lane < HD, outs[0], outs[1]).astype(o_ref.dtype)


def _ctx_attention(big, nctx):
    return pl.pallas_call(
        _ctx_attn_kernel,
        out_shape=jax.ShapeDtypeStruct((nctx, NA_HEADS * HD), BF16),
        grid=(nctx // CTX, NA_HEADS // 2),
        in_specs=[
            pl.BlockSpec((CTX, LANES), lambda b, h: (b, 8 + h)),
            pl.BlockSpec((CTX, LANES), lambda b, h: (b, 12 + h)),
            pl.BlockSpec((CTX, LANES), lambda b, h: (b, 16 + h)),
        ],
        out_specs=pl.BlockSpec((CTX, LANES), lambda b, h: (b, h)),
        compiler_params=_cp(("arbitrary", "arbitrary")),
        name="ctx_attention",
    )(big, big, big)


def _swa_kernel(q_ref, kl_ref, vl_ref, kc_ref, vc_ref, sink_ref, o_ref):
    h2 = pl.program_id(1)
    n = pl.program_id(2)
    start = pl.multiple_of(jnp.clip(SWA_QT * n - SWA_WINDOW, 0, SEQ - SWA_KT), SWA_WINDOW)
    kl = kl_ref[pl.ds(start, SWA_KT), :]
    vl = vl_ref[pl.ds(start, SWA_KT), :]
    qpos = SWA_QT * n + lax.broadcasted_iota(jnp.int32, (SWA_QT, 1), 0)
    kpos = start + lax.broadcasted_iota(jnp.int32, (1, SWA_KT), 1)
    ok = jnp.abs(kpos - qpos) <= SWA_WINDOW

    def sink(a):
        return sink_ref[pl.ds(2 * h2 + a, 1), :][:, :1]

    o = _two_part_attention(q_ref[...], kc_ref[...], vc_ref[...], kl, vl,
                            lambda a, s: jnp.where(ok, s, NEG), sink)
    o_ref[...] = o.astype(o_ref.dtype)


def _swa_attention(rp, pp, sink, nb, nctx):
    nq = SEQ // SWA_QT
    nct_q = nctx // SWA_QT
    lat0 = nctx // SEQ
    sinkv = jnp.broadcast_to(sink.astype(F32)[:, None], (SWA_HEADS, LANES))
    return pl.pallas_call(
        _swa_kernel,
        out_shape=jax.ShapeDtypeStruct((nb * SEQ, SWA_HEADS * HD), BF16),
        grid=(nb, SWA_HEADS // 2, nq),
        in_specs=[
            pl.BlockSpec((SWA_QT, LANES), lambda b, h, n: (nct_q + b * nq + n, h)),
            pl.BlockSpec((SEQ, LANES), lambda b, h, n: (lat0 + b, 4 + h // 2)),
            pl.BlockSpec((SEQ, LANES), lambda b, h, n: (lat0 + b, h // 2)),
            pl.BlockSpec((CTX, LANES), lambda b, h, n: (b, 4 + h // 2)),
            pl.BlockSpec((CTX, LANES), lambda b, h, n: (b, h // 2)),
            pl.BlockSpec((SWA_HEADS, LANES), lambda b, h, n: (0, 0)),
        ],
        out_specs=pl.BlockSpec((SWA_QT, LANES), lambda b, h, n: (b * nq + n, h)),
        compiler_params=_cp(("arbitrary", "arbitrary", "arbitrary")),
        name="swa_attention",
    )(rp, rp, pp, rp, pp, sinkv)


def _conv_kernel(xp_ref, x_ref, xn_ref, w_ref, b_ref, o_ref, *, nctx_blocks, per_seq):
    i = pl.program_id(0)
    tb = x_ref.shape[0]
    k = i - nctx_blocks
    has_prev = jnp.logical_and(i >= nctx_blocks, k % per_seq != 0)
    has_next = jnp.logical_and(i >= nctx_blocks, k % per_seq != per_seq - 1)
    x = x_ref[...].astype(F32)
    xp = jnp.where(has_prev, xp_ref[...].astype(F32), 0.0)
    xn = jnp.where(has_next, xn_ref[...].astype(F32), 0.0)
    row = lax.broadcasted_iota(jnp.int32, (tb, 1), 0)
    pad = (SSM_CONV - 1) // 2
    acc = x * w_ref[pad:pad + 1, :] + b_ref[...]
    for s in range(-pad, pad + 1):
        if s == 0:
            continue
        sh = (-s) % tb
        other = xp if s < 0 else xn
        inside = jnp.logical_and(row + s >= 0, row + s < tb)
        tap = jnp.where(inside, pltpu.roll(x, sh, 0), pltpu.roll(other, sh, 0))
        acc = acc + tap * w_ref[pad + s:pad + s + 1, :]
    o_ref[...] = _silu(acc).astype(o_ref.dtype)


def _conv_silu(big, conv_w, conv_b, nctx):
    rows = big.shape[0]
    tb = CTX
    tc = 512
    nblk = rows // tb
    return pl.pallas_call(
        functools.partial(_conv_kernel, nctx_blocks=nctx // tb, per_seq=SEQ // tb),
        out_shape=jax.ShapeDtypeStruct((rows, SSM_CONV_CH), BF16),
        grid=(nblk, SSM_CONV_CH // tc),
        in_specs=[
            pl.BlockSpec((tb, tc), lambda i, c: (jnp.maximum(i - 1, 0), 5 + c)),
            pl.BlockSpec((tb, tc), lambda i, c: (i, 5 + c)),
            pl.BlockSpec((tb, tc), lambda i, c: (jnp.minimum(i + 1, nblk - 1), 5 + c)),
            pl.BlockSpec((SSM_CONV, tc), lambda i, c: (0, c)),
            pl.BlockSpec((1, tc), lambda i, c: (0, c)),
        ],
        out_specs=pl.BlockSpec((tb, tc), lambda i, c: (i, c)),
        compiler_params=_cp(("arbitrary", "arbitrary")),
        name="conv_silu",
    )(big, big, big, conv_w.reshape(SSM_CONV, SSM_CONV_CH), conv_b.reshape(1, SSM_CONV_CH))


def _chunk_block(b, d, s, nctx):
    ncc = CTX // CHUNK
    nlc = SEQ // CHUNK
    cc = jnp.where(d == 0, s, ncc - 1 - s)
    lc = jnp.where(d == 0, s - ncc, nlc + ncc - 1 - s)
    return jnp.where(s < ncc, ncc * b + cc, nctx // CHUNK + nlc * b + lc)


def _split3(a):
    hi = a.astype(BF16)
    r1 = a - hi.astype(F32)
    mid = r1.astype(BF16)
    lo = (r1 - mid.astype(F32)).astype(BF16)
    return hi, mid, lo


def _softplus(x):
    return jnp.maximum(x, 0.0) + jnp.log(1.0 + jnp.exp(-jnp.abs(x)))


def _ssd_kernel(xs_ref, b_ref, c_ref, dt_ref, dtt_ref, bias_ref, biast_ref, alog_ref, alogt_ref,
                o_ref, s_ref):
    d = pl.program_id(1)
    step = pl.program_id(2)
    q = CHUNK

    @pl.when(step == 0)
    def _():
        s_ref[...] = jnp.zeros_like(s_ref)

    ii = lax.broadcasted_iota(jnp.int32, (q, q), 0)
    jj = lax.broadcasted_iota(jnp.int32, (q, q), 1)
    dist = jnp.where(d == 0, ii - jj, jj - ii)
    causal = dist >= 0
    tri = jnp.where(causal, 1.0, 0.0).astype(BF16)
    trit = jnp.where(dist <= 0, 1.0, 0.0).astype(BF16)

    dt = _softplus(dt_ref[0] + bias_ref[0])
    a = dt * (-jnp.exp(alog_ref[0]))
    dtt = _softplus(dtt_ref[0] + biast_ref[0])
    at = dtt * (-jnp.exp(alogt_ref[0]))
    acum = sum(jnp.dot(tri, p, preferred_element_type=F32) for p in _split3(a))
    acumt = sum(jnp.dot(p, trit, preferred_element_type=F32) for p in _split3(at))
    atot = jnp.sum(a, axis=0, keepdims=True)

    lane = lax.broadcasted_iota(jnp.int32, (1, LANES), 1)
    first = lane < HD
    ngroups = b_ref.shape[1] // SSM_STATE
    pairs_per_group = SSM_HEADS // 2 // ngroups
    for g in range(ngroups):
        bg = b_ref[:, g * SSM_STATE:(g + 1) * SSM_STATE]
        cg = c_ref[:, g * SSM_STATE:(g + 1) * SSM_STATE]
        cb = _nt(cg, bg)
        for pp in range(pairs_per_group):
            p = g * pairs_per_group + pp
            ha, hb = 2 * p, 2 * p + 1
            x = xs_ref[:, p * LANES:(p + 1) * LANES].astype(F32)
            xdt = x * jnp.where(first, dt[:, ha:ha + 1], dt[:, hb:hb + 1])
            xdt16 = xdt.astype(BF16)
            ys = []
            for h in (ha, hb):
                seg = acum[:, h:h + 1] - acumt[h:h + 1, :]
                dec = jnp.exp(jnp.where(causal, seg, NEG))
                ys.append(jnp.dot((dec * cb).astype(BF16), xdt16, preferred_element_type=F32))
            ydiag = jnp.where(first, ys[0], ys[1])
            st = s_ref[p]
            grow = jnp.where(first, jnp.exp(acum[:, ha:ha + 1]), jnp.exp(acum[:, hb:hb + 1]))
            yoff = jnp.dot(cg, st.astype(BF16), preferred_element_type=F32) * grow
            o_ref[0, :, p * LANES:(p + 1) * LANES] = (ydiag + yoff).astype(o_ref.dtype)
            wst = jnp.where(first, jnp.exp(atot[:, ha:ha + 1] - acum[:, ha:ha + 1]),
                            jnp.exp(atot[:, hb:hb + 1] - acum[:, hb:hb + 1]))
            keep = jnp.where(first, jnp.exp(atot[:, ha:ha + 1]), jnp.exp(atot[:, hb:hb + 1]))
            s_ref[p] = st * keep + _tn(bg, (xdt * wst).astype(BF16))


def _ssd_scan(xbc, dt_raw, dt_bias, a_log, nb, nctx):
    rows = xbc.shape[0]
    nsteps = (CTX + SEQ) // CHUNK
    dtd = dt_raw.reshape(rows, 2, SSM_HEADS).transpose(1, 0, 2)
    dtt = dtd.transpose(0, 2, 1)
    blk = functools.partial(_chunk_block, nctx=nctx)
    par = lambda shape: pl.BlockSpec(shape, lambda b, d, s: (d, 0, 0))
    return pl.pallas_call(
        _ssd_kernel,
        out_shape=jax.ShapeDtypeStruct((2, rows, SSM_INNER), BF16),
        grid=(nb, 2, nsteps),
        in_specs=[
            pl.BlockSpec((CHUNK, SSM_INNER), lambda b, d, s: (blk(b, d, s), 0)),
            pl.BlockSpec((CHUNK, 2 * SSM_STATE), lambda b, d, s: (blk(b, d, s), 4)),
            pl.BlockSpec((CHUNK, 2 * SSM_STATE), lambda b, d, s: (blk(b, d, s), 5)),
            pl.BlockSpec((1, CHUNK, SSM_HEADS), lambda b, d, s: (d, blk(b, d, s), 0)),
            pl.BlockSpec((1, SSM_HEADS, CHUNK), lambda b, d, s: (d, 0, blk(b, d, s))),
            par((1, 1, SSM_HEADS)), par((1, SSM_HEADS, 1)),
            par((1, 1, SSM_HEADS)), par((1, SSM_HEADS, 1)),
        ],
        out_specs=pl.BlockSpec((1, CHUNK, SSM_INNER), lambda b, d, s: (d, blk(b, d, s), 0)),
        scratch_shapes=[pltpu.VMEM((SSM_HEADS // 2, SSM_STATE, LANES), F32)],
        compiler_params=_cp(("arbitrary", "arbitrary", "arbitrary")),
        name="ssd_scan",
    )(xbc, xbc, xbc, dtd, dtt,
      dt_bias.reshape(2, 1, SSM_HEADS), dt_bias.reshape(2, SSM_HEADS, 1),
      a_log.reshape(2, 1, SSM_HEADS), a_log.reshape(2, SSM_HEADS, 1))


def _ret_kernel(c_ref, b_ref, x_ref, ld_ref, o_ref, s_ref):
    d = pl.program_id(1)
    step = pl.program_id(2)
    q = CHUNK

    @pl.when(step == 0)
    def _():
        s_ref[...] = jnp.zeros_like(s_ref)

    lg = jnp.log(1.0 - jnp.exp(ld_ref[0]))
    ii = lax.broadcasted_iota(jnp.int32, (q, q), 0)
    jj = lax.broadcasted_iota(jnp.int32, (q, q), 1)
    dist = jnp.where(d == 0, ii - jj, jj - ii)
    causal = dist >= 0
    distf = dist.astype(F32)
    ri = lax.broadcasted_iota(jnp.int32, (q, 1), 0)
    npos = jnp.where(d == 0, ri + 1, q - ri).astype(F32)
    for h in range(RET_HEADS):
        lgh = lg[:, h:h + 1]
        pb, a = h // 2, h % 2
        ch = _head_mask(c_ref[:, pb * LANES:(pb + 1) * LANES], a) * jnp.asarray(HD ** -0.5, BF16)
        bh = _head_mask(b_ref[:, pb * LANES:(pb + 1) * LANES], a)
        cb = _nt(ch, bh)
        dec = jnp.exp(jnp.where(causal, distf * lgh, NEG))
        x = x_ref[:, h * LANES:(h + 1) * LANES]
        ydiag = jnp.dot((dec * cb).astype(BF16), x, preferred_element_type=F32)
        st = s_ref[h]
        yoff = jnp.dot(ch, st.astype(BF16), preferred_element_type=F32) * jnp.exp(npos * lgh)
        o_ref[0, :, h * LANES:(h + 1) * LANES] = (ydiag + yoff).astype(o_ref.dtype)
        xw = (x.astype(F32) * jnp.exp((q - npos) * lgh)).astype(BF16)
        s_ref[h] = st * jnp.exp(q * lgh) + _tn(bh, xw)


def _retention_scan(rp, pp, log_decay, nb, nctx):
    nsteps = (CTX + SEQ) // CHUNK
    ncc = CTX // CHUNK
    nlc = SEQ // CHUNK
    blk = functools.partial(_chunk_block, nctx=nctx)

    def oblk(b, d, s):
        lc = jnp.where(d == 0, s - ncc, nlc + ncc - 1 - s)
        return nlc * b + jnp.clip(lc, 0, nlc - 1)

    w = RET_HEADS * HD
    return pl.pallas_call(
        _ret_kernel,
        out_shape=jax.ShapeDtypeStruct((2, nb * SEQ, RET_HEADS * LANES), BF16),
        grid=(nb, 2, nsteps),
        in_specs=[
            pl.BlockSpec((CHUNK, w), lambda b, d, s: (blk(b, d, s), 2)),
            pl.BlockSpec((CHUNK, w), lambda b, d, s: (blk(b, d, s), 3)),
            pl.BlockSpec((CHUNK, RET_HEADS * LANES), lambda b, d, s: (blk(b, d, s), 1)),
            pl.BlockSpec((1, 1, RET_HEADS), lambda b, d, s: (d, 0, 0)),
        ],
        out_specs=pl.BlockSpec((1, CHUNK, RET_HEADS * LANES), lambda b, d, s: (d, oblk(b, d, s), 0)),
        scratch_shapes=[pltpu.VMEM((RET_HEADS, LANES, LANES), F32)],
        compiler_params=_cp(("arbitrary", "arbitrary", "arbitrary")),
        name="retention_scan",
    )(rp, rp, pp, log_decay.astype(F32).reshape(2, 1, RET_HEADS))


def _ssm_out_kernel(yf_ref, yb_ref, xs_ref, z_ref, dsk_ref, g_ref, o_ref):
    y = yf_ref[0].astype(F32) + yb_ref[0].astype(F32) + xs_ref[...].astype(F32) * dsk_ref[...]
    yz = y * _silu(z_ref[...].astype(F32))
    half = SSM_INNER // 2
    for g in range(2):
        seg = yz[:, g * half:(g + 1) * half]
        ms = jnp.mean(seg * seg, axis=-1, keepdims=True)
        o_ref[:, g * half:(g + 1) * half] = (
            seg * lax.rsqrt(ms + EPS) * g_ref[:, g * half:(g + 1) * half]).astype(o_ref.dtype)


def _ssm_out(y2, xbc, big, d_skip, gn_g, tm):
    rows = xbc.shape[0]
    dsk = jnp.repeat(d_skip.astype(F32), HD).reshape(1, SSM_INNER)
    return pl.pallas_call(
        _ssm_out_kernel,
        out_shape=jax.ShapeDtypeStruct((rows, SSM_INNER), BF16),
        grid=(rows // tm,),
        in_specs=[
            pl.BlockSpec((1, tm, SSM_INNER), lambda i: (0, i, 0)),
            pl.BlockSpec((1, tm, SSM_INNER), lambda i: (1, i, 0)),
            pl.BlockSpec((tm, SSM_INNER), lambda i: (i, 0)),
            pl.BlockSpec((tm, SSM_INNER), lambda i: (i, 0)),
            pl.BlockSpec((1, SSM_INNER), lambda i: (0, 0)),
            pl.BlockSpec((1, SSM_INNER), lambda i: (0, 0)),
        ],
        out_specs=pl.BlockSpec((tm, SSM_INNER), lambda i: (i, 0)),
        compiler_params=_cp(("parallel",)),
        name="ssm_out",
    )(y2, y2, xbc, big, dsk, gn_g.reshape(1, SSM_INNER))


def _ret_out_kernel(rf_ref, rb_ref, rg_ref, g_ref, b_ref, o_ref):
    y = rf_ref[0].astype(F32) + rb_ref[0].astype(F32)
    for h in range(RET_HEADS):
        sl = slice(h * LANES, (h + 1) * LANES)
        seg = y[:, sl]
        mu = jnp.mean(seg, axis=-1, keepdims=True)
        cen = seg - mu
        var = jnp.mean(cen * cen, axis=-1, keepdims=True)
        yn = cen * lax.rsqrt(var + EPS) * g_ref[:, sl] + b_ref[:, sl]
        o_ref[:, sl] = (_silu(rg_ref[:, sl].astype(F32)) * yn).astype(o_ref.dtype)


def _ret_out(y2, pp, gn_g, gn_b, nb, nctx, tm):
    rows = nb * SEQ
    w = RET_HEADS * LANES
    off = nctx // tm
    return pl.pallas_call(
        _ret_out_kernel,
        out_shape=jax.ShapeDtypeStruct((rows, w), BF16),
        grid=(rows // tm,),
        in_specs=[
            pl.BlockSpec((1, tm, w), lambda i: (0, i, 0)),
            pl.BlockSpec((1, tm, w), lambda i: (1, i, 0)),
            pl.BlockSpec((tm, w), lambda i: (off + i, 2)),
            pl.BlockSpec((1, w), lambda i: (0, 0)),
            pl.BlockSpec((1, w), lambda i: (0, 0)),
        ],
        out_specs=pl.BlockSpec((tm, w), lambda i: (i, 0)),
        compiler_params=_cp(("parallel",)),
        name="ret_out",
    )(y2, y2, pp, gn_g.reshape(1, w), gn_b.reshape(1, w))


def _outproj_kernel(x_ref, a1_ref, *rest, n_ctx_tiles):
    if n_ctx_tiles:
        a1c_ref, a2_ref, w1_ref, w2_ref, mod_ref, o_ref = rest
        a1 = jnp.where(pl.program_id(0) < n_ctx_tiles, a1c_ref[...], a1_ref[...])
    else:
        a2_ref, w1_ref, w2_ref, mod_ref, o_ref = rest
        a1 = a1_ref[...]
    y = (jnp.dot(a1, w1_ref[...], preferred_element_type=F32)
         + jnp.dot(a2_ref[...], w2_ref[...], preferred_element_type=F32))
    o_ref[...] = x_ref[...] + mod_ref[0][2:3] * y


def _outproj(x, a1, a2, w_out, mod, nctx, tm, x_off, a1_ctx=None):
    rows = a2.shape[0]
    k1, k2 = a1.shape[1], a2.shape[1]
    w1 = w_out[:k1].astype(BF16)
    w2 = w_out[k1:].astype(BF16)
    nct = 0 if a1_ctx is None else nctx // tm
    in_specs = [
        pl.BlockSpec((tm, D), lambda i: (i + x_off, 0)),
        pl.BlockSpec((tm, k1), lambda i: (jnp.maximum(i - nct, 0), 0)),
    ]
    args = [x, a1]
    if a1_ctx is not None:
        in_specs.append(pl.BlockSpec((tm, k1), lambda i: (jnp.minimum(i, nct - 1), 0)))
        args.append(a1_ctx)
    in_specs += [
        pl.BlockSpec((tm, k2), lambda i: (i, 0)),
        pl.BlockSpec((k1, D), lambda i: (0, 0)),
        pl.BlockSpec((k2, D), lambda i: (0, 0)),
        pl.BlockSpec((1, 6, D), lambda i: (_mod_row((i + x_off) * tm, nctx), 0, 0)),
    ]
    args += [a2, w1, w2, mod]
    return pl.pallas_call(
        functools.partial(_outproj_kernel, n_ctx_tiles=nct),
        out_shape=jax.ShapeDtypeStruct((rows, D), F32),
        grid=(rows // tm,),
        in_specs=in_specs,
        out_specs=pl.BlockSpec((tm, D), lambda i: (i, 0)),
        compiler_params=_cp(("parallel",)),
        name="outproj",
    )(*args)


def _top2_gates(logits):
    lane = lax.broadcasted_iota(jnp.int32, logits.shape, 1)
    valid = lane < N_EXPERTS
    l0 = jnp.where(valid, logits, NEG)
    m1 = jnp.max(l0, axis=-1, keepdims=True)
    i1 = jnp.min(jnp.where(l0 == m1, lane, LANES), axis=-1, keepdims=True)
    l1 = jnp.where(lane == i1, NEG, l0)
    m2 = jnp.max(l1, axis=-1, keepdims=True)
    i2 = jnp.min(jnp.where(l1 == m2, lane, LANES), axis=-1, keepdims=True)
    e2 = jnp.exp(m2 - m1)
    g1 = 1.0 / (1.0 + e2)
    g2 = e2 / (1.0 + e2)
    return jnp.where(lane == i1, g1, 0.0) + jnp.where(lane == i2, g2, 0.0)


def _ffn_kernel(x_ref, g_ref, mod_ref, *rest, routed, final):
    rest = list(rest)
    router_ref = rest.pop(0) if routed else None
    w1_ref, w3_ref, w2_ref = rest[:3]
    rest = rest[3:]
    fg_ref = rest.pop(0) if final else None
    o_ref, h_ref, acc_ref = rest[:3]
    gate_ref = rest[3] if routed else None
    e = pl.program_id(1)
    f = pl.program_id(2)
    first = jnp.logical_and(e == 0, f == 0)
    last = jnp.logical_and(e == pl.num_programs(1) - 1, f == pl.num_programs(2) - 1)

    @pl.when(first)
    def _():
        h = _norm_mod(x_ref[...], g_ref[...], mod_ref[0], 3)
        h16 = h.astype(BF16)
        h_ref[...] = h16
        acc_ref[...] = jnp.zeros_like(acc_ref)
        if routed:
            hl = (h - h16.astype(F32)).astype(BF16)
            r = router_ref[...]
            rh = r.astype(BF16)
            rl = (r - rh.astype(F32)).astype(BF16)
            logits = (jnp.dot(h16, rh, preferred_element_type=F32)
                      + jnp.dot(h16, rl, preferred_element_type=F32)
                      + jnp.dot(hl, rh, preferred_element_type=F32))
            gate_ref[...] = _top2_gates(logits)

    h16 = h_ref[...]
    u = _silu(jnp.dot(h16, w1_ref[0], preferred_element_type=F32)) * jnp.dot(
        h16, w3_ref[0], preferred_element_type=F32)
    if routed:
        lane = lax.broadcasted_iota(jnp.int32, gate_ref.shape, 1)
        ge = jnp.sum(jnp.where(lane == e, gate_ref[...], 0.0), axis=-1, keepdims=True)
        u = u * ge
    acc_ref[...] += jnp.dot(u.astype(BF16), w2_ref[0], preferred_element_type=F32)

    @pl.when(last)
    def _():
        y = x_ref[...] + mod_ref[0][5:6] * acc_ref[...]
        if final:
            ms = jnp.mean(y * y, axis=-1, keepdims=True)
            y = y * lax.rsqrt(ms + EPS) * fg_ref[...]
        o_ref[...] = y


def _ffn(x, g, mod, w1, w3, w2, nctx, tm, tf, x_off=0, rows=None, router=None, final_g=None):
    rows = x.shape[0] if rows is None else rows
    ne, _, dff = w1.shape
    routed = router is not None
    final = final_g is not None
    in_specs = [
        pl.BlockSpec((tm, D), lambda i, e, f: (i + x_off, 0)),
        pl.BlockSpec((1, D), lambda i, e, f: (0, 0)),
        pl.BlockSpec((1, 6, D), lambda i, e, f: (_mod_row((i + x_off) * tm, nctx), 0, 0)),
    ]
    args = [x, g.reshape(1, D), mod]
    if routed:
        in_specs.append(pl.BlockSpec((D, LANES), lambda i, e, f: (0, 0)))
        args.append(jnp.pad(router.astype(F32), ((0, 0), (0, LANES - ne))))
    in_specs += [
        pl.BlockSpec((1, D, tf), lambda i, e, f: (e, 0, f)),
        pl.BlockSpec((1, D, tf), lambda i, e, f: (e, 0, f)),
        pl.BlockSpec((1, tf, D), lambda i, e, f: (e, f, 0)),
    ]
    args += [w1, w3, w2]
    if final:
        in_specs.append(pl.BlockSpec((1, D), lambda i, e, f: (0, 0)))
        args.append(final_g.reshape(1, D))
    scratch = [pltpu.VMEM((tm, D), BF16), pltpu.VMEM((tm, D), F32)]
    if routed:
        scratch.append(pltpu.VMEM((tm, LANES), F32))
    return pl.pallas_call(
        functools.partial(_ffn_kernel, routed=routed, final=final),
        out_shape=jax.ShapeDtypeStruct((rows, D), F32),
        grid=(rows // tm, ne, dff // tf),
        in_specs=in_specs,
        out_specs=pl.BlockSpec((tm, D), lambda i, e, f: (i, 0)),
        scratch_shapes=scratch,
        compiler_params=_cp(("parallel", "arbitrary", "arbitrary")),
        name="moe_ffn" if routed else "dense_ffn",
    )(*args)


def _rope_tables(tm):
    t = np.arange(SEQ)
    n_freq = HD // 4
    inv = ROPE_BASE ** (-np.arange(n_freq, dtype=np.float64) / n_freq)
    ang = np.concatenate([(t // GRID_W)[:, None] * inv, (t % GRID_W)[:, None] * inv], axis=-1)
    cos = np.tile(np.cos(ang), (1, 4))
    sin = np.tile(np.concatenate([-np.sin(ang), np.sin(ang)], axis=-1), (1, 2))
    cos = np.concatenate([np.ones((tm, LANES)), cos], axis=0)
    sin = np.concatenate([np.zeros((tm, LANES)), sin], axis=0)
    return jnp.asarray(cos, F32), jnp.asarray(sin, F32)


def _dup_heads(w, nheads):
    w = w.reshape(D, nheads, 1, HD)
    return jnp.broadcast_to(w, (D, nheads, 2, HD)).reshape(D, nheads * 2 * HD)


def kernel(x, c, ctx, c_ctx, ada_w, ada_b, norm_attn_g, norm_ffn_g, ev_w_in, ev_w_out, na_rpb,
           ssm_conv_w, ssm_conv_b, ssm_dt_bias, ssm_a_log, ssm_d, ssm_norm_g, ffn_w1, ffn_w3, ffn_w2,
           od_w_in, od_w_out, swa_sink, ret_log_decay, ret_gn_g, ret_gn_b, moe_router, moe_w1,
           moe_w3, moe_w2, final_g):
    nb = x.shape[0]
    nctx = -(-(nb * CTX) // SEQ) * SEQ
    tm = 1024
    xs = jnp.concatenate([ctx.reshape(nb * CTX, D), jnp.zeros((nctx - nb * CTX, D), ctx.dtype),
                          x.reshape(nb * SEQ, D)], axis=0).astype(F32)

    rp = -(-(nb + 1) // 8) * 8
    cin = jnp.zeros((rp, D), F32).at[0].set(c_ctx).at[1:nb + 1].set(c)
    mod = _modulation(cin, ada_w, ada_b).reshape(2, rp, 6, D)

    w_in = ev_w_in[0]
    q_, k_, v_, z_, xbc_, dt_ = jnp.split(w_in, [512, 1024, 1536, 2560, 4096], axis=1)
    w_big = jnp.concatenate([z_, q_, k_, v_, xbc_], axis=1).astype(BF16)
    big = _proj(xs, norm_attn_g[0], mod[0], w_big, BF16, nctx, tm, 512, name="even_in_proj")
    dt_raw = _proj(xs, norm_attn_g[0], mod[0], dt_.astype(BF16), F32, nctx, tm, 2 * SSM_HEADS,
                   name="even_dt_proj")
    attn = _na_attention(big, _na_bias_blocks(na_rpb[0]), nb, nctx)
    attn_ctx = _ctx_attention(big, nctx)
    xbc = _conv_silu(big, ssm_conv_w[0], ssm_conv_b[0], nctx)
    y2 = _ssd_scan(xbc, dt_raw, ssm_dt_bias[0], ssm_a_log[0], nb, nctx)
    ys = _ssm_out(y2, xbc, big, ssm_d[0], ssm_norm_g[0], tm)
    xs = _outproj(xs, attn, ys, ev_w_out[0], mod[0], nctx, tm, 0, a1_ctx=attn_ctx)
    xs = _ffn(xs, norm_ffn_g[0], mod[0], ffn_w1.astype(BF16), ffn_w3.astype(BF16),
              ffn_w2.astype(BF16), nctx, tm, 256)

    w_in = od_w_in[0]
    q_, k_, v_, rq_, rk_, rv_, rg_ = jnp.split(w_in, [512, 640, 768, 1280, 1792, 2816], axis=1)
    w_rope = jnp.concatenate([q_, _dup_heads(k_, 2), jnp.zeros((D, 256), F32), rq_, rk_],
                             axis=1).astype(BF16)
    w_plain = jnp.concatenate([_dup_heads(v_, 2), jnp.zeros((D, 768), F32), rv_, rg_],
                              axis=1).astype(BF16)
    rope_tabs = _rope_tables(tm)
    rpj = _proj(xs, norm_attn_g[1], mod[1], w_rope, BF16, nctx, tm, 512, rope=rope_tabs,
                name="odd_rope_proj")
    ppj = _proj(xs, norm_attn_g[1], mod[1], w_plain, BF16, nctx, tm, 512, name="odd_plain_proj")
    yw = _swa_attention(rpj, ppj, swa_sink[0], nb, nctx)
    r2 = _retention_scan(rpj, ppj, ret_log_decay[0], nb, nctx)
    yr = _ret_out(r2, ppj, ret_gn_g[0], ret_gn_b[0], nb, nctx, tm)
    x_off = nctx // tm
    xl = _outproj(xs, yw, yr, od_w_out[0], mod[1], nctx, tm, x_off)
    out = _ffn(xl, norm_ffn_g[1], mod[1], moe_w1[0].astype(BF16), moe_w3[0].astype(BF16),
               moe_w2[0].astype(BF16), 0, tm, 512, router=moe_router[0], final_g=final_g)
    return out.reshape(nb, SEQ, D).astype(x.dtype)
```

```python
import functools

import numpy as np
import jax
import jax.numpy as jnp
from jax import lax
from jax.experimental import pallas as pl
from jax.experimental.pallas import tpu as pltpu

F32 = jnp.float32
BF16 = jnp.bfloat16

D = 1024
SEQ = 2048
CTX = 256
GRID_W = 64
HD = 64
EPS = 1e-6
ROPE_BASE = 10000.0
NEG = -1e30

NA_HEADS = 8
NA_ROWS = 8
NA_COLS = 16
NA_QROWS = 4
NA_KROWS = 12
SSM_HEADS = 16
SSM_INNER = 1024
SSM_STATE = 128
SSM_CONV = 5
SSM_CONV_CH = 1536
CHUNK = 128
SWA_HEADS = 8
SWA_WINDOW = 128
SWA_QT = 512
SWA_SUB = 128
SWA_KT = SWA_SUB + 2 * SWA_WINDOW
RET_HEADS = 8
D_FF = 2816
N_EXPERTS = 8
D_FF_EXPERT = 3584

LANES = 128
VMEM_LIMIT = 56 * 1024 * 1024


def _cp(sem, vmem=VMEM_LIMIT):
    return pltpu.CompilerParams(dimension_semantics=sem, vmem_limit_bytes=vmem)


def _sigmoid(x):
    return 1.0 / (1.0 + jnp.exp(-x))


def _silu(x):
    return x * _sigmoid(x)


def _mod_row(start, nctx):
    return jnp.where(start < nctx, 0, 1 + (start - nctx) // SEQ)


def _mod_kernel(c_ref, w_ref, b_ref, o_ref):
    c = c_ref[...]
    h = _silu(c).astype(BF16)
    o_ref[0] = jnp.dot(h, w_ref[0].astype(BF16), preferred_element_type=F32) + b_ref[0]


def _modulation(cin, ada_w, ada_b):
    depth, _, n6 = ada_w.shape
    rp = cin.shape[0]
    tn = 1024
    return pl.pallas_call(
        _mod_kernel,
        out_shape=jax.ShapeDtypeStruct((depth, rp, n6), F32),
        grid=(depth, n6 // tn),
        in_specs=[
            pl.BlockSpec((rp, D), lambda l, j: (0, 0)),
            pl.BlockSpec((1, D, tn), lambda l, j: (l, 0, j)),
            pl.BlockSpec((1, 1, tn), lambda l, j: (l, 0, j)),
        ],
        out_specs=pl.BlockSpec((1, rp, tn), lambda l, j: (l, 0, j)),
        compiler_params=_cp(("arbitrary", "arbitrary")),
        name="adaln_mod",
    )(cin, ada_w, ada_b.reshape(depth, 1, n6))


def _norm_mod(x, g, m, k):
    ms = jnp.mean(x * x, axis=-1, keepdims=True)
    y = x * lax.rsqrt(ms + EPS) * g
    return y * (1.0 + m[k + 1:k + 2]) + m[k:k + 1]


def _swap32(r):
    lane = lax.broadcasted_iota(jnp.int32, r.shape, 1)
    return jnp.where((lane % 64) < 32, pltpu.roll(r, 96, 1), pltpu.roll(r, 32, 1))


def _proj_kernel(x_ref, g_ref, mod_ref, w_ref, *rest, rope):
    if rope:
        c_ref, s_ref, o_ref, h_ref = rest
    else:
        o_ref, h_ref = rest

    @pl.when(pl.program_id(1) == 0)
    def _():
        h_ref[...] = _norm_mod(x_ref[...], g_ref[...], mod_ref[0], 0).astype(BF16)

    r = jnp.dot(h_ref[...], w_ref[...], preferred_element_type=F32)
    if rope:
        c = c_ref[...]
        s = s_ref[...]
        for k in range(r.shape[1] // LANES):
            rk = r[:, k * LANES:(k + 1) * LANES]
            o_ref[:, k * LANES:(k + 1) * LANES] = (rk * c + _swap32(rk) * s).astype(o_ref.dtype)
    else:
        o_ref[...] = r.astype(o_ref.dtype)


def _proj(x, g, mod, w, out_dtype, nctx, tm, tn, rope=None, name="proj"):
    rows = x.shape[0]
    n = w.shape[1]
    in_specs = [
        pl.BlockSpec((tm, D), lambda i, j: (i, 0)),
        pl.BlockSpec((1, D), lambda i, j: (0, 0)),
        pl.BlockSpec((1, 6, D), lambda i, j: (_mod_row(i * tm, nctx), 0, 0)),
        pl.BlockSpec((D, tn), lambda i, j: (0, j)),
    ]
    args = [x, g.reshape(1, D), mod, w]
    if rope is not None:
        nct = nctx // tm
        per = SEQ // tm

        def tab_map(i, j):
            return (jnp.where(i < nct, 0, 1 + (i - nct) % per), 0)
        in_specs += [pl.BlockSpec((tm, LANES), tab_map), pl.BlockSpec((tm, LANES), tab_map)]
        args += list(rope)
    return pl.pallas_call(
        functools.partial(_proj_kernel, rope=rope is not None),
        out_shape=jax.ShapeDtypeStruct((rows, n), out_dtype),
        grid=(rows // tm, n // tn),
        in_specs=in_specs,
        out_specs=pl.BlockSpec((tm, tn), lambda i, j: (i, j)),
        scratch_shapes=[pltpu.VMEM((tm, D), BF16)],
        compiler_params=_cp(("parallel", "arbitrary")),
        name=name,
    )(*args)


def _head_mask(x, a):
    lane = lax.broadcasted_iota(jnp.int32, (1, LANES), 1)
    keep = (lane < HD) if a == 0 else (lane >= HD)
    return jnp.where(keep, x, jnp.zeros_like(x))


def _nt(a, b):
    return lax.dot_general(a, b, (((1,), (1,)), ((), ())), preferred_element_type=F32)


def _two_part_attention(q, kc, vc, kl, vl, bias_fn, sink_fn):
    lane = lax.broadcasted_iota(jnp.int32, (1, LANES), 1)
    outs = []
    for a in range(2):
        qa = _head_mask(q, a) * jnp.asarray(HD ** -0.5, q.dtype)
        s_c = _nt(qa, kc)
        s_l = bias_fn(a, _nt(qa, kl))
        m = jnp.maximum(jnp.max(s_c, axis=-1, keepdims=True), jnp.max(s_l, axis=-1, keepdims=True))
        sink = sink_fn(a)
        if sink is not None:
            m = jnp.maximum(m, sink)
        p_c = jnp.exp(s_c - m)
        p_l = jnp.exp(s_l - m)
        den = jnp.sum(p_c, axis=-1, keepdims=True) + jnp.sum(p_l, axis=-1, keepdims=True)
        if sink is not None:
            den = den + jnp.exp(sink - m)
        o = (jnp.dot(p_c.astype(BF16), vc, preferred_element_type=F32)
             + jnp.dot(p_l.astype(BF16), vl, preferred_element_type=F32))
        outs.append(o / den)
    return jnp.where(lane < HD, outs[0], outs[1])


def _na_window_start(t):
    return jnp.clip(NA_QROWS * t - NA_ROWS // 2, 0, SEQ // GRID_W - NA_KROWS)


NA_NDROW = 2 * NA_ROWS - 1


def _na_bias_tile(toe_ref, a, t):
    lane = lax.broadcasted_iota(jnp.int32, (1, LANES), 1)
    nrows = SEQ // GRID_W
    w0r = _na_window_start(t)
    tile_rows = []
    for rq in range(NA_QROWS):
        r = NA_QROWS * t + rq
        r0 = jnp.clip(r - NA_ROWS // 2, 0, nrows - NA_ROWS)
        blocks = []
        for j in range(NA_KROWS // 2):
            kr = w0r + 2 * j
            e = jnp.clip(kr - r + NA_ROWS, 0, NA_NDROW)
            pen0 = jnp.where(jnp.logical_and(kr >= r0, kr < r0 + NA_ROWS), 0.0, NEG)
            pen1 = jnp.where(jnp.logical_and(kr + 1 >= r0, kr + 1 < r0 + NA_ROWS), 0.0, NEG)
            blocks.append(toe_ref[a, e] + jnp.where(lane < GRID_W, pen0, pen1))
        tile_rows.append(jnp.concatenate(blocks, axis=1))
    return jnp.concatenate(tile_rows, axis=0)


def _na_kernel(q_ref, kl_ref, vl_ref, kc_ref, vc_ref, toe_ref, o_ref):
    t = pl.program_id(1)
    w0 = pl.multiple_of(_na_window_start(t) * GRID_W, GRID_W)
    nk = NA_KROWS * GRID_W
    kl = kl_ref[pl.ds(w0, nk), :]
    vl = vl_ref[pl.ds(w0, nk), :]
    o = _two_part_attention(q_ref[...], kc_ref[...], vc_ref[...], kl, vl,
                            lambda a, s: s + _na_bias_tile(toe_ref, a, t), lambda a: None)
    o_ref[...] = o.astype(o_ref.dtype)


def _na_bias_kernel(rpb_ref, o_ref):
    c = lax.broadcasted_iota(jnp.int32, (GRID_W, LANES), 0)
    lane = lax.broadcasted_iota(jnp.int32, (GRID_W, LANES), 1)
    kc = lane % GRID_W
    c0 = jnp.clip(c - NA_COLS // 2, 0, GRID_W - NA_COLS)
    col_ok = jnp.logical_and(kc >= c0, kc < c0 + NA_COLS)
    neg = jnp.full((GRID_W, LANES), NEG, F32)

    def toeplitz(d, shift):
        v = jnp.broadcast_to(rpb_ref[0, d:d + 1, :], (GRID_W, LANES))
        return pltpu.roll(v, shift % LANES, 1, stride=1, stride_axis=0)

    for e in range(NA_NDROW + 1):
        lo = toeplitz(e - 1, -(NA_COLS - 1)) if e >= 1 else neg
        hi = toeplitz(e, GRID_W - (NA_COLS - 1)) if e < NA_NDROW else neg
        o_ref[0, e] = jnp.where(col_ok, jnp.where(lane < GRID_W, lo, hi), NEG)


def _na_bias_blocks(rpb):
    nd, ncol = rpb.shape[1], rpb.shape[2]
    padded = jnp.pad(rpb.astype(F32), ((0, 0), (0, NA_NDROW + 1 - nd), (0, LANES - ncol)))
    return pl.pallas_call(
        _na_bias_kernel,
        out_shape=jax.ShapeDtypeStruct((NA_HEADS, NA_NDROW + 1, GRID_W, LANES), F32),
        grid=(NA_HEADS,),
        in_specs=[pl.BlockSpec((1, NA_NDROW + 1, LANES), lambda h: (h, 0, 0))],
        out_specs=pl.BlockSpec((1, NA_NDROW + 1, GRID_W, LANES), lambda h: (h, 0, 0, 0)),
        compiler_params=_cp(("arbitrary",)),
        name="na_bias_blocks",
    )(padded)


def _na_attention(big, bias, nb, nctx):
    tq = NA_QROWS * GRID_W
    ntile = SEQ // tq
    nct_q = nctx // tq
    lat0 = nctx // SEQ
    return pl.pallas_call(
        _na_kernel,
        out_shape=jax.ShapeDtypeStruct((nb * SEQ, NA_HEADS * HD), BF16),
        grid=(NA_HEADS // 2, ntile, nb),
        in_specs=[
            pl.BlockSpec((tq, LANES), lambda h, t, b: (nct_q + b * ntile + t, 8 + h)),
            pl.BlockSpec((SEQ, LANES), lambda h, t, b: (lat0 + b, 12 + h)),
            pl.BlockSpec((SEQ, LANES), lambda h, t, b: (lat0 + b, 16 + h)),
            pl.BlockSpec((CTX, LANES), lambda h, t, b: (b, 12 + h)),
            pl.BlockSpec((CTX, LANES), lambda h, t, b: (b, 16 + h)),
            pl.BlockSpec((2, NA_NDROW + 1, GRID_W, LANES), lambda h, t, b: (h, 0, 0, 0)),
        ],
        out_specs=pl.BlockSpec((tq, LANES), lambda h, t, b: (b * ntile + t, h)),
        compiler_params=_cp(("arbitrary", "arbitrary", "arbitrary")),
        name="na_attention",
    )(big, big, big, big, big, bias)


def _ctx_attn_kernel(q_ref, k_ref, v_ref, o_ref):
    lane = lax.broadcasted_iota(jnp.int32, (1, LANES), 1)
    q = q_ref[...]
    k = k_ref[...]
    v = v_ref[...]
    outs = []
    for a in range(2):
        qa = _head_mask(q, a) * jnp.asarray(HD ** -0.5, q.dtype)
        s = _nt(qa, k)
        m = jnp.max(s, axis=-1, keepdims=True)
        p = jnp.exp(s - m)
        den = jnp.sum(p, axis=-1, keepdims=True)
        outs.append(jnp.dot(p.astype(BF16), v, preferred_element_type=F32) / den)
    o_ref[...] = jnp.where(lane < HD, outs[0], outs[1]).astype(o_ref.dtype)


def _ctx_attention(big, nctx):
    return pl.pallas_call(
        _ctx_attn_kernel,
        out_shape=jax.ShapeDtypeStruct((nctx, NA_HEADS * HD), BF16),
        grid=(nctx // CTX, NA_HEADS // 2),
        in_specs=[
            pl.BlockSpec((CTX, LANES), lambda b, h: (b, 8 + h)),
            pl.BlockSpec((CTX, LANES), lambda b, h: (b, 12 + h)),
            pl.BlockSpec((CTX, LANES), lambda b, h: (b, 16 + h)),
        ],
        out_specs=pl.BlockSpec((CTX, LANES), lambda b, h: (b, h)),
        compiler_params=_cp(("arbitrary", "arbitrary")),
        name="ctx_attention",
    )(big, big, big)


def _swa_kernel(q_ref, kl_ref, vl_ref, kc_ref, vc_ref, sink_ref, o_ref):
    h2 = pl.program_id(1)
    n = pl.program_id(2)
    kc = kc_ref[...]
    vc = vc_ref[...]

    def sink(a):
        return sink_ref[pl.ds(2 * h2 + a, 1), :][:, :1]

    for sub in range(SWA_QT // SWA_SUB):
        q0 = SWA_QT * n + SWA_SUB * sub
        start = pl.multiple_of(jnp.clip(q0 - SWA_WINDOW, 0, SEQ - SWA_KT), SWA_WINDOW)
        kl = kl_ref[pl.ds(start, SWA_KT), :]
        vl = vl_ref[pl.ds(start, SWA_KT), :]
        qpos = q0 + lax.broadcasted_iota(jnp.int32, (SWA_SUB, 1), 0)
        kpos = start + lax.broadcasted_iota(jnp.int32, (1, SWA_KT), 1)
        ok = jnp.abs(kpos - qpos) <= SWA_WINDOW
        rows = slice(sub * SWA_SUB, (sub + 1) * SWA_SUB)
        o = _two_part_attention(q_ref[rows, :], kc, vc, kl, vl,
                                lambda a, s, ok=ok: jnp.where(ok, s, NEG), sink)
        o_ref[rows, :] = o.astype(o_ref.dtype)


def _swa_attention(rp, pp, sink, nb, nctx):
    nq = SEQ // SWA_QT
    nct_q = nctx // SWA_QT
    lat0 = nctx // SEQ
    sinkv = jnp.broadcast_to(sink.astype(F32)[:, None], (SWA_HEADS, LANES))
    return pl.pallas_call(
        _swa_kernel,
        out_shape=jax.ShapeDtypeStruct((nb * SEQ, SWA_HEADS * HD), BF16),
        grid=(nb, SWA_HEADS // 2, nq),
        in_specs=[
            pl.BlockSpec((SWA_QT, LANES), lambda b, h, n: (nct_q + b * nq + n, h)),
            pl.BlockSpec((SEQ, LANES), lambda b, h, n: (lat0 + b, 12 + h // 2)),
            pl.BlockSpec((SEQ, LANES), lambda b, h, n: (lat0 + b, 16 + h // 2)),
            pl.BlockSpec((CTX, LANES), lambda b, h, n: (b, 12 + h // 2)),
            pl.BlockSpec((CTX, LANES), lambda b, h, n: (b, 16 + h // 2)),
            pl.BlockSpec((SWA_HEADS, LANES), lambda b, h, n: (0, 0)),
        ],
        out_specs=pl.BlockSpec((SWA_QT, LANES), lambda b, h, n: (b * nq + n, h)),
        compiler_params=_cp(("arbitrary", "arbitrary", "arbitrary")),
        name="swa_attention",
    )(rp, rp, pp, rp, pp, sinkv)


def _conv_kernel(xp_ref, x_ref, xn_ref, w_ref, b_ref, o_ref, *, nctx_blocks, per_seq):
    i = pl.program_id(0)
    tb = x_ref.shape[0]
    k = i - nctx_blocks
    has_prev = jnp.logical_and(i >= nctx_blocks, k % per_seq != 0)
    has_next = jnp.logical_and(i >= nctx_blocks, k % per_seq != per_seq - 1)
    x = x_ref[...].astype(F32)
    xp = jnp.where(has_prev, xp_ref[...].astype(F32), 0.0)
    xn = jnp.where(has_next, xn_ref[...].astype(F32), 0.0)
    row = lax.broadcasted_iota(jnp.int32, (tb, 1), 0)
    pad = (SSM_CONV - 1) // 2
    acc = x * w_ref[pad:pad + 1, :] + b_ref[...]
    for s in range(-pad, pad + 1):
        if s == 0:
            continue
        sh = (-s) % tb
        other = xp if s < 0 else xn
        inside = jnp.logical_and(row + s >= 0, row + s < tb)
        tap = jnp.where(inside, pltpu.roll(x, sh, 0), pltpu.roll(other, sh, 0))
        acc = acc + tap * w_ref[pad + s:pad + s + 1, :]
    o_ref[...] = _silu(acc).astype(o_ref.dtype)


def _conv_silu(big, conv_w, conv_b, nctx):
    rows = big.shape[0]
    tb = CTX
    tc = 512
    nblk = rows // tb
    return pl.pallas_call(
        functools.partial(_conv_kernel, nctx_blocks=nctx // tb, per_seq=SEQ // tb),
        out_shape=jax.ShapeDtypeStruct((rows, SSM_CONV_CH), BF16),
        grid=(nblk, SSM_CONV_CH // tc),
        in_specs=[
            pl.BlockSpec((tb, tc), lambda i, c: (jnp.maximum(i - 1, 0), 5 + c)),
            pl.BlockSpec((tb, tc), lambda i, c: (i, 5 + c)),
            pl.BlockSpec((tb, tc), lambda i, c: (jnp.minimum(i + 1, nblk - 1), 5 + c)),
            pl.BlockSpec((SSM_CONV, tc), lambda i, c: (0, c)),
            pl.BlockSpec((1, tc), lambda i, c: (0, c)),
        ],
        out_specs=pl.BlockSpec((tb, tc), lambda i, c: (i, c)),
        compiler_params=_cp(("arbitrary", "arbitrary")),
        name="conv_silu",
    )(big, big, big, conv_w.reshape(SSM_CONV, SSM_CONV_CH), conv_b.reshape(1, SSM_CONV_CH))


NCC = CTX // CHUNK
NLC = SEQ // CHUNK


def _chunk_block(b, s, d, nctx):
    cc = s if d == 0 else NCC - 1 - s
    lc = s - NCC if d == 0 else NLC + NCC - 1 - s
    return jnp.where(s < NCC, NCC * b + cc, nctx // CHUNK + NLC * b + lc)


def _latent_chunk_block(b, s, d):
    lc = s - NCC if d == 0 else NLC + NCC - 1 - s
    return NLC * b + jnp.clip(lc, 0, NLC - 1)


def _split3(a):
    hi = a.astype(BF16)
    r1 = a - hi.astype(F32)
    mid = r1.astype(BF16)
    lo = (r1 - mid.astype(F32)).astype(BF16)
    return hi, mid, lo


def _softplus(x):
    return jnp.maximum(x, 0.0) + jnp.log(1.0 + jnp.exp(-jnp.abs(x)))


def _ssd_direction(d, x_ref, b_ref, c_ref, dt_ref, dtt_ref, bias_ref, biast_ref, alog_ref,
                   alogt_ref, y_ref, s_ref):
    q = CHUNK
    ii = lax.broadcasted_iota(jnp.int32, (q, q), 0)
    jj = lax.broadcasted_iota(jnp.int32, (q, q), 1)
    causal = (jj <= ii) if d == 0 else (jj >= ii)
    tri = jnp.where(causal, 1.0, 0.0).astype(BF16)
    trit = jnp.where((ii <= jj) if d == 0 else (ii >= jj), 1.0, 0.0).astype(BF16)

    dt = _softplus(dt_ref[0] + bias_ref[d])
    a = dt * (-jnp.exp(alog_ref[d]))
    dtt = _softplus(dtt_ref[0] + biast_ref[d])
    at = dtt * (-jnp.exp(alogt_ref[d]))
    acum = sum(jnp.dot(tri, p, preferred_element_type=F32) for p in _split3(a))
    acumt = sum(jnp.dot(p, trit, preferred_element_type=F32) for p in _split3(at))
    atot = jnp.sum(at, axis=1, keepdims=True)
    wrow = dtt * jnp.exp(atot - acumt)
    keep = jnp.exp(atot)

    first = lax.broadcasted_iota(jnp.int32, (1, LANES), 1) < HD
    ngroups = b_ref.shape[1] // SSM_STATE
    pairs_per_group = SSM_HEADS // 2 // ngroups
    for g in range(ngroups):
        bg = b_ref[:, g * SSM_STATE:(g + 1) * SSM_STATE]
        cg = c_ref[:, g * SSM_STATE:(g + 1) * SSM_STATE]
        cb = _nt(cg, bg)
        cgf = cg.astype(F32)
        bt = bg.astype(F32).T
        for pp in range(pairs_per_group):
            p = g * pairs_per_group + pp
            x = x_ref[:, p * LANES:(p + 1) * LANES]
            st = s_ref[d, p]
            rhs = jnp.concatenate([x, st.astype(BF16)], axis=0)
            ys, us = [], []
            for h in (2 * p, 2 * p + 1):
                col = jnp.broadcast_to(acum[:, h:h + 1], (q, LANES))
                dec = jnp.exp(jnp.where(causal, col - acumt[h:h + 1, :], NEG)) * (cb * dtt[h:h + 1, :])
                lhs = jnp.concatenate([dec, jnp.exp(col) * cgf], axis=1).astype(BF16)
                ys.append(jnp.dot(lhs, rhs, preferred_element_type=F32))
                us.append(jnp.dot((bt * wrow[h:h + 1, :]).astype(BF16), x,
                                  preferred_element_type=F32))
            y_ref[:, p * LANES:(p + 1) * LANES] = jnp.where(first, ys[0], ys[1]).astype(y_ref.dtype)
            keep_p = jnp.where(first, keep[2 * p:2 * p + 1, :], keep[2 * p + 1:2 * p + 2, :])
            s_ref[d, p] = st * keep_p + jnp.where(first, us[0], us[1])


def _ssd_kernel(*refs):
    fwd, bwd, (bias_ref, biast_ref, alog_ref, alogt_ref), (yf_ref, yb_ref, s_ref) = (
        refs[0:5], refs[5:10], refs[10:14], refs[14:17])
    assert CHUNK == LANES

    @pl.when(pl.program_id(1) == 0)
    def _():
        s_ref[...] = jnp.zeros_like(s_ref)

    for d, ins, y_ref in ((0, fwd, yf_ref), (1, bwd, yb_ref)):
        _ssd_direction(d, *ins, bias_ref, biast_ref, alog_ref, alogt_ref, y_ref, s_ref)


def _ssd_scan(xbc, dt_raw, dt_bias, a_log, nb, nctx):
    rows = xbc.shape[0]
    dtd = dt_raw.reshape(rows, 2, SSM_HEADS).transpose(1, 0, 2)
    dtt = dtd.transpose(0, 2, 1)

    def direction_specs(d):
        blk = functools.partial(_chunk_block, d=d, nctx=nctx)
        return [
            pl.BlockSpec((CHUNK, SSM_INNER), lambda b, s: (blk(b, s), 0)),
            pl.BlockSpec((CHUNK, 2 * SSM_STATE), lambda b, s: (blk(b, s), 4)),
            pl.BlockSpec((CHUNK, 2 * SSM_STATE), lambda b, s: (blk(b, s), 5)),
            pl.BlockSpec((1, CHUNK, SSM_HEADS), lambda b, s: (d, blk(b, s), 0)),
            pl.BlockSpec((1, SSM_HEADS, CHUNK), lambda b, s: (d, 0, blk(b, s))),
        ]

    whole = lambda shape: pl.BlockSpec(shape, lambda b, s: (0, 0, 0))
    out_spec = lambda d: pl.BlockSpec(
        (CHUNK, SSM_INNER), lambda b, s: (_chunk_block(b, s, d, nctx), 0))
    y_shape = jax.ShapeDtypeStruct((rows, SSM_INNER), BF16)
    return pl.pallas_call(
        _ssd_kernel,
        out_shape=(y_shape, y_shape),
        grid=(nb, NCC + NLC),
        in_specs=direction_specs(0) + direction_specs(1) + [
            whole((2, 1, SSM_HEADS)), whole((2, SSM_HEADS, 1)),
            whole((2, 1, SSM_HEADS)), whole((2, SSM_HEADS, 1)),
        ],
        out_specs=(out_spec(0), out_spec(1)),
        scratch_shapes=[pltpu.VMEM((2, SSM_HEADS // 2, SSM_STATE, LANES), F32)],
        compiler_params=_cp(("arbitrary", "arbitrary")),
        name="ssd_scan",
    )(*([xbc, xbc, xbc, dtd, dtt] * 2),
      dt_bias.reshape(2, 1, SSM_HEADS), dt_bias.reshape(2, SSM_HEADS, 1),
      a_log.reshape(2, 1, SSM_HEADS), a_log.reshape(2, SSM_HEADS, 1))


def _ret_geometry(d):
    q = CHUNK
    ii = lax.broadcasted_iota(jnp.int32, (q, q), 0)
    jj = lax.broadcasted_iota(jnp.int32, (q, q), 1)
    dist = (ii - jj) if d == 0 else (jj - ii)
    ri = lax.broadcasted_iota(jnp.int32, (q, LANES), 0)
    cj = lax.broadcasted_iota(jnp.int32, (1, q), 1)
    steps_in = ((ri + 1) if d == 0 else (q - ri)).astype(F32)
    steps_out = ((q - 1 - cj) if d == 0 else cj).astype(F32)
    return dist >= 0, dist.astype(F32), steps_in, steps_out


def _ret_kernel(*refs):
    fwd, bwd, ld_ref, (yf_ref, yb_ref, s_ref) = refs[0:3], refs[3:6], refs[6], refs[7:10]
    q = CHUNK
    step = pl.program_id(1)

    @pl.when(step == 0)
    def _():
        s_ref[...] = jnp.zeros_like(s_ref)

    scale = jnp.asarray(HD ** -0.5, BF16)
    dirs = ((0, fwd, yf_ref), (1, bwd, yb_ref))

    for d, (c_ref, b_ref, x_ref), y_ref in dirs:
        causal, distf, steps_in, _ = _ret_geometry(d)
        lg = jnp.log(1.0 - jnp.exp(ld_ref[d]))
        for h in range(RET_HEADS):
            lgh = lg[:, h:h + 1]
            pb, a = h // 2, h % 2
            ch = _head_mask(c_ref[:, pb * LANES:(pb + 1) * LANES], a) * scale
            bh = _head_mask(b_ref[:, pb * LANES:(pb + 1) * LANES], a)
            dec = jnp.exp(jnp.where(causal, distf * lgh, NEG)) * _nt(ch, bh)
            grow = jnp.exp(steps_in * lgh) * ch.astype(F32)
            lhs = jnp.concatenate([dec, grow], axis=1).astype(BF16)
            x = x_ref[:, h * LANES:(h + 1) * LANES]
            rhs = jnp.concatenate([x, s_ref[d, h].astype(BF16)], axis=0)
            y_ref[:, h * LANES:(h + 1) * LANES] = jnp.dot(
                lhs, rhs, preferred_element_type=F32).astype(y_ref.dtype)

    for d, (c_ref, b_ref, x_ref), y_ref in dirs:
        _, _, _, steps_out = _ret_geometry(d)
        lg = jnp.log(1.0 - jnp.exp(ld_ref[d]))
        for h in range(RET_HEADS):
            lgh = lg[:, h:h + 1]
            pb, a = h // 2, h % 2
            bt = _head_mask(b_ref[:, pb * LANES:(pb + 1) * LANES], a).astype(F32).T
            bth = (bt * jnp.exp(steps_out * lgh)).astype(BF16)
            x = x_ref[:, h * LANES:(h + 1) * LANES]
            s_ref[d, h] = s_ref[d, h] * jnp.exp(q * lgh) + jnp.dot(
                bth, x, preferred_element_type=F32)


def _retention_scan(rp, pp, log_decay, nb, nctx):
    w = RET_HEADS * HD
    wv = RET_HEADS * LANES

    def direction_specs(d):
        blk = functools.partial(_chunk_block, d=d, nctx=nctx)
        return [
            pl.BlockSpec((CHUNK, w), lambda b, s: (blk(b, s), 1)),
            pl.BlockSpec((CHUNK, w), lambda b, s: (blk(b, s), 2)),
            pl.BlockSpec((CHUNK, wv), lambda b, s: (blk(b, s), 0)),
        ]

    out_spec = lambda d: pl.BlockSpec((CHUNK, wv), lambda b, s: (_latent_chunk_block(b, s, d), 0))
    y_shape = jax.ShapeDtypeStruct((nb * SEQ, wv), BF16)
    return pl.pallas_call(
        _ret_kernel,
        out_shape=(y_shape, y_shape),
        grid=(nb, NCC + NLC),
        in_specs=direction_specs(0) + direction_specs(1) + [
            pl.BlockSpec((2, 1, RET_HEADS), lambda b, s: (0, 0, 0))],
        out_specs=(out_spec(0), out_spec(1)),
        scratch_shapes=[pltpu.VMEM((2, RET_HEADS, LANES, LANES), F32)],
        compiler_params=_cp(("arbitrary", "arbitrary")),
        name="retention_scan",
    )(*([rp, rp, pp] * 2), log_decay.astype(F32).reshape(2, 1, RET_HEADS))


def _ssm_out_kernel(yf_ref, yb_ref, xs_ref, z_ref, dsk_ref, g_ref, o_ref):
    y = yf_ref[...].astype(F32) + yb_ref[...].astype(F32) + xs_ref[...].astype(F32) * dsk_ref[...]
    yz = y * _silu(z_ref[...].astype(F32))
    half = SSM_INNER // 2
    for g in range(2):
        seg = yz[:, g * half:(g + 1) * half]
        ms = jnp.mean(seg * seg, axis=-1, keepdims=True)
        o_ref[:, g * half:(g + 1) * half] = (
            seg * lax.rsqrt(ms + EPS) * g_ref[:, g * half:(g + 1) * half]).astype(o_ref.dtype)


def _ssm_out(yf, yb, xbc, big, d_skip, gn_g, tm):
    rows = xbc.shape[0]
    dsk = jnp.repeat(d_skip.astype(F32), HD).reshape(1, SSM_INNER)
    return pl.pallas_call(
        _ssm_out_kernel,
        out_shape=jax.ShapeDtypeStruct((rows, SSM_INNER), BF16),
        grid=(rows // tm,),
        in_specs=[
            pl.BlockSpec((tm, SSM_INNER), lambda i: (i, 0)),
            pl.BlockSpec((tm, SSM_INNER), lambda i: (i, 0)),
            pl.BlockSpec((tm, SSM_INNER), lambda i: (i, 0)),
            pl.BlockSpec((tm, SSM_INNER), lambda i: (i, 0)),
            pl.BlockSpec((1, SSM_INNER), lambda i: (0, 0)),
            pl.BlockSpec((1, SSM_INNER), lambda i: (0, 0)),
        ],
        out_specs=pl.BlockSpec((tm, SSM_INNER), lambda i: (i, 0)),
        compiler_params=_cp(("parallel",)),
        name="ssm_out",
    )(yf, yb, xbc, big, dsk, gn_g.reshape(1, SSM_INNER))


def _ret_out_kernel(rf_ref, rb_ref, rg_ref, g_ref, b_ref, o_ref):
    y = rf_ref[...].astype(F32) + rb_ref[...].astype(F32)
    for h in range(RET_HEADS):
        sl = slice(h * LANES, (h + 1) * LANES)
        seg = y[:, sl]
        mu = jnp.mean(seg, axis=-1, keepdims=True)
        cen = seg - mu
        var = jnp.mean(cen * cen, axis=-1, keepdims=True)
        yn = cen * lax.rsqrt(var + EPS) * g_ref[:, sl] + b_ref[:, sl]
        o_ref[:, sl] = (_silu(rg_ref[:, sl].astype(F32)) * yn).astype(o_ref.dtype)


def _ret_out(rf, rb, pp, gn_g, gn_b, nb, nctx, tm):
    rows = nb * SEQ
    w = RET_HEADS * LANES
    off = nctx // tm
    return pl.pallas_call(
        _ret_out_kernel,
        out_shape=jax.ShapeDtypeStruct((rows, w), BF16),
        grid=(rows // tm,),
        in_specs=[
            pl.BlockSpec((tm, w), lambda i: (i, 0)),
            pl.BlockSpec((tm, w), lambda i: (i, 0)),
            pl.BlockSpec((tm, w), lambda i: (off + i, 1)),
            pl.BlockSpec((1, w), lambda i: (0, 0)),
            pl.BlockSpec((1, w), lambda i: (0, 0)),
        ],
        out_specs=pl.BlockSpec((tm, w), lambda i: (i, 0)),
        compiler_params=_cp(("parallel",)),
        name="ret_out",
    )(rf, rb, pp, gn_g.reshape(1, w), gn_b.reshape(1, w))


def _outproj_kernel(x_ref, a1_ref, *rest, n_ctx_tiles):
    if n_ctx_tiles:
        a1c_ref, a2_ref, w1_ref, w2_ref, mod_ref, o_ref = rest
        a1 = jnp.where(pl.program_id(0) < n_ctx_tiles, a1c_ref[...], a1_ref[...])
    else:
        a2_ref, w1_ref, w2_ref, mod_ref, o_ref = rest
        a1 = a1_ref[...]
    y = (jnp.dot(a1, w1_ref[...], preferred_element_type=F32)
         + jnp.dot(a2_ref[...], w2_ref[...], preferred_element_type=F32))
    o_ref[...] = x_ref[...] + mod_ref[0][2:3] * y


def _outproj(x, a1, a2, w_out, mod, nctx, tm, x_off, a1_ctx=None):
    rows = a2.shape[0]
    k1, k2 = a1.shape[1], a2.shape[1]
    w1 = w_out[:k1].astype(BF16)
    w2 = w_out[k1:].astype(BF16)
    nct = 0 if a1_ctx is None else nctx // tm
    in_specs = [
        pl.BlockSpec((tm, D), lambda i: (i + x_off, 0)),
        pl.BlockSpec((tm, k1), lambda i: (jnp.maximum(i - nct, 0), 0)),
    ]
    args = [x, a1]
    if a1_ctx is not None:
        in_specs.append(pl.BlockSpec((tm, k1), lambda i: (jnp.minimum(i, nct - 1), 0)))
        args.append(a1_ctx)
    in_specs += [
        pl.BlockSpec((tm, k2), lambda i: (i, 0)),
        pl.BlockSpec((k1, D), lambda i: (0, 0)),
        pl.BlockSpec((k2, D), lambda i: (0, 0)),
        pl.BlockSpec((1, 6, D), lambda i: (_mod_row((i + x_off) * tm, nctx), 0, 0)),
    ]
    args += [a2, w1, w2, mod]
    return pl.pallas_call(
        functools.partial(_outproj_kernel, n_ctx_tiles=nct),
        out_shape=jax.ShapeDtypeStruct((rows, D), F32),
        grid=(rows // tm,),
        in_specs=in_specs,
        out_specs=pl.BlockSpec((tm, D), lambda i: (i, 0)),
        compiler_params=_cp(("parallel",)),
        name="outproj",
    )(*args)


def _top2_gates(logits):
    lane = lax.broadcasted_iota(jnp.int32, logits.shape, 1)
    valid = lane < N_EXPERTS
    l0 = jnp.where(valid, logits, NEG)
    m1 = jnp.max(l0, axis=-1, keepdims=True)
    i1 = jnp.min(jnp.where(l0 == m1, lane, LANES), axis=-1, keepdims=True)
    l1 = jnp.where(lane == i1, NEG, l0)
    m2 = jnp.max(l1, axis=-1, keepdims=True)
    i2 = jnp.min(jnp.where(l1 == m2, lane, LANES), axis=-1, keepdims=True)
    e2 = jnp.exp(m2 - m1)
    g1 = 1.0 / (1.0 + e2)
    g2 = e2 / (1.0 + e2)
    return jnp.where(lane == i1, g1, 0.0) + jnp.where(lane == i2, g2, 0.0)


def _ffn_kernel(x_ref, g_ref, mod_ref, *rest, routed, final):
    rest = list(rest)
    router_ref = rest.pop(0) if routed else None
    w1_ref, w3_ref, w2_ref = rest[:3]
    rest = rest[3:]
    fg_ref = rest.pop(0) if final else None
    o_ref, h_ref, acc_ref = rest[:3]
    gate_ref = rest[3] if routed else None
    e = pl.program_id(1)
    f = pl.program_id(2)
    first = jnp.logical_and(e == 0, f == 0)
    last = jnp.logical_and(e == pl.num_programs(1) - 1, f == pl.num_programs(2) - 1)

    @pl.when(first)
    def _():
        h = _norm_mod(x_ref[...], g_ref[...], mod_ref[0], 3)
        h16 = h.astype(BF16)
        h_ref[...] = h16
        acc_ref[...] = jnp.zeros_like(acc_ref)
        if routed:
            hl = (h - h16.astype(F32)).astype(BF16)
            r = router_ref[...]
            rh = r.astype(BF16)
            rl = (r - rh.astype(F32)).astype(BF16)
            logits = (jnp.dot(h16, rh, preferred_element_type=F32)
                      + jnp.dot(h16, rl, preferred_element_type=F32)
                      + jnp.dot(hl, rh, preferred_element_type=F32))
            gate_ref[...] = _top2_gates(logits)

    h16 = h_ref[...]
    u = _silu(jnp.dot(h16, w1_ref[0], preferred_element_type=F32)) * jnp.dot(
        h16, w3_ref[0], preferred_element_type=F32)
    if routed:
        lane = lax.broadcasted_iota(jnp.int32, gate_ref.shape, 1)
        ge = jnp.sum(jnp.where(lane == e, gate_ref[...], 0.0), axis=-1, keepdims=True)
        u = u * ge
    acc_ref[...] += jnp.dot(u.astype(BF16), w2_ref[0], preferred_element_type=F32)

    @pl.when(last)
    def _():
        y = x_ref[...] + mod_ref[0][5:6] * acc_ref[...]
        if final:
            ms = jnp.mean(y * y, axis=-1, keepdims=True)
            y = y * lax.rsqrt(ms + EPS) * fg_ref[...]
        o_ref[...] = y


def _ffn(x, g, mod, w1, w3, w2, nctx, tm, tf, x_off=0, rows=None, router=None, final_g=None):
    rows = x.shape[0] if rows is None else rows
    ne, _, dff = w1.shape
    routed = router is not None
    final = final_g is not None
    in_specs = [
        pl.BlockSpec((tm, D), lambda i, e, f: (i + x_off, 0)),
        pl.BlockSpec((1, D), lambda i, e, f: (0, 0)),
        pl.BlockSpec((1, 6, D), lambda i, e, f: (_mod_row((i + x_off) * tm, nctx), 0, 0)),
    ]
    args = [x, g.reshape(1, D), mod]
    if routed:
        in_specs.append(pl.BlockSpec((D, LANES), lambda i, e, f: (0, 0)))
        args.append(jnp.pad(router.astype(F32), ((0, 0), (0, LANES - ne))))
    in_specs += [
        pl.BlockSpec((1, D, tf), lambda i, e, f: (e, 0, f)),
        pl.BlockSpec((1, D, tf), lambda i, e, f: (e, 0, f)),
        pl.BlockSpec((1, tf, D), lambda i, e, f: (e, f, 0)),
    ]
    args += [w1, w3, w2]
    if final:
        in_specs.append(pl.BlockSpec((1, D), lambda i, e, f: (0, 0)))
        args.append(final_g.reshape(1, D))
    scratch = [pltpu.VMEM((tm, D), BF16), pltpu.VMEM((tm, D), F32)]
    if routed:
        scratch.append(pltpu.VMEM((tm, LANES), F32))
    return pl.pallas_call(
        functools.partial(_ffn_kernel, routed=routed, final=final),
        out_shape=jax.ShapeDtypeStruct((rows, D), F32),
        grid=(rows // tm, ne, dff // tf),
        in_specs=in_specs,
        out_specs=pl.BlockSpec((tm, D), lambda i, e, f: (i, 0)),
        scratch_shapes=scratch,
        compiler_params=_cp(("parallel", "arbitrary", "arbitrary")),
        name="moe_ffn" if routed else "dense_ffn",
    )(*args)


def _rope_tables(tm):
    t = np.arange(SEQ)
    n_freq = HD // 4
    inv = ROPE_BASE ** (-np.arange(n_freq, dtype=np.float64) / n_freq)
    ang = np.concatenate([(t // GRID_W)[:, None] * inv, (t % GRID_W)[:, None] * inv], axis=-1)
    cos = np.tile(np.cos(ang), (1, 4))
    sin = np.tile(np.concatenate([-np.sin(ang), np.sin(ang)], axis=-1), (1, 2))
    cos = np.concatenate([np.ones((tm, LANES)), cos], axis=0)
    sin = np.concatenate([np.zeros((tm, LANES)), sin], axis=0)
    return jnp.asarray(cos, F32), jnp.asarray(sin, F32)


def _dup_heads(w, nheads):
    w = w.reshape(D, nheads, 1, HD)
    return jnp.broadcast_to(w, (D, nheads, 2, HD)).reshape(D, nheads * 2 * HD)


def kernel(x, c, ctx, c_ctx, ada_w, ada_b, norm_attn_g, norm_ffn_g, ev_w_in, ev_w_out, na_rpb,
           ssm_conv_w, ssm_conv_b, ssm_dt_bias, ssm_a_log, ssm_d, ssm_norm_g, ffn_w1, ffn_w3, ffn_w2,
           od_w_in, od_w_out, swa_sink, ret_log_decay, ret_gn_g, ret_gn_b, moe_router, moe_w1,
           moe_w3, moe_w2, final_g):
    nb = x.shape[0]
    nctx = -(-(nb * CTX) // SEQ) * SEQ
    tm = 1024
    xs = jnp.concatenate([ctx.reshape(nb * CTX, D), jnp.zeros((nctx - nb * CTX, D), ctx.dtype),
                          x.reshape(nb * SEQ, D)], axis=0).astype(F32)

    rp = -(-(nb + 1) // 8) * 8
    cin = jnp.zeros((rp, D), F32).at[0].set(c_ctx).at[1:nb + 1].set(c)
    mod = _modulation(cin, ada_w, ada_b).reshape(2, rp, 6, D)

    w_in = ev_w_in[0]
    q_, k_, v_, z_, xbc_, dt_ = jnp.split(w_in, [512, 1024, 1536, 2560, 4096], axis=1)
    w_big = jnp.concatenate([z_, q_, k_, v_, xbc_], axis=1).astype(BF16)
    big = _proj(xs, norm_attn_g[0], mod[0], w_big, BF16, nctx, tm, 2048, name="even_in_proj")
    dt_raw = _proj(xs, norm_attn_g[0], mod[0], dt_.astype(BF16), F32, nctx, tm, 2 * SSM_HEADS,
                   name="even_dt_proj")
    attn = _na_attention(big, _na_bias_blocks(na_rpb[0]), nb, nctx)
    attn_ctx = _ctx_attention(big, nctx)
    xbc = _conv_silu(big, ssm_conv_w[0], ssm_conv_b[0], nctx)
    yf, yb = _ssd_scan(xbc, dt_raw, ssm_dt_bias[0], ssm_a_log[0], nb, nctx)
    ys = _ssm_out(yf, yb, xbc, big, ssm_d[0], ssm_norm_g[0], tm)
    xs = _outproj(xs, attn, ys, ev_w_out[0], mod[0], nctx, tm, 0, a1_ctx=attn_ctx)
    xs = _ffn(xs, norm_ffn_g[0], mod[0], ffn_w1.astype(BF16), ffn_w3.astype(BF16),
              ffn_w2.astype(BF16), nctx, tm, 256)

    w_in = od_w_in[0]
    q_, k_, v_, rq_, rk_, rv_, rg_ = jnp.split(w_in, [512, 640, 768, 1280, 1792, 2816], axis=1)
    w_rope = jnp.concatenate([q_, rq_, rk_, _dup_heads(k_, 2)], axis=1).astype(BF16)
    w_plain = jnp.concatenate([rv_, rg_, _dup_heads(v_, 2)], axis=1).astype(BF16)
    rope_tabs = _rope_tables(tm)
    rpj = _proj(xs, norm_attn_g[1], mod[1], w_rope, BF16, nctx, tm, w_rope.shape[1],
                rope=rope_tabs, name="odd_rope_proj")
    ppj = _proj(xs, norm_attn_g[1], mod[1], w_plain, BF16, nctx, tm, w_plain.shape[1],
                name="odd_plain_proj")
    yw = _swa_attention(rpj, ppj, swa_sink[0], nb, nctx)
    rf, rb = _retention_scan(rpj, ppj, ret_log_decay[0], nb, nctx)
    yr = _ret_out(rf, rb, ppj, ret_gn_g[0], ret_gn_b[0], nb, nctx, tm)
    x_off = nctx // tm
    xl = _outproj(xs, yw, yr, od_w_out[0], mod[1], nctx, tm, x_off)
    out = _ffn(xl, norm_ffn_g[1], mod[1], moe_w1[0].astype(BF16), moe_w3[0].astype(BF16),
               moe_w2[0].astype(BF16), 0, tm, 512, router=moe_router[0], final_g=final_g)
    return out.reshape(nb, SEQ, D).astype(x.dtype)
```

```python
import functools

import numpy as np
import jax
import jax.numpy as jnp
from jax import lax
from jax.experimental import pallas as pl
from jax.experimental.pallas import tpu as pltpu

F32 = jnp.float32
BF16 = jnp.bfloat16

D = 1024
SEQ = 2048
CTX = 256
GRID_W = 64
HD = 64
EPS = 1e-6
ROPE_BASE = 10000.0
NEG = -1e30

NA_HEADS = 8
NA_ROWS = 8
NA_COLS = 16
NA_QROWS = 4
NA_KROWS = 12
SSM_HEADS = 16
SSM_INNER = 1024
SSM_STATE = 128
SSM_CONV = 5
SSM_CONV_CH = 1536
CHUNK = 128
SWA_HEADS = 8
SWA_WINDOW = 128
SWA_QT = 512
SWA_KT = SWA_QT + 2 * SWA_WINDOW
RET_HEADS = 8
D_FF = 2816
N_EXPERTS = 8
D_FF_EXPERT = 3584

EV_Q_BLK, EV_K_BLK, EV_V_BLK = 12, 16, 20
EV_Z_BLK = 3

LANES = 128
VMEM_LIMIT = 56 * 1024 * 1024


def _cp(sem, vmem=VMEM_LIMIT):
    return pltpu.CompilerParams(dimension_semantics=sem, vmem_limit_bytes=vmem)


def _sigmoid(x):
    return 1.0 / (1.0 + jnp.exp(-x))


def _silu(x):
    return x * _sigmoid(x)


def _mod_row(start, nctx):
    return jnp.where(start < nctx, 0, 1 + (start - nctx) // SEQ)


def _mod_kernel(c_ref, w_ref, b_ref, o_ref):
    c = c_ref[...]
    h = _silu(c).astype(BF16)
    o_ref[0] = jnp.dot(h, w_ref[0].astype(BF16), preferred_element_type=F32) + b_ref[0]


def _modulation(cin, ada_w, ada_b):
    depth, _, n6 = ada_w.shape
    rp = cin.shape[0]
    tn = 1024
    return pl.pallas_call(
        _mod_kernel,
        out_shape=jax.ShapeDtypeStruct((depth, rp, n6), F32),
        grid=(depth, n6 // tn),
        in_specs=[
            pl.BlockSpec((rp, D), lambda l, j: (0, 0)),
            pl.BlockSpec((1, D, tn), lambda l, j: (l, 0, j)),
            pl.BlockSpec((1, 1, tn), lambda l, j: (l, 0, j)),
        ],
        out_specs=pl.BlockSpec((1, rp, tn), lambda l, j: (l, 0, j)),
        compiler_params=_cp(("arbitrary", "arbitrary")),
        name="adaln_mod",
    )(cin, ada_w, ada_b.reshape(depth, 1, n6))


def _norm_mod(x, g, m, k):
    ms = jnp.mean(x * x, axis=-1, keepdims=True)
    y = x * lax.rsqrt(ms + EPS) * g
    return y * (1.0 + m[k + 1:k + 2]) + m[k:k + 1]


def _swap32(r):
    lane = lax.broadcasted_iota(jnp.int32, r.shape, 1)
    return jnp.where((lane % 64) < 32, pltpu.roll(r, 96, 1), pltpu.roll(r, 32, 1))


def _proj_kernel(x_ref, g_ref, mod_ref, w_ref, *rest, rope):
    if rope:
        c_ref, s_ref, o_ref, h_ref = rest
    else:
        o_ref, h_ref = rest

    @pl.when(pl.program_id(1) == 0)
    def _():
        h_ref[...] = _norm_mod(x_ref[...], g_ref[...], mod_ref[0], 0).astype(BF16)

    r = jnp.dot(h_ref[...], w_ref[...], preferred_element_type=F32)
    if rope:
        c = c_ref[...]
        s = s_ref[...]
        for k in range(r.shape[1] // LANES):
            rk = r[:, k * LANES:(k + 1) * LANES]
            o_ref[:, k * LANES:(k + 1) * LANES] = (rk * c + _swap32(rk) * s).astype(o_ref.dtype)
    else:
        o_ref[...] = r.astype(o_ref.dtype)


def _proj(x, g, mod, w, out_dtype, nctx, tm, tn, rope=None, name="proj"):
    rows = x.shape[0]
    n = w.shape[1]
    in_specs = [
        pl.BlockSpec((tm, D), lambda i, j: (i, 0)),
        pl.BlockSpec((1, D), lambda i, j: (0, 0)),
        pl.BlockSpec((1, 6, D), lambda i, j: (_mod_row(i * tm, nctx), 0, 0)),
        pl.BlockSpec((D, tn), lambda i, j: (0, j)),
    ]
    args = [x, g.reshape(1, D), mod, w]
    if rope is not None:
        nct = nctx // tm
        per = SEQ // tm

        def tab_map(i, j):
            return (jnp.where(i < nct, 0, 1 + (i - nct) % per), 0)
        in_specs += [pl.BlockSpec((tm, LANES), tab_map), pl.BlockSpec((tm, LANES), tab_map)]
        args += list(rope)
    return pl.pallas_call(
        functools.partial(_proj_kernel, rope=rope is not None),
        out_shape=jax.ShapeDtypeStruct((rows, n), out_dtype),
        grid=(rows // tm, n // tn),
        in_specs=in_specs,
        out_specs=pl.BlockSpec((tm, tn), lambda i, j: (i, j)),
        scratch_shapes=[pltpu.VMEM((tm, D), BF16)],
        compiler_params=_cp(("parallel", "arbitrary")),
        name=name,
    )(*args)


def _head_mask(x, a):
    lane = lax.broadcasted_iota(jnp.int32, (1, LANES), 1)
    keep = (lane < HD) if a == 0 else (lane >= HD)
    return jnp.where(keep, x, jnp.zeros_like(x))


def _nt(a, b):
    return lax.dot_general(a, b, (((1,), (1,)), ((), ())), preferred_element_type=F32)


def _two_part_attention(q, kc, vc, kl, vl, bias_fn, sink_fn):
    lane = lax.broadcasted_iota(jnp.int32, (1, LANES), 1)
    outs = []
    for a in range(2):
        qa = _head_mask(q, a) * jnp.asarray(HD ** -0.5, q.dtype)
        s_c = _nt(qa, kc)
        s_l = bias_fn(a, _nt(qa, kl))
        m = jnp.maximum(jnp.max(s_c, axis=-1, keepdims=True), jnp.max(s_l, axis=-1, keepdims=True))
        sink = sink_fn(a)
        if sink is not None:
            m = jnp.maximum(m, sink)
        p_c = jnp.exp(s_c - m)
        p_l = jnp.exp(s_l - m)
        den = jnp.sum(p_c, axis=-1, keepdims=True) + jnp.sum(p_l, axis=-1, keepdims=True)
        if sink is not None:
            den = den + jnp.exp(sink - m)
        o = (jnp.dot(p_c.astype(BF16), vc, preferred_element_type=F32)
             + jnp.dot(p_l.astype(BF16), vl, preferred_element_type=F32))
        outs.append(o / den)
    return jnp.where(lane < HD, outs[0], outs[1])


def _na_window_start(t):
    return jnp.clip(NA_QROWS * t - NA_ROWS // 2, 0, SEQ // GRID_W - NA_KROWS)


NA_NDROW = 2 * NA_ROWS - 1


def _na_bias_tile(toe_ref, a, t):
    lane = lax.broadcasted_iota(jnp.int32, (1, LANES), 1)
    nrows = SEQ // GRID_W
    w0r = _na_window_start(t)
    tile_rows = []
    for rq in range(NA_QROWS):
        r = NA_QROWS * t + rq
        r0 = jnp.clip(r - NA_ROWS // 2, 0, nrows - NA_ROWS)
        blocks = []
        for j in range(NA_KROWS // 2):
            kr = w0r + 2 * j
            e = jnp.clip(kr - r + NA_ROWS, 0, NA_NDROW)
            pen0 = jnp.where(jnp.logical_and(kr >= r0, kr < r0 + NA_ROWS), 0.0, NEG)
            pen1 = jnp.where(jnp.logical_and(kr + 1 >= r0, kr + 1 < r0 + NA_ROWS), 0.0, NEG)
            blocks.append(toe_ref[a, e] + jnp.where(lane < GRID_W, pen0, pen1))
        tile_rows.append(jnp.concatenate(blocks, axis=1))
    return jnp.concatenate(tile_rows, axis=0)


def _na_kernel(q_ref, kl_ref, vl_ref, kc_ref, vc_ref, toe_ref, o_ref):
    t = pl.program_id(1)
    w0 = pl.multiple_of(_na_window_start(t) * GRID_W, GRID_W)
    nk = NA_KROWS * GRID_W
    kl = kl_ref[pl.ds(w0, nk), :]
    vl = vl_ref[pl.ds(w0, nk), :]
    o = _two_part_attention(q_ref[...], kc_ref[...], vc_ref[...], kl, vl,
                            lambda a, s: s + _na_bias_tile(toe_ref, a, t), lambda a: None)
    o_ref[...] = o.astype(o_ref.dtype)


def _na_bias_kernel(rpb_ref, o_ref):
    c = lax.broadcasted_iota(jnp.int32, (GRID_W, LANES), 0)
    lane = lax.broadcasted_iota(jnp.int32, (GRID_W, LANES), 1)
    kc = lane % GRID_W
    c0 = jnp.clip(c - NA_COLS // 2, 0, GRID_W - NA_COLS)
    col_ok = jnp.logical_and(kc >= c0, kc < c0 + NA_COLS)
    neg = jnp.full((GRID_W, LANES), NEG, F32)

    def toeplitz(d, shift):
        v = jnp.broadcast_to(rpb_ref[0, d:d + 1, :], (GRID_W, LANES))
        return pltpu.roll(v, shift % LANES, 1, stride=1, stride_axis=0)

    for e in range(NA_NDROW + 1):
        lo = toeplitz(e - 1, -(NA_COLS - 1)) if e >= 1 else neg
        hi = toeplitz(e, GRID_W - (NA_COLS - 1)) if e < NA_NDROW else neg
        o_ref[0, e] = jnp.where(col_ok, jnp.where(lane < GRID_W, lo, hi), NEG)


def _na_bias_blocks(rpb):
    nd, ncol = rpb.shape[1], rpb.shape[2]
    padded = jnp.pad(rpb.astype(F32), ((0, 0), (0, NA_NDROW + 1 - nd), (0, LANES - ncol)))
    return pl.pallas_call(
        _na_bias_kernel,
        out_shape=jax.ShapeDtypeStruct((NA_HEADS, NA_NDROW + 1, GRID_W, LANES), F32),
        grid=(NA_HEADS,),
        in_specs=[pl.BlockSpec((1, NA_NDROW + 1, LANES), lambda h: (h, 0, 0))],
        out_specs=pl.BlockSpec((1, NA_NDROW + 1, GRID_W, LANES), lambda h: (h, 0, 0, 0)),
        compiler_params=_cp(("arbitrary",)),
        name="na_bias_blocks",
    )(padded)


def _na_attention(big, bias, nb, nctx):
    tq = NA_QROWS * GRID_W
    ntile = SEQ // tq
    nct_q = nctx // tq
    lat0 = nctx // SEQ
    return pl.pallas_call(
        _na_kernel,
        out_shape=jax.ShapeDtypeStruct((nb * SEQ, NA_HEADS * HD), BF16),
        grid=(NA_HEADS // 2, ntile, nb),
        in_specs=[
            pl.BlockSpec((tq, LANES), lambda h, t, b: (nct_q + b * ntile + t, EV_Q_BLK + h)),
            pl.BlockSpec((SEQ, LANES), lambda h, t, b: (lat0 + b, EV_K_BLK + h)),
            pl.BlockSpec((SEQ, LANES), lambda h, t, b: (lat0 + b, EV_V_BLK + h)),
            pl.BlockSpec((CTX, LANES), lambda h, t, b: (b, EV_K_BLK + h)),
            pl.BlockSpec((CTX, LANES), lambda h, t, b: (b, EV_V_BLK + h)),
            pl.BlockSpec((2, NA_NDROW + 1, GRID_W, LANES), lambda h, t, b: (h, 0, 0, 0)),
        ],
        out_specs=pl.BlockSpec((tq, LANES), lambda h, t, b: (b * ntile + t, h)),
        compiler_params=_cp(("arbitrary", "arbitrary", "arbitrary")),
        name="na_attention",
    )(big, big, big, big, big, bias)


def _ctx_attn_kernel(q_ref, k_ref, v_ref, o_ref):
    lane = lax.broadcasted_iota(jnp.int32, (1, LANES), 1)
    q = q_ref[...]
    k = k_ref[...]
    v = v_ref[...]
    outs = []
    for a in range(2):
        qa = _head_mask(q, a) * jnp.asarray(HD ** -0.5, q.dtype)
        s = _nt(qa, k)
        m = jnp.max(s, axis=-1, keepdims=True)
        p = jnp.exp(s - m)
        den = jnp.sum(p, axis=-1, keepdims=True)
        outs.append(jnp.dot(p.astype(BF16), v, preferred_element_type=F32) / den)
    o_ref[...] = jnp.where(lane < HD, outs[0], outs[1]).astype(o_ref.dtype)


def _ctx_attention(big, nctx):
    return pl.pallas_call(
        _ctx_attn_kernel,
        out_shape=jax.ShapeDtypeStruct((nctx, NA_HEADS * HD), BF16),
        grid=(nctx // CTX, NA_HEADS // 2),
        in_specs=[
            pl.BlockSpec((CTX, LANES), lambda b, h: (b, EV_Q_BLK + h)),
            pl.BlockSpec((CTX, LANES), lambda b, h: (b, EV_K_BLK + h)),
            pl.BlockSpec((CTX, LANES), lambda b, h: (b, EV_V_BLK + h)),
        ],
        out_specs=pl.BlockSpec((CTX, LANES), lambda b, h: (b, h)),
        compiler_params=_cp(("arbitrary", "arbitrary")),
        name="ctx_attention",
    )(big, big, big)


def _swa_kernel(q_ref, kl_ref, vl_ref, kc_ref, vc_ref, sink_ref, o_ref):
    h2 = pl.program_id(1)
    n = pl.program_id(2)
    start = pl.multiple_of(jnp.clip(SWA_QT * n - SWA_WINDOW, 0, SEQ - SWA_KT), SWA_WINDOW)
    kl = kl_ref[pl.ds(start, SWA_KT), :]
    vl = vl_ref[pl.ds(start, SWA_KT), :]
    qpos = SWA_QT * n + lax.broadcasted_iota(jnp.int32, (SWA_QT, 1), 0)
    kpos = start + lax.broadcasted_iota(jnp.int32, (1, SWA_KT), 1)
    ok = jnp.abs(kpos - qpos) <= SWA_WINDOW

    def sink(a):
        return sink_ref[pl.ds(2 * h2 + a, 1), :][:, :1]

    o = _two_part_attention(q_ref[...], kc_ref[...], vc_ref[...], kl, vl,
                            lambda a, s: jnp.where(ok, s, NEG), sink)
    o_ref[...] = o.astype(o_ref.dtype)


def _swa_attention(rp, pp, sink, nb, nctx):
    nq = SEQ // SWA_QT
    nct_q = nctx // SWA_QT
    lat0 = nctx // SEQ
    sinkv = jnp.broadcast_to(sink.astype(F32)[:, None], (SWA_HEADS, LANES))
    return pl.pallas_call(
        _swa_kernel,
        out_shape=jax.ShapeDtypeStruct((nb * SEQ, SWA_HEADS * HD), BF16),
        grid=(nb, SWA_HEADS // 2, nq),
        in_specs=[
            pl.BlockSpec((SWA_QT, LANES), lambda b, h, n: (nct_q + b * nq + n, h)),
            pl.BlockSpec((SEQ, LANES), lambda b, h, n: (lat0 + b, 12 + h // 2)),
            pl.BlockSpec((SEQ, LANES), lambda b, h, n: (lat0 + b, 16 + h // 2)),
            pl.BlockSpec((CTX, LANES), lambda b, h, n: (b, 12 + h // 2)),
            pl.BlockSpec((CTX, LANES), lambda b, h, n: (b, 16 + h // 2)),
            pl.BlockSpec((SWA_HEADS, LANES), lambda b, h, n: (0, 0)),
        ],
        out_specs=pl.BlockSpec((SWA_QT, LANES), lambda b, h, n: (b * nq + n, h)),
        compiler_params=_cp(("arbitrary", "arbitrary", "arbitrary")),
        name="swa_attention",
    )(rp, rp, pp, rp, pp, sinkv)


CONV_PAD = (SSM_CONV - 1) // 2
CONV_TAPS = tuple(s for s in range(-CONV_PAD, CONV_PAD + 1) if s != 0)
CONV_HALO = 16
CONV_EDGE = 8


def _conv_shift_tables(tb):
    nt = len(CONV_TAPS)
    shift = np.zeros((nt * tb, tb), np.float32)
    edge = np.zeros((nt * 2 * CONV_EDGE, 2 * CONV_HALO), np.float32)
    for t, s in enumerate(CONV_TAPS):
        for i in range(tb):
            j = i + s
            if 0 <= j < tb:
                shift[t * tb + i, j] = 1.0
            elif j < 0:
                edge[t * 2 * CONV_EDGE + i, CONV_HALO + j] = 1.0
            else:
                edge[t * 2 * CONV_EDGE + CONV_EDGE + i - (tb - CONV_EDGE), CONV_HALO + j - tb] = 1.0
    return jnp.asarray(shift, BF16), jnp.asarray(edge, BF16)


def _conv_kernel(xp_ref, x_ref, xn_ref, sh_ref, ed_ref, w_ref, b_ref, o_ref, *, nctx_blocks,
                 per_seq):
    i = pl.program_id(0)
    tb = x_ref.shape[0]
    k = i - nctx_blocks
    has_prev = jnp.logical_and(i >= nctx_blocks, k % per_seq != 0)
    has_next = jnp.logical_and(i >= nctx_blocks, k % per_seq != per_seq - 1)
    x = x_ref[...]
    zero = jnp.zeros((CONV_HALO, x.shape[1]), x.dtype)
    halo = jnp.concatenate([jnp.where(has_prev, xp_ref[...], zero),
                            jnp.where(has_next, xn_ref[...], zero)], axis=0)
    taps = jnp.dot(sh_ref[...], x, preferred_element_type=F32)
    edges = jnp.dot(ed_ref[...], halo, preferred_element_type=F32)
    acc = x.astype(F32) * w_ref[CONV_PAD:CONV_PAD + 1, :] + b_ref[...]
    for t, s in enumerate(CONV_TAPS):
        tap = taps[t * tb:(t + 1) * tb]
        e0 = 2 * CONV_EDGE * t
        tap = jnp.concatenate([tap[:CONV_EDGE] + edges[e0:e0 + CONV_EDGE],
                               tap[CONV_EDGE:tb - CONV_EDGE],
                               tap[tb - CONV_EDGE:] + edges[e0 + CONV_EDGE:e0 + 2 * CONV_EDGE]], axis=0)
        acc = acc + tap * w_ref[CONV_PAD + s:CONV_PAD + s + 1, :]
    o_ref[...] = _silu(acc).astype(o_ref.dtype)


def _conv_silu(big, conv_w, conv_b, nctx):
    rows = big.shape[0]
    tb = CTX
    tc = SSM_CONV_CH
    nblk = rows // tb
    hpb = tb // CONV_HALO
    shift, edge = _conv_shift_tables(tb)
    return pl.pallas_call(
        functools.partial(_conv_kernel, nctx_blocks=nctx // tb, per_seq=SEQ // tb),
        out_shape=jax.ShapeDtypeStruct((rows, SSM_CONV_CH), BF16),
        grid=(nblk, SSM_CONV_CH // tc),
        in_specs=[
            pl.BlockSpec((CONV_HALO, tc), lambda i, c: (jnp.maximum(i * hpb - 1, 0), c)),
            pl.BlockSpec((tb, tc), lambda i, c: (i, c)),
            pl.BlockSpec((CONV_HALO, tc),
                         lambda i, c: (jnp.minimum((i + 1) * hpb, nblk * hpb - 1), c)),
            pl.BlockSpec(shift.shape, lambda i, c: (0, 0)),
            pl.BlockSpec(edge.shape, lambda i, c: (0, 0)),
            pl.BlockSpec((SSM_CONV, tc), lambda i, c: (0, c)),
            pl.BlockSpec((1, tc), lambda i, c: (0, c)),
        ],
        out_specs=pl.BlockSpec((tb, tc), lambda i, c: (i, c)),
        compiler_params=_cp(("arbitrary", "arbitrary")),
        name="conv_silu",
    )(big, big, big, shift, edge, conv_w.reshape(SSM_CONV, SSM_CONV_CH),
      conv_b.reshape(1, SSM_CONV_CH))


NCC = CTX // CHUNK
NLC = SEQ // CHUNK


def _chunk_block(b, s, d, nctx):
    cc = s if d == 0 else NCC - 1 - s
    lc = s - NCC if d == 0 else NLC + NCC - 1 - s
    return jnp.where(s < NCC, NCC * b + cc, nctx // CHUNK + NLC * b + lc)


def _latent_chunk_block(b, s, d):
    lc = s - NCC if d == 0 else NLC + NCC - 1 - s
    return NLC * b + jnp.clip(lc, 0, NLC - 1)


def _split3(a):
    hi = a.astype(BF16)
    r1 = a - hi.astype(F32)
    mid = r1.astype(BF16)
    lo = (r1 - mid.astype(F32)).astype(BF16)
    return hi, mid, lo


def _softplus(x):
    return jnp.maximum(x, 0.0) + jnp.log(1.0 + jnp.exp(-jnp.abs(x)))


def _ssd_direction(d, x_ref, b_ref, c_ref, dt_ref, dtt_ref, bias_ref, biast_ref, alog_ref,
                   alogt_ref, y_ref, s_ref):
    q = CHUNK
    ii = lax.broadcasted_iota(jnp.int32, (q, q), 0)
    jj = lax.broadcasted_iota(jnp.int32, (q, q), 1)
    causal = (jj <= ii) if d == 0 else (jj >= ii)
    tri = jnp.where(causal, 1.0, 0.0).astype(BF16)
    trit = jnp.where((ii <= jj) if d == 0 else (ii >= jj), 1.0, 0.0).astype(BF16)

    dt = _softplus(dt_ref[0] + bias_ref[d])
    a = dt * (-jnp.exp(alog_ref[d]))
    dtt = _softplus(dtt_ref[0] + biast_ref[d])
    at = dtt * (-jnp.exp(alogt_ref[d]))
    acum = sum(jnp.dot(tri, p, preferred_element_type=F32) for p in _split3(a))
    acumt = sum(jnp.dot(p, trit, preferred_element_type=F32) for p in _split3(at))
    atot = jnp.sum(at, axis=1, keepdims=True)
    wrow = dtt * jnp.exp(atot - acumt)
    keep = jnp.exp(jnp.broadcast_to(atot, (SSM_HEADS, LANES)))

    first = lax.broadcasted_iota(jnp.int32, (1, LANES), 1) < HD
    ngroups = b_ref.shape[1] // SSM_STATE
    pairs_per_group = SSM_HEADS // 2 // ngroups
    for g in range(ngroups):
        bg = b_ref[:, g * SSM_STATE:(g + 1) * SSM_STATE]
        cg = c_ref[:, g * SSM_STATE:(g + 1) * SSM_STATE]
        cb = _nt(cg, bg)
        cgf = cg.astype(F32)
        bt = bg.astype(F32).T
        for pp in range(pairs_per_group):
            p = g * pairs_per_group + pp
            x = x_ref[:, p * LANES:(p + 1) * LANES]
            st = s_ref[d, p]
            rhs = jnp.concatenate([x, st.astype(BF16)], axis=0)
            ys, us = [], []
            for h in (2 * p, 2 * p + 1):
                col = jnp.broadcast_to(acum[:, h:h + 1], (q, LANES))
                dec = jnp.exp(jnp.where(causal, col - acumt[h:h + 1, :], NEG)) * (cb * dtt[h:h + 1, :])
                lhs = jnp.concatenate([dec, jnp.exp(col) * cgf], axis=1).astype(BF16)
                ys.append(jnp.dot(lhs, rhs, preferred_element_type=F32))
                us.append(jnp.dot((bt * wrow[h:h + 1, :]).astype(BF16), x,
                                  preferred_element_type=F32))
            y_ref[:, p * LANES:(p + 1) * LANES] = jnp.where(first, ys[0], ys[1]).astype(y_ref.dtype)
            keep_p = jnp.where(first, keep[2 * p:2 * p + 1, :], keep[2 * p + 1:2 * p + 2, :])
            s_ref[d, p] = st * keep_p + jnp.where(first, us[0], us[1])


def _ssd_kernel(*refs):
    fwd, bwd, (bias_ref, biast_ref, alog_ref, alogt_ref), (yf_ref, yb_ref, s_ref) = (
        refs[0:5], refs[5:10], refs[10:14], refs[14:17])
    assert CHUNK == LANES

    @pl.when(pl.program_id(1) == 0)
    def _():
        s_ref[...] = jnp.zeros_like(s_ref)

    for d, ins, y_ref in ((0, fwd, yf_ref), (1, bwd, yb_ref)):
        _ssd_direction(d, *ins, bias_ref, biast_ref, alog_ref, alogt_ref, y_ref, s_ref)


def _ssd_scan(xbc, dt_raw, dt_bias, a_log, nb, nctx):
    rows = xbc.shape[0]
    dtd = dt_raw.reshape(rows, 2, SSM_HEADS).transpose(1, 0, 2)
    dtt = dtd.transpose(0, 2, 1)

    def direction_specs(d):
        blk = functools.partial(_chunk_block, d=d, nctx=nctx)
        return [
            pl.BlockSpec((CHUNK, SSM_INNER), lambda b, s: (blk(b, s), 0)),
            pl.BlockSpec((CHUNK, 2 * SSM_STATE), lambda b, s: (blk(b, s), 4)),
            pl.BlockSpec((CHUNK, 2 * SSM_STATE), lambda b, s: (blk(b, s), 5)),
            pl.BlockSpec((1, CHUNK, SSM_HEADS), lambda b, s: (d, blk(b, s), 0)),
            pl.BlockSpec((1, SSM_HEADS, CHUNK), lambda b, s: (d, 0, blk(b, s))),
        ]

    whole = lambda shape: pl.BlockSpec(shape, lambda b, s: (0, 0, 0))
    out_spec = lambda d: pl.BlockSpec(
        (CHUNK, SSM_INNER), lambda b, s: (_chunk_block(b, s, d, nctx), 0))
    y_shape = jax.ShapeDtypeStruct((rows, SSM_INNER), BF16)
    return pl.pallas_call(
        _ssd_kernel,
        out_shape=(y_shape, y_shape),
        grid=(nb, NCC + NLC),
        in_specs=direction_specs(0) + direction_specs(1) + [
            whole((2, 1, SSM_HEADS)), whole((2, SSM_HEADS, 1)),
            whole((2, 1, SSM_HEADS)), whole((2, SSM_HEADS, 1)),
        ],
        out_specs=(out_spec(0), out_spec(1)),
        scratch_shapes=[pltpu.VMEM((2, SSM_HEADS // 2, SSM_STATE, LANES), F32)],
        compiler_params=_cp(("arbitrary", "arbitrary")),
        name="ssd_scan",
    )(*([xbc, xbc, xbc, dtd, dtt] * 2),
      dt_bias.reshape(2, 1, SSM_HEADS), dt_bias.reshape(2, SSM_HEADS, 1),
      a_log.reshape(2, 1, SSM_HEADS), a_log.reshape(2, SSM_HEADS, 1))


def _ret_geometry(d):
    q = CHUNK
    ii = lax.broadcasted_iota(jnp.int32, (q, q), 0)
    jj = lax.broadcasted_iota(jnp.int32, (q, q), 1)
    dist = (ii - jj) if d == 0 else (jj - ii)
    ri = lax.broadcasted_iota(jnp.int32, (q, LANES), 0)
    cj = lax.broadcasted_iota(jnp.int32, (1, q), 1)
    steps_in = ((ri + 1) if d == 0 else (q - ri)).astype(F32)
    steps_out = ((q - 1 - cj) if d == 0 else cj).astype(F32)
    return dist >= 0, dist.astype(F32), steps_in, steps_out


def _ret_kernel(*refs):
    fwd, bwd, ld_ref, (yf_ref, yb_ref, s_ref) = refs[0:3], refs[3:6], refs[6], refs[7:10]
    q = CHUNK
    step = pl.program_id(1)

    @pl.when(step == 0)
    def _():
        s_ref[...] = jnp.zeros_like(s_ref)

    scale = jnp.asarray(HD ** -0.5, BF16)
    dirs = ((0, fwd, yf_ref), (1, bwd, yb_ref))

    def log_gamma(d, h):
        lg = jnp.log(1.0 - jnp.exp(ld_ref[d, h]))
        return jnp.broadcast_to(lg[None], (q // 8, 8, LANES)).reshape(q, LANES)

    qk = {}
    for d, (c_ref, b_ref, x_ref), y_ref in dirs:
        for h in range(RET_HEADS):
            pb, a = h // 2, h % 2
            ch = _head_mask(c_ref[:, pb * LANES:(pb + 1) * LANES], a) * scale
            bh = _head_mask(b_ref[:, pb * LANES:(pb + 1) * LANES], a)
            qk[d, h] = (ch, _nt(ch, bh))

    for d, (c_ref, b_ref, x_ref), y_ref in dirs:
        causal, distf, steps_in, _ = _ret_geometry(d)
        for h in range(RET_HEADS):
            lg = log_gamma(d, h)
            ch, cb = qk[d, h]
            dec = jnp.exp(jnp.where(causal, distf * lg, NEG)) * cb
            grow = jnp.exp(steps_in * lg) * ch.astype(F32)
            lhs = jnp.concatenate([dec, grow], axis=1).astype(BF16)
            x = x_ref[:, h * LANES:(h + 1) * LANES]
            rhs = jnp.concatenate([x, s_ref[d, h].astype(BF16)], axis=0)
            y_ref[:, h * LANES:(h + 1) * LANES] = jnp.dot(
                lhs, rhs, preferred_element_type=F32).astype(y_ref.dtype)

    for d, (c_ref, b_ref, x_ref), y_ref in dirs:
        _, _, _, steps_out = _ret_geometry(d)
        for h in range(RET_HEADS):
            lg = log_gamma(d, h)
            pb, a = h // 2, h % 2
            bt = _head_mask(b_ref[:, pb * LANES:(pb + 1) * LANES], a).astype(F32).T
            bth = (bt * jnp.exp(steps_out * lg[0:1, :])).astype(BF16)
            x = x_ref[:, h * LANES:(h + 1) * LANES]
            s_ref[d, h] = s_ref[d, h] * jnp.exp(q * lg) + jnp.dot(
                bth, x, preferred_element_type=F32)


def _retention_scan(rp, pp, log_decay, nb, nctx):
    w = RET_HEADS * HD
    wv = RET_HEADS * LANES

    def direction_specs(d):
        blk = functools.partial(_chunk_block, d=d, nctx=nctx)
        return [
            pl.BlockSpec((CHUNK, w), lambda b, s: (blk(b, s), 1)),
            pl.BlockSpec((CHUNK, w), lambda b, s: (blk(b, s), 2)),
            pl.BlockSpec((CHUNK, wv), lambda b, s: (blk(b, s), 0)),
        ]

    out_spec = lambda d: pl.BlockSpec((CHUNK, wv), lambda b, s: (_latent_chunk_block(b, s, d), 0))
    y_shape = jax.ShapeDtypeStruct((nb * SEQ, wv), BF16)
    return pl.pallas_call(
        _ret_kernel,
        out_shape=(y_shape, y_shape),
        grid=(nb, NCC + NLC),
        in_specs=direction_specs(0) + direction_specs(1) + [
            pl.BlockSpec((2, RET_HEADS, 8, LANES), lambda b, s: (0, 0, 0, 0))],
        out_specs=(out_spec(0), out_spec(1)),
        scratch_shapes=[pltpu.VMEM((2, RET_HEADS, LANES, LANES), F32)],
        compiler_params=_cp(("arbitrary", "arbitrary")),
        name="retention_scan",
    )(*([rp, rp, pp] * 2),
      jnp.broadcast_to(log_decay.astype(F32)[:, :, None, None], (2, RET_HEADS, 8, LANES)))


def _ssm_out_kernel(yf_ref, yb_ref, xs_ref, z_ref, dsk_ref, g_ref, o_ref):
    y = yf_ref[...].astype(F32) + yb_ref[...].astype(F32) + xs_ref[...].astype(F32) * dsk_ref[...]
    yz = y * _silu(z_ref[...].astype(F32))
    half = SSM_INNER // 2
    for g in range(2):
        seg = yz[:, g * half:(g + 1) * half]
        ms = jnp.mean(seg * seg, axis=-1, keepdims=True)
        o_ref[:, g * half:(g + 1) * half] = (
            seg * lax.rsqrt(ms + EPS) * g_ref[:, g * half:(g + 1) * half]).astype(o_ref.dtype)


def _ssm_out(yf, yb, xbc, big, d_skip, gn_g, tm):
    rows = xbc.shape[0]
    dsk = jnp.repeat(d_skip.astype(F32), HD).reshape(1, SSM_INNER)
    return pl.pallas_call(
        _ssm_out_kernel,
        out_shape=jax.ShapeDtypeStruct((rows, SSM_INNER), BF16),
        grid=(rows // tm,),
        in_specs=[
            pl.BlockSpec((tm, SSM_INNER), lambda i: (i, 0)),
            pl.BlockSpec((tm, SSM_INNER), lambda i: (i, 0)),
            pl.BlockSpec((tm, SSM_INNER), lambda i: (i, 0)),
            pl.BlockSpec((tm, SSM_INNER), lambda i: (i, EV_Z_BLK)),
            pl.BlockSpec((1, SSM_INNER), lambda i: (0, 0)),
            pl.BlockSpec((1, SSM_INNER), lambda i: (0, 0)),
        ],
        out_specs=pl.BlockSpec((tm, SSM_INNER), lambda i: (i, 0)),
        compiler_params=_cp(("parallel",)),
        name="ssm_out",
    )(yf, yb, xbc, big, dsk, gn_g.reshape(1, SSM_INNER))


def _ret_out_kernel(rf_ref, rb_ref, rg_ref, g_ref, b_ref, o_ref):
    y = rf_ref[...].astype(F32) + rb_ref[...].astype(F32)
    for h in range(RET_HEADS):
        sl = slice(h * LANES, (h + 1) * LANES)
        seg = y[:, sl]
        mu = jnp.mean(seg, axis=-1, keepdims=True)
        cen = seg - mu
        var = jnp.mean(cen * cen, axis=-1, keepdims=True)
        yn = cen * lax.rsqrt(var + EPS) * g_ref[:, sl] + b_ref[:, sl]
        o_ref[:, sl] = (_silu(rg_ref[:, sl].astype(F32)) * yn).astype(o_ref.dtype)


def _ret_out(rf, rb, pp, gn_g, gn_b, nb, nctx, tm):
    rows = nb * SEQ
    w = RET_HEADS * LANES
    off = nctx // tm
    return pl.pallas_call(
        _ret_out_kernel,
        out_shape=jax.ShapeDtypeStruct((rows, w), BF16),
        grid=(rows // tm,),
        in_specs=[
            pl.BlockSpec((tm, w), lambda i: (i, 0)),
            pl.BlockSpec((tm, w), lambda i: (i, 0)),
            pl.BlockSpec((tm, w), lambda i: (off + i, 1)),
            pl.BlockSpec((1, w), lambda i: (0, 0)),
            pl.BlockSpec((1, w), lambda i: (0, 0)),
        ],
        out_specs=pl.BlockSpec((tm, w), lambda i: (i, 0)),
        compiler_params=_cp(("parallel",)),
        name="ret_out",
    )(rf, rb, pp, gn_g.reshape(1, w), gn_b.reshape(1, w))


def _outproj_kernel(x_ref, a1_ref, *rest, n_ctx_tiles):
    if n_ctx_tiles:
        a1c_ref, a2_ref, w1_ref, w2_ref, mod_ref, o_ref = rest
        a1 = jnp.where(pl.program_id(0) < n_ctx_tiles, a1c_ref[...], a1_ref[...])
    else:
        a2_ref, w1_ref, w2_ref, mod_ref, o_ref = rest
        a1 = a1_ref[...]
    y = (jnp.dot(a1, w1_ref[...], preferred_element_type=F32)
         + jnp.dot(a2_ref[...], w2_ref[...], preferred_element_type=F32))
    o_ref[...] = x_ref[...] + mod_ref[0][2:3] * y


def _outproj(x, a1, a2, w_out, mod, nctx, tm, x_off, a1_ctx=None):
    rows = a2.shape[0]
    k1, k2 = a1.shape[1], a2.shape[1]
    w1 = w_out[:k1].astype(BF16)
    w2 = w_out[k1:].astype(BF16)
    nct = 0 if a1_ctx is None else nctx // tm
    in_specs = [
        pl.BlockSpec((tm, D), lambda i: (i + x_off, 0)),
        pl.BlockSpec((tm, k1), lambda i: (jnp.maximum(i - nct, 0), 0)),
    ]
    args = [x, a1]
    if a1_ctx is not None:
        in_specs.append(pl.BlockSpec((tm, k1), lambda i: (jnp.minimum(i, nct - 1), 0)))
        args.append(a1_ctx)
    in_specs += [
        pl.BlockSpec((tm, k2), lambda i: (i, 0)),
        pl.BlockSpec((k1, D), lambda i: (0, 0)),
        pl.BlockSpec((k2, D), lambda i: (0, 0)),
        pl.BlockSpec((1, 6, D), lambda i: (_mod_row((i + x_off) * tm, nctx), 0, 0)),
    ]
    args += [a2, w1, w2, mod]
    return pl.pallas_call(
        functools.partial(_outproj_kernel, n_ctx_tiles=nct),
        out_shape=jax.ShapeDtypeStruct((rows, D), F32),
        grid=(rows // tm,),
        in_specs=in_specs,
        out_specs=pl.BlockSpec((tm, D), lambda i: (i, 0)),
        compiler_params=_cp(("parallel",)),
        name="outproj",
    )(*args)


def _top2_gates(logits):
    lane = lax.broadcasted_iota(jnp.int32, logits.shape, 1)
    valid = lane < N_EXPERTS
    l0 = jnp.where(valid, logits, NEG)
    m1 = jnp.max(l0, axis=-1, keepdims=True)
    i1 = jnp.min(jnp.where(l0 == m1, lane, LANES), axis=-1, keepdims=True)
    l1 = jnp.where(lane == i1, NEG, l0)
    m2 = jnp.max(l1, axis=-1, keepdims=True)
    i2 = jnp.min(jnp.where(l1 == m2, lane, LANES), axis=-1, keepdims=True)
    e2 = jnp.exp(m2 - m1)
    g1 = 1.0 / (1.0 + e2)
    g2 = e2 / (1.0 + e2)
    return jnp.where(lane == i1, g1, 0.0) + jnp.where(lane == i2, g2, 0.0)


def _ffn_kernel(x_ref, g_ref, mod_ref, *rest, routed, final):
    rest = list(rest)
    router_ref = rest.pop(0) if routed else None
    w1_ref, w3_ref, w2_ref = rest[:3]
    rest = rest[3:]
    fg_ref = rest.pop(0) if final else None
    o_ref, h_ref, acc_ref = rest[:3]
    gate_ref = rest[3] if routed else None
    e = pl.program_id(1)
    f = pl.program_id(2)
    first = jnp.logical_and(e == 0, f == 0)
    last = jnp.logical_and(e == pl.num_programs(1) - 1, f == pl.num_programs(2) - 1)

    @pl.when(first)
    def _():
        h = _norm_mod(x_ref[...], g_ref[...], mod_ref[0], 3)
        h16 = h.astype(BF16)
        h_ref[...] = h16
        acc_ref[...] = jnp.zeros_like(acc_ref)
        if routed:
            hl = (h - h16.astype(F32)).astype(BF16)
            r = router_ref[...]
            rh = r.astype(BF16)
            rl = (r - rh.astype(F32)).astype(BF16)
            logits = (jnp.dot(h16, rh, preferred_element_type=F32)
                      + jnp.dot(h16, rl, preferred_element_type=F32)
                      + jnp.dot(hl, rh, preferred_element_type=F32))
            gate_ref[...] = _top2_gates(logits)

    h16 = h_ref[...]
    u = _silu(jnp.dot(h16, w1_ref[0], preferred_element_type=F32)) * jnp.dot(
        h16, w3_ref[0], preferred_element_type=F32)
    if routed:
        lane = lax.broadcasted_iota(jnp.int32, gate_ref.shape, 1)
        ge = jnp.sum(jnp.where(lane == e, gate_ref[...], 0.0), axis=-1, keepdims=True)
        u = u * ge
    acc_ref[...] += jnp.dot(u.astype(BF16), w2_ref[0], preferred_element_type=F32)

    @pl.when(last)
    def _():
        y = x_ref[...] + mod_ref[0][5:6] * acc_ref[...]
        if final:
            ms = jnp.mean(y * y, axis=-1, keepdims=True)
            y = y * lax.rsqrt(ms + EPS) * fg_ref[...]
        o_ref[...] = y


def _ffn(x, g, mod, w1, w3, w2, nctx, tm, tf, x_off=0, rows=None, router=None, final_g=None):
    rows = x.shape[0] if rows is None else rows
    ne, _, dff = w1.shape
    routed = router is not None
    final = final_g is not None
    in_specs = [
        pl.BlockSpec((tm, D), lambda i, e, f: (i + x_off, 0)),
        pl.BlockSpec((1, D), lambda i, e, f: (0, 0)),
        pl.BlockSpec((1, 6, D), lambda i, e, f: (_mod_row((i + x_off) * tm, nctx), 0, 0)),
    ]
    args = [x, g.reshape(1, D), mod]
    if routed:
        in_specs.append(pl.BlockSpec((D, LANES), lambda i, e, f: (0, 0)))
        args.append(jnp.pad(router.astype(F32), ((0, 0), (0, LANES - ne))))
    in_specs += [
        pl.BlockSpec((1, D, tf), lambda i, e, f: (e, 0, f)),
        pl.BlockSpec((1, D, tf), lambda i, e, f: (e, 0, f)),
        pl.BlockSpec((1, tf, D), lambda i, e, f: (e, f, 0)),
    ]
    args += [w1, w3, w2]
    if final:
        in_specs.append(pl.BlockSpec((1, D), lambda i, e, f: (0, 0)))
        args.append(final_g.reshape(1, D))
    scratch = [pltpu.VMEM((tm, D), BF16), pltpu.VMEM((tm, D), F32)]
    if routed:
        scratch.append(pltpu.VMEM((tm, LANES), F32))
    return pl.pallas_call(
        functools.partial(_ffn_kernel, routed=routed, final=final),
        out_shape=jax.ShapeDtypeStruct((rows, D), F32),
        grid=(rows // tm, ne, dff // tf),
        in_specs=in_specs,
        out_specs=pl.BlockSpec((tm, D), lambda i, e, f: (i, 0)),
        scratch_shapes=scratch,
        compiler_params=_cp(("parallel", "arbitrary", "arbitrary")),
        name="moe_ffn" if routed else "dense_ffn",
    )(*args)


def _rope_tables(tm):
    t = np.arange(SEQ)
    n_freq = HD // 4
    inv = ROPE_BASE ** (-np.arange(n_freq, dtype=np.float64) / n_freq)
    ang = np.concatenate([(t // GRID_W)[:, None] * inv, (t % GRID_W)[:, None] * inv], axis=-1)
    cos = np.tile(np.cos(ang), (1, 4))
    sin = np.tile(np.concatenate([-np.sin(ang), np.sin(ang)], axis=-1), (1, 2))
    cos = np.concatenate([np.ones((tm, LANES)), cos], axis=0)
    sin = np.concatenate([np.zeros((tm, LANES)), sin], axis=0)
    return jnp.asarray(cos, F32), jnp.asarray(sin, F32)


def _dup_heads(w, nheads):
    w = w.reshape(D, nheads, 1, HD)
    return jnp.broadcast_to(w, (D, nheads, 2, HD)).reshape(D, nheads * 2 * HD)


def kernel(x, c, ctx, c_ctx, ada_w, ada_b, norm_attn_g, norm_ffn_g, ev_w_in, ev_w_out, na_rpb,
           ssm_conv_w, ssm_conv_b, ssm_dt_bias, ssm_a_log, ssm_d, ssm_norm_g, ffn_w1, ffn_w3, ffn_w2,
           od_w_in, od_w_out, swa_sink, ret_log_decay, ret_gn_g, ret_gn_b, moe_router, moe_w1,
           moe_w3, moe_w2, final_g):
    nb = x.shape[0]
    nctx = -(-(nb * CTX) // SEQ) * SEQ
    tm = 1024
    xs = jnp.concatenate([ctx.reshape(nb * CTX, D), jnp.zeros((nctx - nb * CTX, D), ctx.dtype),
                          x.reshape(nb * SEQ, D)], axis=0).astype(F32)

    rp = -(-(nb + 1) // 8) * 8
    cin = jnp.zeros((rp, D), F32).at[0].set(c_ctx).at[1:nb + 1].set(c)
    mod = _modulation(cin, ada_w, ada_b).reshape(2, rp, 6, D)

    w_in = ev_w_in[0]
    q_, k_, v_, z_, xbc_, dt_ = jnp.split(w_in, [512, 1024, 1536, 2560, 4096], axis=1)
    w_big = jnp.concatenate([xbc_, q_, k_, v_, z_], axis=1).astype(BF16)
    big = _proj(xs, norm_attn_g[0], mod[0], w_big, BF16, nctx, tm, 2048, name="even_in_proj")
    dt_raw = _proj(xs, norm_attn_g[0], mod[0], dt_.astype(BF16), F32, nctx, tm, 2 * SSM_HEADS,
                   name="even_dt_proj")
    attn = _na_attention(big, _na_bias_blocks(na_rpb[0]), nb, nctx)
    attn_ctx = _ctx_attention(big, nctx)
    xbc = _conv_silu(big, ssm_conv_w[0], ssm_conv_b[0], nctx)
    yf, yb = _ssd_scan(xbc, dt_raw, ssm_dt_bias[0], ssm_a_log[0], nb, nctx)
    ys = _ssm_out(yf, yb, xbc, big, ssm_d[0], ssm_norm_g[0], tm)
    xs = _outproj(xs, attn, ys, ev_w_out[0], mod[0], nctx, tm, 0, a1_ctx=attn_ctx)
    xs = _ffn(xs, norm_ffn_g[0], mod[0], ffn_w1.astype(BF16), ffn_w3.astype(BF16),
              ffn_w2.astype(BF16), nctx, 2 * tm, 256)

    w_in = od_w_in[0]
    q_, k_, v_, rq_, rk_, rv_, rg_ = jnp.split(w_in, [512, 640, 768, 1280, 1792, 2816], axis=1)
    w_rope = jnp.concatenate([q_, rq_, rk_, _dup_heads(k_, 2)], axis=1).astype(BF16)
    w_plain = jnp.concatenate([rv_, rg_, _dup_heads(v_, 2)], axis=1).astype(BF16)
    rope_tabs = _rope_tables(tm)
    rpj = _proj(xs, norm_attn_g[1], mod[1], w_rope, BF16, nctx, tm, w_rope.shape[1],
                rope=rope_tabs, name="odd_rope_proj")
    ppj = _proj(xs, norm_attn_g[1], mod[1], w_plain, BF16, nctx, tm, w_plain.shape[1],
                name="odd_plain_proj")
    yw = _swa_attention(rpj, ppj, swa_sink[0], nb, nctx)
    rf, rb = _retention_scan(rpj, ppj, ret_log_decay[0], nb, nctx)
    yr = _ret_out(rf, rb, ppj, ret_gn_g[0], ret_gn_b[0], nb, nctx, tm)
    x_off = nctx // tm
    xl = _outproj(xs, yw, yr, od_w_out[0], mod[1], nctx, tm, x_off)
    out = _ffn(xl, norm_ffn_g[1], mod[1], moe_w1[0].astype(BF16), moe_w3[0].astype(BF16),
               moe_w2[0].astype(BF16), 0, tm, 512, router=moe_router[0], final_g=final_g)
    return out.reshape(nb, SEQ, D).astype(x.dtype)
```

```python
import functools

import numpy as np
import jax
import jax.numpy as jnp
from jax import lax
from jax.experimental import pallas as pl
from jax.experimental.pallas import tpu as pltpu

F32 = jnp.float32
BF16 = jnp.bfloat16

D = 1024
SEQ = 2048
CTX = 256
GRID_W = 64
HD = 64
EPS = 1e-6
ROPE_BASE = 10000.0
NEG = -1e30

NA_HEADS = 8
NA_ROWS = 8
NA_COLS = 16
NA_QROWS = 4
NA_KROWS = 12
SSM_HEADS = 16
SSM_INNER = 1024
SSM_STATE = 128
SSM_CONV = 5
SSM_CONV_CH = 1536
CHUNK = 128
SWA_HEADS = 8
SWA_WINDOW = 128
SWA_QT = 512
SWA_KT = SWA_QT + 2 * SWA_WINDOW
RET_HEADS = 8
D_FF = 2816
N_EXPERTS = 8
D_FF_EXPERT = 3584

EV_Q_BLK, EV_K_BLK, EV_V_BLK = 12, 16, 20
EV_Z_BLK = 3

LANES = 128
VMEM_LIMIT = 56 * 1024 * 1024


def _cp(sem, vmem=VMEM_LIMIT):
    return pltpu.CompilerParams(dimension_semantics=sem, vmem_limit_bytes=vmem)


def _sigmoid(x):
    return 1.0 / (1.0 + jnp.exp(-x))


def _silu(x):
    return x * _sigmoid(x)


def _mod_row(start, nctx):
    return jnp.where(start < nctx, 0, 1 + (start - nctx) // SEQ)


def _mod_kernel(c_ref, w_ref, b_ref, o_ref):
    c = c_ref[...]
    h = _silu(c).astype(BF16)
    o_ref[0] = jnp.dot(h, w_ref[0].astype(BF16), preferred_element_type=F32) + b_ref[0]


def _modulation(cin, ada_w, ada_b):
    depth, _, n6 = ada_w.shape
    rp = cin.shape[0]
    tn = 1024
    return pl.pallas_call(
        _mod_kernel,
        out_shape=jax.ShapeDtypeStruct((depth, rp, n6), F32),
        grid=(depth, n6 // tn),
        in_specs=[
            pl.BlockSpec((rp, D), lambda l, j: (0, 0)),
            pl.BlockSpec((1, D, tn), lambda l, j: (l, 0, j)),
            pl.BlockSpec((1, 1, tn), lambda l, j: (l, 0, j)),
        ],
        out_specs=pl.BlockSpec((1, rp, tn), lambda l, j: (l, 0, j)),
        compiler_params=_cp(("arbitrary", "arbitrary")),
        name="adaln_mod",
    )(cin, ada_w, ada_b.reshape(depth, 1, n6))


def _norm_mod(x, g, m, k):
    ms = jnp.mean(x * x, axis=-1, keepdims=True)
    y = x * lax.rsqrt(ms + EPS) * g
    return y * (1.0 + m[k + 1:k + 2]) + m[k:k + 1]


def _swap32(r):
    lane = lax.broadcasted_iota(jnp.int32, r.shape, 1)
    return jnp.where((lane % 64) < 32, pltpu.roll(r, 96, 1), pltpu.roll(r, 32, 1))


def _proj_kernel(x_ref, g_ref, mod_ref, w_ref, *rest, rope):
    if rope:
        c_ref, s_ref, o_ref, h_ref = rest
    else:
        o_ref, h_ref = rest

    @pl.when(pl.program_id(1) == 0)
    def _():
        h_ref[...] = _norm_mod(x_ref[...], g_ref[...], mod_ref[0], 0).astype(BF16)

    r = jnp.dot(h_ref[...], w_ref[...], preferred_element_type=F32)
    if rope:
        c = c_ref[...]
        s = s_ref[...]
        for k in range(r.shape[1] // LANES):
            rk = r[:, k * LANES:(k + 1) * LANES]
            o_ref[:, k * LANES:(k + 1) * LANES] = (rk * c + _swap32(rk) * s).astype(o_ref.dtype)
    else:
        o_ref[...] = r.astype(o_ref.dtype)


def _proj(x, g, mod, w, out_dtype, nctx, tm, tn, rope=None, name="proj"):
    rows = x.shape[0]
    n = w.shape[1]
    in_specs = [
        pl.BlockSpec((tm, D), lambda i, j: (i, 0)),
        pl.BlockSpec((1, D), lambda i, j: (0, 0)),
        pl.BlockSpec((1, 6, D), lambda i, j: (_mod_row(i * tm, nctx), 0, 0)),
        pl.BlockSpec((D, tn), lambda i, j: (0, j)),
    ]
    args = [x, g.reshape(1, D), mod, w]
    if rope is not None:
        nct = nctx // tm
        per = SEQ // tm

        def tab_map(i, j):
            return (jnp.where(i < nct, 0, 1 + (i - nct) % per), 0)
        in_specs += [pl.BlockSpec((tm, LANES), tab_map), pl.BlockSpec((tm, LANES), tab_map)]
        args += list(rope)
    return pl.pallas_call(
        functools.partial(_proj_kernel, rope=rope is not None),
        out_shape=jax.ShapeDtypeStruct((rows, n), out_dtype),
        grid=(rows // tm, n // tn),
        in_specs=in_specs,
        out_specs=pl.BlockSpec((tm, tn), lambda i, j: (i, j)),
        scratch_shapes=[pltpu.VMEM((tm, D), BF16)],
        compiler_params=_cp(("parallel", "arbitrary")),
        name=name,
    )(*args)


def _head_mask(x, a):
    lane = lax.broadcasted_iota(jnp.int32, (1, LANES), 1)
    keep = (lane < HD) if a == 0 else (lane >= HD)
    return jnp.where(keep, x, jnp.zeros_like(x))


def _nt(a, b):
    return lax.dot_general(a, b, (((1,), (1,)), ((), ())), preferred_element_type=F32)


def _two_part_attention(q, kc, vc, kl, vl, bias_fn, sink_fn):
    lane = lax.broadcasted_iota(jnp.int32, (1, LANES), 1)
    outs = []
    for a in range(2):
        qa = _head_mask(q, a) * jnp.asarray(HD ** -0.5, q.dtype)
        s_c = _nt(qa, kc)
        s_l = bias_fn(a, _nt(qa, kl))
        m = jnp.maximum(jnp.max(s_c, axis=-1, keepdims=True), jnp.max(s_l, axis=-1, keepdims=True))
        sink = sink_fn(a)
        if sink is not None:
            m = jnp.maximum(m, sink)
        p_c = jnp.exp(s_c - m)
        p_l = jnp.exp(s_l - m)
        den = jnp.sum(p_c, axis=-1, keepdims=True) + jnp.sum(p_l, axis=-1, keepdims=True)
        if sink is not None:
            den = den + jnp.exp(sink - m)
        o = (jnp.dot(p_c.astype(BF16), vc, preferred_element_type=F32)
             + jnp.dot(p_l.astype(BF16), vl, preferred_element_type=F32))
        outs.append(o / den)
    return jnp.where(lane < HD, outs[0], outs[1])


def _na_window_start(t):
    return jnp.clip(NA_QROWS * t - NA_ROWS // 2, 0, SEQ // GRID_W - NA_KROWS)


NA_NDROW = 2 * NA_ROWS - 1


def _na_bias_tile(toe_ref, a, t):
    lane = lax.broadcasted_iota(jnp.int32, (1, LANES), 1)
    nrows = SEQ // GRID_W
    w0r = _na_window_start(t)
    tile_rows = []
    for rq in range(NA_QROWS):
        r = NA_QROWS * t + rq
        r0 = jnp.clip(r - NA_ROWS // 2, 0, nrows - NA_ROWS)
        blocks = []
        for j in range(NA_KROWS // 2):
            kr = w0r + 2 * j
            e = jnp.clip(kr - r + NA_ROWS, 0, NA_NDROW)
            pen0 = jnp.where(jnp.logical_and(kr >= r0, kr < r0 + NA_ROWS), 0.0, NEG)
            pen1 = jnp.where(jnp.logical_and(kr + 1 >= r0, kr + 1 < r0 + NA_ROWS), 0.0, NEG)
            blocks.append(toe_ref[a, e] + jnp.where(lane < GRID_W, pen0, pen1))
        tile_rows.append(jnp.concatenate(blocks, axis=1))
    return jnp.concatenate(tile_rows, axis=0)


def _na_kernel(q_ref, kl_ref, vl_ref, kc_ref, vc_ref, toe_ref, o_ref):
    t = pl.program_id(1)
    w0 = pl.multiple_of(_na_window_start(t) * GRID_W, GRID_W)
    nk = NA_KROWS * GRID_W
    kl = kl_ref[pl.ds(w0, nk), :]
    vl = vl_ref[pl.ds(w0, nk), :]
    o = _two_part_attention(q_ref[...], kc_ref[...], vc_ref[...], kl, vl,
                            lambda a, s: s + _na_bias_tile(toe_ref, a, t), lambda a: None)
    o_ref[...] = o.astype(o_ref.dtype)


def _na_bias_kernel(rpb_ref, o_ref):
    c = lax.broadcasted_iota(jnp.int32, (GRID_W, LANES), 0)
    lane = lax.broadcasted_iota(jnp.int32, (GRID_W, LANES), 1)
    kc = lane % GRID_W
    c0 = jnp.clip(c - NA_COLS // 2, 0, GRID_W - NA_COLS)
    col_ok = jnp.logical_and(kc >= c0, kc < c0 + NA_COLS)
    neg = jnp.full((GRID_W, LANES), NEG, F32)

    def toeplitz(d, shift):
        v = jnp.broadcast_to(rpb_ref[0, d:d + 1, :], (GRID_W, LANES))
        return pltpu.roll(v, shift % LANES, 1, stride=1, stride_axis=0)

    for e in range(NA_NDROW + 1):
        lo = toeplitz(e - 1, -(NA_COLS - 1)) if e >= 1 else neg
        hi = toeplitz(e, GRID_W - (NA_COLS - 1)) if e < NA_NDROW else neg
        o_ref[0, e] = jnp.where(col_ok, jnp.where(lane < GRID_W, lo, hi), NEG)


def _na_bias_blocks(rpb):
    nd, ncol = rpb.shape[1], rpb.shape[2]
    padded = jnp.pad(rpb.astype(F32), ((0, 0), (0, NA_NDROW + 1 - nd), (0, LANES - ncol)))
    return pl.pallas_call(
        _na_bias_kernel,
        out_shape=jax.ShapeDtypeStruct((NA_HEADS, NA_NDROW + 1, GRID_W, LANES), F32),
        grid=(NA_HEADS,),
        in_specs=[pl.BlockSpec((1, NA_NDROW + 1, LANES), lambda h: (h, 0, 0))],
        out_specs=pl.BlockSpec((1, NA_NDROW + 1, GRID_W, LANES), lambda h: (h, 0, 0, 0)),
        compiler_params=_cp(("arbitrary",)),
        name="na_bias_blocks",
    )(padded)


def _na_attention(big, bias, nb, nctx):
    tq = NA_QROWS * GRID_W
    ntile = SEQ // tq
    nct_q = nctx // tq
    lat0 = nctx // SEQ
    return pl.pallas_call(
        _na_kernel,
        out_shape=jax.ShapeDtypeStruct((nb * SEQ, NA_HEADS * HD), BF16),
        grid=(NA_HEADS // 2, ntile, nb),
        in_specs=[
            pl.BlockSpec((tq, LANES), lambda h, t, b: (nct_q + b * ntile + t, EV_Q_BLK + h)),
            pl.BlockSpec((SEQ, LANES), lambda h, t, b: (lat0 + b, EV_K_BLK + h)),
            pl.BlockSpec((SEQ, LANES), lambda h, t, b: (lat0 + b, EV_V_BLK + h)),
            pl.BlockSpec((CTX, LANES), lambda h, t, b: (b, EV_K_BLK + h)),
            pl.BlockSpec((CTX, LANES), lambda h, t, b: (b, EV_V_BLK + h)),
            pl.BlockSpec((2, NA_NDROW + 1, GRID_W, LANES), lambda h, t, b: (h, 0, 0, 0)),
        ],
        out_specs=pl.BlockSpec((tq, LANES), lambda h, t, b: (b * ntile + t, h)),
        compiler_params=_cp(("arbitrary", "arbitrary", "arbitrary")),
        name="na_attention",
    )(big, big, big, big, big, bias)


def _ctx_attn_kernel(q_ref, k_ref, v_ref, o_ref):
    lane = lax.broadcasted_iota(jnp.int32, (1, LANES), 1)
    q = q_ref[...]
    k = k_ref[...]
    v = v_ref[...]
    outs = []
    for a in range(2):
        qa = _head_mask(q, a) * jnp.asarray(HD ** -0.5, q.dtype)
        s = _nt(qa, k)
        m = jnp.max(s, axis=-1, keepdims=True)
        p = jnp.exp(s - m)
        den = jnp.sum(p, axis=-1, keepdims=True)
        outs.append(jnp.dot(p.astype(BF16), v, preferred_element_type=F32) / den)
    o_ref[...] = jnp.where(lane < HD, outs[0], outs[1]).astype(o_ref.dtype)


def _ctx_attention(big, nctx):
    return pl.pallas_call(
        _ctx_attn_kernel,
        out_shape=jax.ShapeDtypeStruct((nctx, NA_HEADS * HD), BF16),
        grid=(nctx // CTX, NA_HEADS // 2),
        in_specs=[
            pl.BlockSpec((CTX, LANES), lambda b, h: (b, EV_Q_BLK + h)),
            pl.BlockSpec((CTX, LANES), lambda b, h: (b, EV_K_BLK + h)),
            pl.BlockSpec((CTX, LANES), lambda b, h: (b, EV_V_BLK + h)),
        ],
        out_specs=pl.BlockSpec((CTX, LANES), lambda b, h: (b, h)),
        compiler_params=_cp(("arbitrary", "arbitrary")),
        name="ctx_attention",
    )(big, big, big)


def _swa_kernel(q_ref, kl_ref, vl_ref, kc_ref, vc_ref, sink_ref, o_ref):
    h2 = pl.program_id(1)
    n = pl.program_id(2)
    start = pl.multiple_of(jnp.clip(SWA_QT * n - SWA_WINDOW, 0, SEQ - SWA_KT), SWA_WINDOW)
    kl = kl_ref[pl.ds(start, SWA_KT), :]
    vl = vl_ref[pl.ds(start, SWA_KT), :]
    qpos = SWA_QT * n + lax.broadcasted_iota(jnp.int32, (SWA_QT, 1), 0)
    kpos = start + lax.broadcasted_iota(jnp.int32, (1, SWA_KT), 1)
    ok = jnp.abs(kpos - qpos) <= SWA_WINDOW

    def sink(a):
        return sink_ref[pl.ds(2 * h2 + a, 1), :][:, :1]

    o = _two_part_attention(q_ref[...], kc_ref[...], vc_ref[...], kl, vl,
                            lambda a, s: jnp.where(ok, s, NEG), sink)
    o_ref[...] = o.astype(o_ref.dtype)


def _swa_attention(rp, pp, sink, nb, nctx):
    nq = SEQ // SWA_QT
    nct_q = nctx // SWA_QT
    lat0 = nctx // SEQ
    sinkv = jnp.broadcast_to(sink.astype(F32)[:, None], (SWA_HEADS, LANES))
    return pl.pallas_call(
        _swa_kernel,
        out_shape=jax.ShapeDtypeStruct((nb * SEQ, SWA_HEADS * HD), BF16),
        grid=(nb, SWA_HEADS // 2, nq),
        in_specs=[
            pl.BlockSpec((SWA_QT, LANES), lambda b, h, n: (nct_q + b * nq + n, h)),
            pl.BlockSpec((SEQ, LANES), lambda b, h, n: (lat0 + b, 12 + h // 2)),
            pl.BlockSpec((SEQ, LANES), lambda b, h, n: (lat0 + b, 16 + h // 2)),
            pl.BlockSpec((CTX, LANES), lambda b, h, n: (b, 12 + h // 2)),
            pl.BlockSpec((CTX, LANES), lambda b, h, n: (b, 16 + h // 2)),
            pl.BlockSpec((SWA_HEADS, LANES), lambda b, h, n: (0, 0)),
        ],
        out_specs=pl.BlockSpec((SWA_QT, LANES), lambda b, h, n: (b * nq + n, h)),
        compiler_params=_cp(("arbitrary", "arbitrary", "arbitrary")),
        name="swa_attention",
    )(rp, rp, pp, rp, pp, sinkv)


CONV_PAD = (SSM_CONV - 1) // 2
CONV_TAPS = tuple(s for s in range(-CONV_PAD, CONV_PAD + 1) if s != 0)
CONV_HALO = 16
CONV_EDGE = 8


def _conv_shift_tables(tb):
    nt = len(CONV_TAPS)
    shift = np.zeros((nt * tb, tb), np.float32)
    edge = np.zeros((nt * 2 * CONV_EDGE, 2 * CONV_HALO), np.float32)
    for t, s in enumerate(CONV_TAPS):
        for i in range(tb):
            j = i + s
            if 0 <= j < tb:
                shift[t * tb + i, j] = 1.0
            elif j < 0:
                edge[t * 2 * CONV_EDGE + i, CONV_HALO + j] = 1.0
            else:
                edge[t * 2 * CONV_EDGE + CONV_EDGE + i - (tb - CONV_EDGE), CONV_HALO + j - tb] = 1.0
    return jnp.asarray(shift, BF16), jnp.asarray(edge, BF16)


def _conv_kernel(xp_ref, x_ref, xn_ref, sh_ref, ed_ref, w_ref, b_ref, o_ref, *, nctx_blocks,
                 per_seq):
    i = pl.program_id(0)
    tb = x_ref.shape[0]
    k = i - nctx_blocks
    has_prev = jnp.logical_and(i >= nctx_blocks, k % per_seq != 0)
    has_next = jnp.logical_and(i >= nctx_blocks, k % per_seq != per_seq - 1)
    x = x_ref[...]
    zero = jnp.zeros((CONV_HALO, x.shape[1]), x.dtype)
    halo = jnp.concatenate([jnp.where(has_prev, xp_ref[...], zero),
                            jnp.where(has_next, xn_ref[...], zero)], axis=0)
    taps = jnp.dot(sh_ref[...], x, preferred_element_type=F32)
    edges = jnp.dot(ed_ref[...], halo, preferred_element_type=F32)
    acc = x.astype(F32) * w_ref[CONV_PAD:CONV_PAD + 1, :] + b_ref[...]
    for t, s in enumerate(CONV_TAPS):
        tap = taps[t * tb:(t + 1) * tb]
        e0 = 2 * CONV_EDGE * t
        tap = jnp.concatenate([tap[:CONV_EDGE] + edges[e0:e0 + CONV_EDGE],
                               tap[CONV_EDGE:tb - CONV_EDGE],
                               tap[tb - CONV_EDGE:] + edges[e0 + CONV_EDGE:e0 + 2 * CONV_EDGE]], axis=0)
        acc = acc + tap * w_ref[CONV_PAD + s:CONV_PAD + s + 1, :]
    o_ref[...] = _silu(acc).astype(o_ref.dtype)


def _conv_silu(big, conv_w, conv_b, nctx):
    rows = big.shape[0]
    tb = CTX
    tc = SSM_CONV_CH
    nblk = rows // tb
    hpb = tb // CONV_HALO
    shift, edge = _conv_shift_tables(tb)
    return pl.pallas_call(
        functools.partial(_conv_kernel, nctx_blocks=nctx // tb, per_seq=SEQ // tb),
        out_shape=jax.ShapeDtypeStruct((rows, SSM_CONV_CH), BF16),
        grid=(nblk, SSM_CONV_CH // tc),
        in_specs=[
            pl.BlockSpec((CONV_HALO, tc), lambda i, c: (jnp.maximum(i * hpb - 1, 0), c)),
            pl.BlockSpec((tb, tc), lambda i, c: (i, c)),
            pl.BlockSpec((CONV_HALO, tc),
                         lambda i, c: (jnp.minimum((i + 1) * hpb, nblk * hpb - 1), c)),
            pl.BlockSpec(shift.shape, lambda i, c: (0, 0)),
            pl.BlockSpec(edge.shape, lambda i, c: (0, 0)),
            pl.BlockSpec((SSM_CONV, tc), lambda i, c: (0, c)),
            pl.BlockSpec((1, tc), lambda i, c: (0, c)),
        ],
        out_specs=pl.BlockSpec((tb, tc), lambda i, c: (i, c)),
        compiler_params=_cp(("arbitrary", "arbitrary")),
        name="conv_silu",
    )(big, big, big, shift, edge, conv_w.reshape(SSM_CONV, SSM_CONV_CH),
      conv_b.reshape(1, SSM_CONV_CH))


NCC = CTX // CHUNK
NLC = SEQ // CHUNK


def _chunk_block(b, s, d, nctx):
    cc = s if d == 0 else NCC - 1 - s
    lc = s - NCC if d == 0 else NLC + NCC - 1 - s
    return jnp.where(s < NCC, NCC * b + cc, nctx // CHUNK + NLC * b + lc)


def _latent_chunk_block(b, s, d):
    lc = s - NCC if d == 0 else NLC + NCC - 1 - s
    return NLC * b + jnp.clip(lc, 0, NLC - 1)


def _split3(a):
    hi = a.astype(BF16)
    r1 = a - hi.astype(F32)
    mid = r1.astype(BF16)
    lo = (r1 - mid.astype(F32)).astype(BF16)
    return hi, mid, lo


def _softplus(x):
    return jnp.maximum(x, 0.0) + jnp.log(1.0 + jnp.exp(-jnp.abs(x)))


def _ssd_direction(d, x_ref, b_ref, c_ref, dt_ref, dtt_ref, bias_ref, biast_ref, alog_ref,
                   alogt_ref, y_ref, s_ref):
    q = CHUNK
    ii = lax.broadcasted_iota(jnp.int32, (q, q), 0)
    jj = lax.broadcasted_iota(jnp.int32, (q, q), 1)
    causal = (jj <= ii) if d == 0 else (jj >= ii)
    tri = jnp.where(causal, 1.0, 0.0).astype(BF16)
    trit = jnp.where((ii <= jj) if d == 0 else (ii >= jj), 1.0, 0.0).astype(BF16)

    dt = _softplus(dt_ref[0] + bias_ref[d])
    a = dt * (-jnp.exp(alog_ref[d]))
    dtt = _softplus(dtt_ref[0] + biast_ref[d])
    at = dtt * (-jnp.exp(alogt_ref[d]))
    acum = sum(jnp.dot(tri, p, preferred_element_type=F32) for p in _split3(a))
    acumt = sum(jnp.dot(p, trit, preferred_element_type=F32) for p in _split3(at))
    atot = jnp.sum(at, axis=1, keepdims=True)
    wrow = dtt * jnp.exp(atot - acumt)
    keep = jnp.exp(jnp.broadcast_to(atot, (SSM_HEADS, LANES)))

    first = lax.broadcasted_iota(jnp.int32, (1, LANES), 1) < HD
    ngroups = b_ref.shape[1] // SSM_STATE
    pairs_per_group = SSM_HEADS // 2 // ngroups
    for g in range(ngroups):
        bg = b_ref[:, g * SSM_STATE:(g + 1) * SSM_STATE]
        cg = c_ref[:, g * SSM_STATE:(g + 1) * SSM_STATE]
        cb = _nt(cg, bg)
        cgf = cg.astype(F32)
        bt = bg.astype(F32).T
        for pp in range(pairs_per_group):
            p = g * pairs_per_group + pp
            x = x_ref[:, p * LANES:(p + 1) * LANES]
            st = s_ref[d, p]
            rhs = jnp.concatenate([x, st.astype(BF16)], axis=0)
            ys, us = [], []
            for h in (2 * p, 2 * p + 1):
                col = jnp.broadcast_to(acum[:, h:h + 1], (q, LANES))
                dec = jnp.exp(jnp.where(causal, col - acumt[h:h + 1, :], NEG)) * (cb * dtt[h:h + 1, :])
                lhs = jnp.concatenate([dec, jnp.exp(col) * cgf], axis=1).astype(BF16)
                ys.append(jnp.dot(lhs, rhs, preferred_element_type=F32))
                us.append(jnp.dot((bt * wrow[h:h + 1, :]).astype(BF16), x,
                                  preferred_element_type=F32))
            y_ref[:, p * LANES:(p + 1) * LANES] = jnp.where(first, ys[0], ys[1]).astype(y_ref.dtype)
            keep_p = jnp.where(first, keep[2 * p:2 * p + 1, :], keep[2 * p + 1:2 * p + 2, :])
            s_ref[d, p] = st * keep_p + jnp.where(first, us[0], us[1])


def _ssd_kernel(*refs):
    fwd, bwd, (bias_ref, biast_ref, alog_ref, alogt_ref), (yf_ref, yb_ref, s_ref) = (
        refs[0:5], refs[5:10], refs[10:14], refs[14:17])
    assert CHUNK == LANES

    @pl.when(pl.program_id(1) == 0)
    def _():
        s_ref[...] = jnp.zeros_like(s_ref)

    for d, ins, y_ref in ((0, fwd, yf_ref), (1, bwd, yb_ref)):
        _ssd_direction(d, *ins, bias_ref, biast_ref, alog_ref, alogt_ref, y_ref, s_ref)


def _ssd_scan(xbc, dt_raw, dt_bias, a_log, nb, nctx):
    rows = xbc.shape[0]
    dtd = dt_raw.reshape(rows, 2, SSM_HEADS).transpose(1, 0, 2)
    dtt = dtd.transpose(0, 2, 1)

    def direction_specs(d):
        blk = functools.partial(_chunk_block, d=d, nctx=nctx)
        return [
            pl.BlockSpec((CHUNK, SSM_INNER), lambda b, s: (blk(b, s), 0)),
            pl.BlockSpec((CHUNK, 2 * SSM_STATE), lambda b, s: (blk(b, s), 4)),
            pl.BlockSpec((CHUNK, 2 * SSM_STATE), lambda b, s: (blk(b, s), 5)),
            pl.BlockSpec((1, CHUNK, SSM_HEADS), lambda b, s: (d, blk(b, s), 0)),
            pl.BlockSpec((1, SSM_HEADS, CHUNK), lambda b, s: (d, 0, blk(b, s))),
        ]

    whole = lambda shape: pl.BlockSpec(shape, lambda b, s: (0, 0, 0))
    out_spec = lambda d: pl.BlockSpec(
        (CHUNK, SSM_INNER), lambda b, s: (_chunk_block(b, s, d, nctx), 0))
    y_shape = jax.ShapeDtypeStruct((rows, SSM_INNER), BF16)
    return pl.pallas_call(
        _ssd_kernel,
        out_shape=(y_shape, y_shape),
        grid=(nb, NCC + NLC),
        in_specs=direction_specs(0) + direction_specs(1) + [
            whole((2, 1, SSM_HEADS)), whole((2, SSM_HEADS, 1)),
            whole((2, 1, SSM_HEADS)), whole((2, SSM_HEADS, 1)),
        ],
        out_specs=(out_spec(0), out_spec(1)),
        scratch_shapes=[pltpu.VMEM((2, SSM_HEADS // 2, SSM_STATE, LANES), F32)],
        compiler_params=_cp(("arbitrary", "arbitrary")),
        name="ssd_scan",
    )(*([xbc, xbc, xbc, dtd, dtt] * 2),
      dt_bias.reshape(2, 1, SSM_HEADS), dt_bias.reshape(2, SSM_HEADS, 1),
      a_log.reshape(2, 1, SSM_HEADS), a_log.reshape(2, SSM_HEADS, 1))


def _ret_geometry(d):
    q = CHUNK
    ii = lax.broadcasted_iota(jnp.int32, (q, q), 0)
    jj = lax.broadcasted_iota(jnp.int32, (q, q), 1)
    dist = (ii - jj) if d == 0 else (jj - ii)
    ri = lax.broadcasted_iota(jnp.int32, (q, LANES), 0)
    cj = lax.broadcasted_iota(jnp.int32, (1, q), 1)
    steps_in = ((ri + 1) if d == 0 else (q - ri)).astype(F32)
    steps_out = ((q - 1 - cj) if d == 0 else cj).astype(F32)
    return dist >= 0, dist.astype(F32), steps_in, steps_out


def _ret_kernel(*refs):
    fwd, bwd, ld_ref, (yf_ref, yb_ref, s_ref) = refs[0:3], refs[3:6], refs[6], refs[7:10]
    q = CHUNK
    step = pl.program_id(1)

    @pl.when(step == 0)
    def _():
        s_ref[...] = jnp.zeros_like(s_ref)

    scale = jnp.asarray(HD ** -0.5, BF16)
    dirs = ((0, fwd, yf_ref), (1, bwd, yb_ref))

    def log_gamma(d, h):
        lg = jnp.log(1.0 - jnp.exp(ld_ref[d, h]))
        return jnp.broadcast_to(lg[None], (q // 8, 8, LANES)).reshape(q, LANES)

    qk = {}
    for d, (c_ref, b_ref, x_ref), y_ref in dirs:
        for h in range(RET_HEADS):
            pb, a = h // 2, h % 2
            ch = _head_mask(c_ref[:, pb * LANES:(pb + 1) * LANES], a) * scale
            bh = _head_mask(b_ref[:, pb * LANES:(pb + 1) * LANES], a)
            qk[d, h] = (ch, _nt(ch, bh))

    for d, (c_ref, b_ref, x_ref), y_ref in dirs:
        causal, distf, steps_in, _ = _ret_geometry(d)
        for h in range(RET_HEADS):
            lg = log_gamma(d, h)
            ch, cb = qk[d, h]
            dec = jnp.exp(jnp.where(causal, distf * lg, NEG)) * cb
            grow = jnp.exp(steps_in * lg) * ch.astype(F32)
            lhs = jnp.concatenate([dec, grow], axis=1).astype(BF16)
            x = x_ref[:, h * LANES:(h + 1) * LANES]
            rhs = jnp.concatenate([x, s_ref[d, h].astype(BF16)], axis=0)
            y_ref[:, h * LANES:(h + 1) * LANES] = jnp.dot(
                lhs, rhs, preferred_element_type=F32).astype(y_ref.dtype)

    for d, (c_ref, b_ref, x_ref), y_ref in dirs:
        _, _, _, steps_out = _ret_geometry(d)
        for h in range(RET_HEADS):
            lg = log_gamma(d, h)
            pb, a = h // 2, h % 2
            bt = _head_mask(b_ref[:, pb * LANES:(pb + 1) * LANES], a).astype(F32).T
            bth = (bt * jnp.exp(steps_out * lg[0:1, :])).astype(BF16)
            x = x_ref[:, h * LANES:(h + 1) * LANES]
            s_ref[d, h] = s_ref[d, h] * jnp.exp(q * lg) + jnp.dot(
                bth, x, preferred_element_type=F32)


def _retention_scan(rp, pp, log_decay, nb, nctx):
    w = RET_HEADS * HD
    wv = RET_HEADS * LANES

    def direction_specs(d):
        blk = functools.partial(_chunk_block, d=d, nctx=nctx)
        return [
            pl.BlockSpec((CHUNK, w), lambda b, s: (blk(b, s), 1)),
            pl.BlockSpec((CHUNK, w), lambda b, s: (blk(b, s), 2)),
            pl.BlockSpec((CHUNK, wv), lambda b, s: (blk(b, s), 0)),
        ]

    out_spec = lambda d: pl.BlockSpec((CHUNK, wv), lambda b, s: (_latent_chunk_block(b, s, d), 0))
    y_shape = jax.ShapeDtypeStruct((nb * SEQ, wv), BF16)
    return pl.pallas_call(
        _ret_kernel,
        out_shape=(y_shape, y_shape),
        grid=(nb, NCC + NLC),
        in_specs=direction_specs(0) + direction_specs(1) + [
            pl.BlockSpec((2, RET_HEADS, 8, LANES), lambda b, s: (0, 0, 0, 0))],
        out_specs=(out_spec(0), out_spec(1)),
        scratch_shapes=[pltpu.VMEM((2, RET_HEADS, LANES, LANES), F32)],
        compiler_params=_cp(("arbitrary", "arbitrary")),
        name="retention_scan",
    )(*([rp, rp, pp] * 2),
      jnp.broadcast_to(log_decay.astype(F32)[:, :, None, None], (2, RET_HEADS, 8, LANES)))


def _ssm_out_kernel(yf_ref, yb_ref, xs_ref, z_ref, dsk_ref, g_ref, o_ref):
    y = yf_ref[...].astype(F32) + yb_ref[...].astype(F32) + xs_ref[...].astype(F32) * dsk_ref[...]
    yz = y * _silu(z_ref[...].astype(F32))
    half = SSM_INNER // 2
    for g in range(2):
        seg = yz[:, g * half:(g + 1) * half]
        ms = jnp.mean(seg * seg, axis=-1, keepdims=True)
        o_ref[:, g * half:(g + 1) * half] = (
            seg * lax.rsqrt(ms + EPS) * g_ref[:, g * half:(g + 1) * half]).astype(o_ref.dtype)


def _ssm_out(yf, yb, xbc, big, d_skip, gn_g, tm):
    rows = xbc.shape[0]
    dsk = jnp.repeat(d_skip.astype(F32), HD).reshape(1, SSM_INNER)
    return pl.pallas_call(
        _ssm_out_kernel,
        out_shape=jax.ShapeDtypeStruct((rows, SSM_INNER), BF16),
        grid=(rows // tm,),
        in_specs=[
            pl.BlockSpec((tm, SSM_INNER), lambda i: (i, 0)),
            pl.BlockSpec((tm, SSM_INNER), lambda i: (i, 0)),
            pl.BlockSpec((tm, SSM_INNER), lambda i: (i, 0)),
            pl.BlockSpec((tm, SSM_INNER), lambda i: (i, EV_Z_BLK)),
            pl.BlockSpec((1, SSM_INNER), lambda i: (0, 0)),
            pl.BlockSpec((1, SSM_INNER), lambda i: (0, 0)),
        ],
        out_specs=pl.BlockSpec((tm, SSM_INNER), lambda i: (i, 0)),
        compiler_params=_cp(("parallel",)),
        name="ssm_out",
    )(yf, yb, xbc, big, dsk, gn_g.reshape(1, SSM_INNER))


def _ret_out_kernel(rf_ref, rb_ref, rg_ref, g_ref, b_ref, o_ref):
    y = rf_ref[...].astype(F32) + rb_ref[...].astype(F32)
    for h in range(RET_HEADS):
        sl = slice(h * LANES, (h + 1) * LANES)
        seg = y[:, sl]
        mu = jnp.mean(seg, axis=-1, keepdims=True)
        cen = seg - mu
        var = jnp.mean(cen * cen, axis=-1, keepdims=True)
        yn = cen * lax.rsqrt(var + EPS) * g_ref[:, sl] + b_ref[:, sl]
        o_ref[:, sl] = (_silu(rg_ref[:, sl].astype(F32)) * yn).astype(o_ref.dtype)


def _ret_out(rf, rb, pp, gn_g, gn_b, nb, nctx, tm):
    rows = nb * SEQ
    w = RET_HEADS * LANES
    off = nctx // tm
    return pl.pallas_call(
        _ret_out_kernel,
        out_shape=jax.ShapeDtypeStruct((rows, w), BF16),
        grid=(rows // tm,),
        in_specs=[
            pl.BlockSpec((tm, w), lambda i: (i, 0)),
            pl.BlockSpec((tm, w), lambda i: (i, 0)),
            pl.BlockSpec((tm, w), lambda i: (off + i, 1)),
            pl.BlockSpec((1, w), lambda i: (0, 0)),
            pl.BlockSpec((1, w), lambda i: (0, 0)),
        ],
        out_specs=pl.BlockSpec((tm, w), lambda i: (i, 0)),
        compiler_params=_cp(("parallel",)),
        name="ret_out",
    )(rf, rb, pp, gn_g.reshape(1, w), gn_b.reshape(1, w))


def _outproj_kernel(x_ref, a1_ref, *rest, n_ctx_tiles):
    if n_ctx_tiles:
        a1c_ref, a2_ref, w1_ref, w2_ref, mod_ref, o_ref = rest
        a1 = jnp.where(pl.program_id(0) < n_ctx_tiles, a1c_ref[...], a1_ref[...])
    else:
        a2_ref, w1_ref, w2_ref, mod_ref, o_ref = rest
        a1 = a1_ref[...]
    y = (jnp.dot(a1, w1_ref[...], preferred_element_type=F32)
         + jnp.dot(a2_ref[...], w2_ref[...], preferred_element_type=F32))
    o_ref[...] = x_ref[...] + mod_ref[0][2:3] * y


def _outproj(x, a1, a2, w_out, mod, nctx, tm, x_off, a1_ctx=None):
    rows = a2.shape[0]
    k1, k2 = a1.shape[1], a2.shape[1]
    w1 = w_out[:k1].astype(BF16)
    w2 = w_out[k1:].astype(BF16)
    nct = 0 if a1_ctx is None else nctx // tm
    in_specs = [
        pl.BlockSpec((tm, D), lambda i: (i + x_off, 0)),
        pl.BlockSpec((tm, k1), lambda i: (jnp.maximum(i - nct, 0), 0)),
    ]
    args = [x, a1]
    if a1_ctx is not None:
        in_specs.append(pl.BlockSpec((tm, k1), lambda i: (jnp.minimum(i, nct - 1), 0)))
        args.append(a1_ctx)
    in_specs += [
        pl.BlockSpec((tm, k2), lambda i: (i, 0)),
        pl.BlockSpec((k1, D), lambda i: (0, 0)),
        pl.BlockSpec((k2, D), lambda i: (0, 0)),
        pl.BlockSpec((1, 6, D), lambda i: (_mod_row((i + x_off) * tm, nctx), 0, 0)),
    ]
    args += [a2, w1, w2, mod]
    return pl.pallas_call(
        functools.partial(_outproj_kernel, n_ctx_tiles=nct),
        out_shape=jax.ShapeDtypeStruct((rows, D), F32),
        grid=(rows // tm,),
        in_specs=in_specs,
        out_specs=pl.BlockSpec((tm, D), lambda i: (i, 0)),
        compiler_params=_cp(("parallel",)),
        name="outproj",
    )(*args)


def _top2_gates(logits):
    lane = lax.broadcasted_iota(jnp.int32, logits.shape, 1)
    valid = lane < N_EXPERTS
    l0 = jnp.where(valid, logits, NEG)
    m1 = jnp.max(l0, axis=-1, keepdims=True)
    i1 = jnp.min(jnp.where(l0 == m1, lane, LANES), axis=-1, keepdims=True)
    l1 = jnp.where(lane == i1, NEG, l0)
    m2 = jnp.max(l1, axis=-1, keepdims=True)
    i2 = jnp.min(jnp.where(l1 == m2, lane, LANES), axis=-1, keepdims=True)
    e2 = jnp.exp(m2 - m1)
    g1 = 1.0 / (1.0 + e2)
    g2 = e2 / (1.0 + e2)
    return jnp.where(lane == i1, g1, 0.0) + jnp.where(lane == i2, g2, 0.0)


def _ffn_kernel(x_ref, g_ref, mod_ref, *rest, routed, final, sub):
    rest = list(rest)
    router_ref = rest.pop(0) if routed else None
    w1_ref, w3_ref, w2_ref = rest[:3]
    rest = rest[3:]
    fg_ref = rest.pop(0) if final else None
    o_ref, h_ref, acc_ref = rest[:3]
    gate_ref = rest[3] if routed else None
    e = pl.program_id(1)
    f = pl.program_id(2)
    first = jnp.logical_and(e == 0, f == 0)
    last = jnp.logical_and(e == pl.num_programs(1) - 1, f == pl.num_programs(2) - 1)

    @pl.when(first)
    def _():
        h = _norm_mod(x_ref[...], g_ref[...], mod_ref[0], 3)
        h16 = h.astype(BF16)
        h_ref[...] = h16
        acc_ref[...] = jnp.zeros_like(acc_ref)
        if routed:
            hl = (h - h16.astype(F32)).astype(BF16)
            r = router_ref[...]
            rh = r.astype(BF16)
            rl = (r - rh.astype(F32)).astype(BF16)
            logits = (jnp.dot(h16, rh, preferred_element_type=F32)
                      + jnp.dot(h16, rl, preferred_element_type=F32)
                      + jnp.dot(hl, rh, preferred_element_type=F32))
            gate_ref[...] = _top2_gates(logits)

    h16 = h_ref[...]
    if routed:
        lane = lax.broadcasted_iota(jnp.int32, gate_ref.shape, 1)
        ge = jnp.sum(jnp.where(lane == e, gate_ref[...], 0.0), axis=-1, keepdims=True)
    for c in range(w1_ref.shape[2] // sub):
        cols = slice(c * sub, (c + 1) * sub)
        u = _silu(jnp.dot(h16, w1_ref[0, :, cols], preferred_element_type=F32)) * jnp.dot(
            h16, w3_ref[0, :, cols], preferred_element_type=F32)
        if routed:
            u = u * ge
        acc_ref[...] += jnp.dot(u.astype(BF16), w2_ref[0, cols, :], preferred_element_type=F32)

    @pl.when(last)
    def _():
        y = x_ref[...] + mod_ref[0][5:6] * acc_ref[...]
        if final:
            ms = jnp.mean(y * y, axis=-1, keepdims=True)
            y = y * lax.rsqrt(ms + EPS) * fg_ref[...]
        o_ref[...] = y


def _ffn(x, g, mod, w1, w3, w2, nctx, tm, tf, sub, x_off=0, rows=None, router=None,
         final_g=None):
    rows = x.shape[0] if rows is None else rows
    ne, _, dff = w1.shape
    routed = router is not None
    final = final_g is not None
    in_specs = [
        pl.BlockSpec((tm, D), lambda i, e, f: (i + x_off, 0)),
        pl.BlockSpec((1, D), lambda i, e, f: (0, 0)),
        pl.BlockSpec((1, 6, D), lambda i, e, f: (_mod_row((i + x_off) * tm, nctx), 0, 0)),
    ]
    args = [x, g.reshape(1, D), mod]
    if routed:
        in_specs.append(pl.BlockSpec((D, LANES), lambda i, e, f: (0, 0)))
        args.append(jnp.pad(router.astype(F32), ((0, 0), (0, LANES - ne))))
    in_specs += [
        pl.BlockSpec((1, D, tf), lambda i, e, f: (e, 0, f)),
        pl.BlockSpec((1, D, tf), lambda i, e, f: (e, 0, f)),
        pl.BlockSpec((1, tf, D), lambda i, e, f: (e, f, 0)),
    ]
    args += [w1, w3, w2]
    if final:
        in_specs.append(pl.BlockSpec((1, D), lambda i, e, f: (0, 0)))
        args.append(final_g.reshape(1, D))
    scratch = [pltpu.VMEM((tm, D), BF16), pltpu.VMEM((tm, D), F32)]
    if routed:
        scratch.append(pltpu.VMEM((tm, LANES), F32))
    return pl.pallas_call(
        functools.partial(_ffn_kernel, routed=routed, final=final, sub=sub),
        out_shape=jax.ShapeDtypeStruct((rows, D), F32),
        grid=(rows // tm, ne, dff // tf),
        in_specs=in_specs,
        out_specs=pl.BlockSpec((tm, D), lambda i, e, f: (i, 0)),
        scratch_shapes=scratch,
        compiler_params=_cp(("parallel", "arbitrary", "arbitrary")),
        name="moe_ffn" if routed else "dense_ffn",
    )(*args)


def _rope_tables(tm):
    t = np.arange(SEQ)
    n_freq = HD // 4
    inv = ROPE_BASE ** (-np.arange(n_freq, dtype=np.float64) / n_freq)
    ang = np.concatenate([(t // GRID_W)[:, None] * inv, (t % GRID_W)[:, None] * inv], axis=-1)
    cos = np.tile(np.cos(ang), (1, 4))
    sin = np.tile(np.concatenate([-np.sin(ang), np.sin(ang)], axis=-1), (1, 2))
    cos = np.concatenate([np.ones((tm, LANES)), cos], axis=0)
    sin = np.concatenate([np.zeros((tm, LANES)), sin], axis=0)
    return jnp.asarray(cos, F32), jnp.asarray(sin, F32)


def _dup_heads(w, nheads):
    w = w.reshape(D, nheads, 1, HD)
    return jnp.broadcast_to(w, (D, nheads, 2, HD)).reshape(D, nheads * 2 * HD)


def kernel(x, c, ctx, c_ctx, ada_w, ada_b, norm_attn_g, norm_ffn_g, ev_w_in, ev_w_out, na_rpb,
           ssm_conv_w, ssm_conv_b, ssm_dt_bias, ssm_a_log, ssm_d, ssm_norm_g, ffn_w1, ffn_w3, ffn_w2,
           od_w_in, od_w_out, swa_sink, ret_log_decay, ret_gn_g, ret_gn_b, moe_router, moe_w1,
           moe_w3, moe_w2, final_g):
    nb = x.shape[0]
    nctx = -(-(nb * CTX) // SEQ) * SEQ
    tm = 1024
    xs = jnp.concatenate([ctx.reshape(nb * CTX, D), jnp.zeros((nctx - nb * CTX, D), ctx.dtype),
                          x.reshape(nb * SEQ, D)], axis=0).astype(F32)

    rp = -(-(nb + 1) // 8) * 8
    cin = jnp.zeros((rp, D), F32).at[0].set(c_ctx).at[1:nb + 1].set(c)
    mod = _modulation(cin, ada_w, ada_b).reshape(2, rp, 6, D)

    w_in = ev_w_in[0]
    q_, k_, v_, z_, xbc_, dt_ = jnp.split(w_in, [512, 1024, 1536, 2560, 4096], axis=1)
    w_big = jnp.concatenate([xbc_, q_, k_, v_, z_], axis=1).astype(BF16)
    big = _proj(xs, norm_attn_g[0], mod[0], w_big, BF16, nctx, tm, 2048, name="even_in_proj")
    dt_raw = _proj(xs, norm_attn_g[0], mod[0], dt_.astype(BF16), F32, nctx, tm, 2 * SSM_HEADS,
                   name="even_dt_proj")
    attn = _na_attention(big, _na_bias_blocks(na_rpb[0]), nb, nctx)
    attn_ctx = _ctx_attention(big, nctx)
    xbc = _conv_silu(big, ssm_conv_w[0], ssm_conv_b[0], nctx)
    yf, yb = _ssd_scan(xbc, dt_raw, ssm_dt_bias[0], ssm_a_log[0], nb, nctx)
    ys = _ssm_out(yf, yb, xbc, big, ssm_d[0], ssm_norm_g[0], tm)
    xs = _outproj(xs, attn, ys, ev_w_out[0], mod[0], nctx, tm, 0, a1_ctx=attn_ctx)
    xs = _ffn(xs, norm_ffn_g[0], mod[0], ffn_w1.astype(BF16), ffn_w3.astype(BF16),
              ffn_w2.astype(BF16), nctx, tm // 2, D_FF, 256)

    w_in = od_w_in[0]
    q_, k_, v_, rq_, rk_, rv_, rg_ = jnp.split(w_in, [512, 640, 768, 1280, 1792, 2816], axis=1)
    w_rope = jnp.concatenate([q_, rq_, rk_, _dup_heads(k_, 2)], axis=1).astype(BF16)
    w_plain = jnp.concatenate([rv_, rg_, _dup_heads(v_, 2)], axis=1).astype(BF16)
    rope_tabs = _rope_tables(tm)
    rpj = _proj(xs, norm_attn_g[1], mod[1], w_rope, BF16, nctx, tm, w_rope.shape[1],
                rope=rope_tabs, name="odd_rope_proj")
    ppj = _proj(xs, norm_attn_g[1], mod[1], w_plain, BF16, nctx, tm, w_plain.shape[1],
                name="odd_plain_proj")
    yw = _swa_attention(rpj, ppj, swa_sink[0], nb, nctx)
    rf, rb = _retention_scan(rpj, ppj, ret_log_decay[0], nb, nctx)
    yr = _ret_out(rf, rb, ppj, ret_gn_g[0], ret_gn_b[0], nb, nctx, tm)
    x_off = nctx // tm
    xl = _outproj(xs, yw, yr, od_w_out[0], mod[1], nctx, tm, x_off)
    out = _ffn(xl, norm_ffn_g[1], mod[1], moe_w1[0].astype(BF16), moe_w3[0].astype(BF16),
               moe_w2[0].astype(BF16), 0, tm, D_FF_EXPERT // 2, 256, router=moe_router[0],
               final_g=final_g)
    return out.reshape(nb, SEQ, D).astype(x.dtype)
```

```python
import functools

import numpy as np
import jax
import jax.numpy as jnp
from jax import lax
from jax.experimental import pallas as pl
from jax.experimental.pallas import tpu as pltpu

F32 = jnp.float32
BF16 = jnp.bfloat16

D = 1024
SEQ = 2048
CTX = 256
GRID_W = 64
HD = 64
EPS = 1e-6
ROPE_BASE = 10000.0
NEG = -1e30

NA_HEADS = 8
NA_ROWS = 8
NA_COLS = 16
NA_QROWS = 4
NA_KROWS = 12
SSM_HEADS = 16
SSM_INNER = 1024
SSM_STATE = 128
SSM_CONV = 5
SSM_CONV_CH = 1536
CHUNK = 128
SWA_HEADS = 8
SWA_WINDOW = 128
SWA_QT = 512
SWA_KT = SWA_QT + 2 * SWA_WINDOW
RET_HEADS = 8
D_FF = 2816
N_EXPERTS = 8
D_FF_EXPERT = 3584

EV_Q_BLK, EV_K_BLK, EV_V_BLK = 12, 16, 20
EV_Z_BLK = 3

LANES = 128
VMEM_LIMIT = 56 * 1024 * 1024


def _cp(sem, vmem=VMEM_LIMIT):
    return pltpu.CompilerParams(dimension_semantics=sem, vmem_limit_bytes=vmem)


def _sigmoid(x):
    return 1.0 / (1.0 + jnp.exp(-x))


def _silu(x):
    return x * _sigmoid(x)


def _mod_row(start, nctx):
    return jnp.where(start < nctx, 0, 1 + (start - nctx) // SEQ)


def _mod_kernel(c_ref, w_ref, b_ref, o_ref):
    c = c_ref[...]
    h = _silu(c).astype(BF16)
    o_ref[0] = jnp.dot(h, w_ref[0].astype(BF16), preferred_element_type=F32) + b_ref[0]


def _modulation(cin, ada_w, ada_b):
    depth, _, n6 = ada_w.shape
    rp = cin.shape[0]
    tn = 1024
    return pl.pallas_call(
        _mod_kernel,
        out_shape=jax.ShapeDtypeStruct((depth, rp, n6), F32),
        grid=(depth, n6 // tn),
        in_specs=[
            pl.BlockSpec((rp, D), lambda l, j: (0, 0)),
            pl.BlockSpec((1, D, tn), lambda l, j: (l, 0, j)),
            pl.BlockSpec((1, 1, tn), lambda l, j: (l, 0, j)),
        ],
        out_specs=pl.BlockSpec((1, rp, tn), lambda l, j: (l, 0, j)),
        compiler_params=_cp(("arbitrary", "arbitrary")),
        name="adaln_mod",
    )(cin, ada_w, ada_b.reshape(depth, 1, n6))


def _norm_mod(x, g, m, k):
    ms = jnp.mean(x * x, axis=-1, keepdims=True)
    y = x * lax.rsqrt(ms + EPS) * g
    return y * (1.0 + m[k + 1:k + 2]) + m[k:k + 1]


def _swap32(r):
    lane = lax.broadcasted_iota(jnp.int32, r.shape, 1)
    return jnp.where((lane % 64) < 32, pltpu.roll(r, 96, 1), pltpu.roll(r, 32, 1))


def _proj_kernel(x_ref, *rest, rope, n_ctx_tiles, side):
    rest = list(rest)
    xc_ref = rest.pop(0) if n_ctx_tiles else None
    g_ref, mod_ref, w_ref = rest[:3]
    rest = rest[3:]
    ws_ref = rest.pop(0) if side else None
    if rope:
        c_ref, s_ref = rest[:2]
        rest = rest[2:]
    o_ref = rest.pop(0)
    os_ref = rest.pop(0) if side else None
    h_ref, = rest

    @pl.when(pl.program_id(1) == 0)
    def _():
        x = x_ref[...]
        if n_ctx_tiles:
            x = jnp.where(pl.program_id(0) < n_ctx_tiles, xc_ref[...], x)
        h = _norm_mod(x, g_ref[...], mod_ref[0], 0).astype(BF16)
        h_ref[...] = h
        if side:
            os_ref[...] = jnp.dot(h, ws_ref[...], preferred_element_type=F32)

    r = jnp.dot(h_ref[...], w_ref[...], preferred_element_type=F32)
    if rope:
        c = c_ref[...]
        s = s_ref[...]
        for k in range(r.shape[1] // LANES):
            rk = r[:, k * LANES:(k + 1) * LANES]
            o_ref[:, k * LANES:(k + 1) * LANES] = (rk * c + _swap32(rk) * s).astype(o_ref.dtype)
    else:
        o_ref[...] = r.astype(o_ref.dtype)


def _proj(x, g, mod, w, out_dtype, nctx, tm, tn, rope=None, x_ctx=None, w_side=None, name="proj"):
    n = w.shape[1]
    nct = nctx // tm
    if x_ctx is None:
        rows = x.shape[0]
        in_specs = [pl.BlockSpec((tm, D), lambda i, j: (i, 0))]
        args = [x]
    else:
        rows = x.shape[0] + x_ctx.shape[0]
        in_specs = [pl.BlockSpec((tm, D), lambda i, j: (jnp.maximum(i - nct, 0), 0)),
                    pl.BlockSpec((tm, D), lambda i, j: (jnp.minimum(i, nct - 1), 0))]
        args = [x, x_ctx]
    in_specs += [
        pl.BlockSpec((1, D), lambda i, j: (0, 0)),
        pl.BlockSpec((1, 6, D), lambda i, j: (_mod_row(i * tm, nctx), 0, 0)),
        pl.BlockSpec((D, tn), lambda i, j: (0, j)),
    ]
    args += [g.reshape(1, D), mod, w]
    out_shape = jax.ShapeDtypeStruct((rows, n), out_dtype)
    out_specs = pl.BlockSpec((tm, tn), lambda i, j: (i, j))
    if w_side is not None:
        ns = w_side.shape[1]
        in_specs.append(pl.BlockSpec((D, ns), lambda i, j: (0, 0)))
        args.append(w_side)
        out_shape = (out_shape, jax.ShapeDtypeStruct((rows, ns), F32))
        out_specs = (out_specs, pl.BlockSpec((tm, ns), lambda i, j: (i, 0)))
    if rope is not None:
        per = SEQ // tm

        def tab_map(i, j):
            return (jnp.where(i < nct, 0, 1 + (i - nct) % per), 0)
        in_specs += [pl.BlockSpec((tm, LANES), tab_map), pl.BlockSpec((tm, LANES), tab_map)]
        args += list(rope)
    return pl.pallas_call(
        functools.partial(_proj_kernel, rope=rope is not None,
                          n_ctx_tiles=0 if x_ctx is None else nct, side=w_side is not None),
        out_shape=out_shape,
        grid=(rows // tm, n // tn),
        in_specs=in_specs,
        out_specs=out_specs,
        scratch_shapes=[pltpu.VMEM((tm, D), BF16)],
        compiler_params=_cp(("parallel", "arbitrary")),
        name=name,
    )(*args)


def _head_mask(x, a):
    lane = lax.broadcasted_iota(jnp.int32, (1, LANES), 1)
    keep = (lane < HD) if a == 0 else (lane >= HD)
    return jnp.where(keep, x, jnp.zeros_like(x))


def _nt(a, b):
    return lax.dot_general(a, b, (((1,), (1,)), ((), ())), preferred_element_type=F32)


def _two_part_attention(q, kc, vc, kl, vl, bias_fn, sink_fn):
    lane = lax.broadcasted_iota(jnp.int32, (1, LANES), 1)
    outs = []
    for a in range(2):
        qa = _head_mask(q, a) * jnp.asarray(HD ** -0.5, q.dtype)
        s_c = _nt(qa, kc)
        s_l = bias_fn(a, _nt(qa, kl))
        m = jnp.maximum(jnp.max(s_c, axis=-1, keepdims=True), jnp.max(s_l, axis=-1, keepdims=True))
        sink = sink_fn(a)
        if sink is not None:
            m = jnp.maximum(m, sink)
        p_c = jnp.exp(s_c - m)
        p_l = jnp.exp(s_l - m)
        den = jnp.sum(p_c, axis=-1, keepdims=True) + jnp.sum(p_l, axis=-1, keepdims=True)
        if sink is not None:
            den = den + jnp.exp(sink - m)
        o = (jnp.dot(p_c.astype(BF16), vc, preferred_element_type=F32)
             + jnp.dot(p_l.astype(BF16), vl, preferred_element_type=F32))
        outs.append(o / den)
    return jnp.where(lane < HD, outs[0], outs[1])


def _na_window_start(t):
    return jnp.clip(NA_QROWS * t - NA_ROWS // 2, 0, SEQ // GRID_W - NA_KROWS)


NA_NDROW = 2 * NA_ROWS - 1


def _na_bias_tile(toe_ref, a, t):
    lane = lax.broadcasted_iota(jnp.int32, (1, LANES), 1)
    nrows = SEQ // GRID_W
    w0r = _na_window_start(t)
    tile_rows = []
    for rq in range(NA_QROWS):
        r = NA_QROWS * t + rq
        r0 = jnp.clip(r - NA_ROWS // 2, 0, nrows - NA_ROWS)
        blocks = []
        for j in range(NA_KROWS // 2):
            kr = w0r + 2 * j
            e = jnp.clip(kr - r + NA_ROWS, 0, NA_NDROW)
            pen0 = jnp.where(jnp.logical_and(kr >= r0, kr < r0 + NA_ROWS), 0.0, NEG)
            pen1 = jnp.where(jnp.logical_and(kr + 1 >= r0, kr + 1 < r0 + NA_ROWS), 0.0, NEG)
            blocks.append(toe_ref[a, e] + jnp.where(lane < GRID_W, pen0, pen1))
        tile_rows.append(jnp.concatenate(blocks, axis=1))
    return jnp.concatenate(tile_rows, axis=0)


def _na_kernel(q_ref, kl_ref, vl_ref, kc_ref, vc_ref, toe_ref, o_ref):
    t = pl.program_id(1)
    w0 = pl.multiple_of(_na_window_start(t) * GRID_W, GRID_W)
    nk = NA_KROWS * GRID_W
    kl = kl_ref[pl.ds(w0, nk), :]
    vl = vl_ref[pl.ds(w0, nk), :]
    o = _two_part_attention(q_ref[...], kc_ref[...], vc_ref[...], kl, vl,
                            lambda a, s: s + _na_bias_tile(toe_ref, a, t), lambda a: None)
    o_ref[...] = o.astype(o_ref.dtype)


def _na_bias_kernel(rpb_ref, o_ref):
    c = lax.broadcasted_iota(jnp.int32, (GRID_W, LANES), 0)
    lane = lax.broadcasted_iota(jnp.int32, (GRID_W, LANES), 1)
    kc = lane % GRID_W
    c0 = jnp.clip(c - NA_COLS // 2, 0, GRID_W - NA_COLS)
    col_ok = jnp.logical_and(kc >= c0, kc < c0 + NA_COLS)
    neg = jnp.full((GRID_W, LANES), NEG, F32)

    def toeplitz(d, shift):
        v = jnp.broadcast_to(rpb_ref[0, d:d + 1, :], (GRID_W, LANES))
        return pltpu.roll(v, shift % LANES, 1, stride=1, stride_axis=0)

    for e in range(NA_NDROW + 1):
        lo = toeplitz(e - 1, -(NA_COLS - 1)) if e >= 1 else neg
        hi = toeplitz(e, GRID_W - (NA_COLS - 1)) if e < NA_NDROW else neg
        o_ref[0, e] = jnp.where(col_ok, jnp.where(lane < GRID_W, lo, hi), NEG)


def _na_bias_blocks(rpb):
    nd, ncol = rpb.shape[1], rpb.shape[2]
    padded = jnp.pad(rpb.astype(F32), ((0, 0), (0, NA_NDROW + 1 - nd), (0, LANES - ncol)))
    return pl.pallas_call(
        _na_bias_kernel,
        out_shape=jax.ShapeDtypeStruct((NA_HEADS, NA_NDROW + 1, GRID_W, LANES), F32),
        grid=(NA_HEADS,),
        in_specs=[pl.BlockSpec((1, NA_NDROW + 1, LANES), lambda h: (h, 0, 0))],
        out_specs=pl.BlockSpec((1, NA_NDROW + 1, GRID_W, LANES), lambda h: (h, 0, 0, 0)),
        compiler_params=_cp(("arbitrary",)),
        name="na_bias_blocks",
    )(padded)


def _na_attention(big, bias, nb, nctx):
    tq = NA_QROWS * GRID_W
    ntile = SEQ // tq
    nct_q = nctx // tq
    lat0 = nctx // SEQ
    return pl.pallas_call(
        _na_kernel,
        out_shape=jax.ShapeDtypeStruct((nb * SEQ, NA_HEADS * HD), BF16),
        grid=(NA_HEADS // 2, ntile, nb),
        in_specs=[
            pl.BlockSpec((tq, LANES), lambda h, t, b: (nct_q + b * ntile + t, EV_Q_BLK + h)),
            pl.BlockSpec((SEQ, LANES), lambda h, t, b: (lat0 + b, EV_K_BLK + h)),
            pl.BlockSpec((SEQ, LANES), lambda h, t, b: (lat0 + b, EV_V_BLK + h)),
            pl.BlockSpec((CTX, LANES), lambda h, t, b: (b, EV_K_BLK + h)),
            pl.BlockSpec((CTX, LANES), lambda h, t, b: (b, EV_V_BLK + h)),
            pl.BlockSpec((2, NA_NDROW + 1, GRID_W, LANES), lambda h, t, b: (h, 0, 0, 0)),
        ],
        out_specs=pl.BlockSpec((tq, LANES), lambda h, t, b: (b * ntile + t, h)),
        compiler_params=_cp(("arbitrary", "arbitrary", "arbitrary")),
        name="na_attention",
    )(big, big, big, big, big, bias)


def _ctx_attn_kernel(q_ref, k_ref, v_ref, o_ref):
    lane = lax.broadcasted_iota(jnp.int32, (1, LANES), 1)
    q = q_ref[...]
    k = k_ref[...]
    v = v_ref[...]
    outs = []
    for a in range(2):
        qa = _head_mask(q, a) * jnp.asarray(HD ** -0.5, q.dtype)
        s = _nt(qa, k)
        m = jnp.max(s, axis=-1, keepdims=True)
        p = jnp.exp(s - m)
        den = jnp.sum(p, axis=-1, keepdims=True)
        outs.append(jnp.dot(p.astype(BF16), v, preferred_element_type=F32) / den)
    o_ref[...] = jnp.where(lane < HD, outs[0], outs[1]).astype(o_ref.dtype)


def _ctx_attention(big, nctx):
    return pl.pallas_call(
        _ctx_attn_kernel,
        out_shape=jax.ShapeDtypeStruct((nctx, NA_HEADS * HD), BF16),
        grid=(nctx // CTX, NA_HEADS // 2),
        in_specs=[
            pl.BlockSpec((CTX, LANES), lambda b, h: (b, EV_Q_BLK + h)),
            pl.BlockSpec((CTX, LANES), lambda b, h: (b, EV_K_BLK + h)),
            pl.BlockSpec((CTX, LANES), lambda b, h: (b, EV_V_BLK + h)),
        ],
        out_specs=pl.BlockSpec((CTX, LANES), lambda b, h: (b, h)),
        compiler_params=_cp(("arbitrary", "arbitrary")),
        name="ctx_attention",
    )(big, big, big)


def _swa_kernel(q_ref, kl_ref, vl_ref, kc_ref, vc_ref, sink_ref, o_ref):
    h2 = pl.program_id(1)
    n = pl.program_id(2)
    start = pl.multiple_of(jnp.clip(SWA_QT * n - SWA_WINDOW, 0, SEQ - SWA_KT), SWA_WINDOW)
    kl = kl_ref[pl.ds(start, SWA_KT), :]
    vl = vl_ref[pl.ds(start, SWA_KT), :]
    qpos = SWA_QT * n + lax.broadcasted_iota(jnp.int32, (SWA_QT, 1), 0)
    kpos = start + lax.broadcasted_iota(jnp.int32, (1, SWA_KT), 1)
    ok = jnp.abs(kpos - qpos) <= SWA_WINDOW

    def sink(a):
        return sink_ref[pl.ds(2 * h2 + a, 1), :][:, :1]

    o = _two_part_attention(q_ref[...], kc_ref[...], vc_ref[...], kl, vl,
                            lambda a, s: jnp.where(ok, s, NEG), sink)
    o_ref[...] = o.astype(o_ref.dtype)


def _swa_attention(rp, pp, sink, nb, nctx):
    nq = SEQ // SWA_QT
    nct_q = nctx // SWA_QT
    lat0 = nctx // SEQ
    sinkv = jnp.broadcast_to(sink.astype(F32)[:, None], (SWA_HEADS, LANES))
    return pl.pallas_call(
        _swa_kernel,
        out_shape=jax.ShapeDtypeStruct((nb * SEQ, SWA_HEADS * HD), BF16),
        grid=(nb, SWA_HEADS // 2, nq),
        in_specs=[
            pl.BlockSpec((SWA_QT, LANES), lambda b, h, n: (nct_q + b * nq + n, h)),
            pl.BlockSpec((SEQ, LANES), lambda b, h, n: (lat0 + b, 12 + h // 2)),
            pl.BlockSpec((SEQ, LANES), lambda b, h, n: (lat0 + b, 16 + h // 2)),
            pl.BlockSpec((CTX, LANES), lambda b, h, n: (b, 12 + h // 2)),
            pl.BlockSpec((CTX, LANES), lambda b, h, n: (b, 16 + h // 2)),
            pl.BlockSpec((SWA_HEADS, LANES), lambda b, h, n: (0, 0)),
        ],
        out_specs=pl.BlockSpec((SWA_QT, LANES), lambda b, h, n: (b * nq + n, h)),
        compiler_params=_cp(("arbitrary", "arbitrary", "arbitrary")),
        name="swa_attention",
    )(rp, rp, pp, rp, pp, sinkv)


CONV_PAD = (SSM_CONV - 1) // 2
CONV_TAPS = tuple(s for s in range(-CONV_PAD, CONV_PAD + 1) if s != 0)
CONV_HALO = 16
CONV_EDGE = 8


def _conv_shift_tables(tb):
    nt = len(CONV_TAPS)
    shift = np.zeros((nt * tb, tb), np.float32)
    edge = np.zeros((nt * 2 * CONV_EDGE, 2 * CONV_HALO), np.float32)
    for t, s in enumerate(CONV_TAPS):
        for i in range(tb):
            j = i + s
            if 0 <= j < tb:
                shift[t * tb + i, j] = 1.0
            elif j < 0:
                edge[t * 2 * CONV_EDGE + i, CONV_HALO + j] = 1.0
            else:
                edge[t * 2 * CONV_EDGE + CONV_EDGE + i - (tb - CONV_EDGE), CONV_HALO + j - tb] = 1.0
    return jnp.asarray(shift, BF16), jnp.asarray(edge, BF16)


def _conv_kernel(xp_ref, x_ref, xn_ref, sh_ref, ed_ref, w_ref, b_ref, o_ref, *, nctx_blocks,
                 per_seq):
    i = pl.program_id(0)
    tb = x_ref.shape[0]
    k = i - nctx_blocks
    has_prev = jnp.logical_and(i >= nctx_blocks, k % per_seq != 0)
    has_next = jnp.logical_and(i >= nctx_blocks, k % per_seq != per_seq - 1)
    x = x_ref[...]
    zero = jnp.zeros((CONV_HALO, x.shape[1]), x.dtype)
    halo = jnp.concatenate([jnp.where(has_prev, xp_ref[...], zero),
                            jnp.where(has_next, xn_ref[...], zero)], axis=0)
    taps = jnp.dot(sh_ref[...], x, preferred_element_type=F32)
    edges = jnp.dot(ed_ref[...], halo, preferred_element_type=F32)
    acc = x.astype(F32) * w_ref[CONV_PAD:CONV_PAD + 1, :] + b_ref[...]
    for t, s in enumerate(CONV_TAPS):
        tap = taps[t * tb:(t + 1) * tb]
        e0 = 2 * CONV_EDGE * t
        tap = jnp.concatenate([tap[:CONV_EDGE] + edges[e0:e0 + CONV_EDGE],
                               tap[CONV_EDGE:tb - CONV_EDGE],
                               tap[tb - CONV_EDGE:] + edges[e0 + CONV_EDGE:e0 + 2 * CONV_EDGE]], axis=0)
        acc = acc + tap * w_ref[CONV_PAD + s:CONV_PAD + s + 1, :]
    o_ref[...] = _silu(acc).astype(o_ref.dtype)


def _conv_silu(big, conv_w, conv_b, nctx):
    rows = big.shape[0]
    tb = CTX
    tc = SSM_CONV_CH
    nblk = rows // tb
    hpb = tb // CONV_HALO
    shift, edge = _conv_shift_tables(tb)
    return pl.pallas_call(
        functools.partial(_conv_kernel, nctx_blocks=nctx // tb, per_seq=SEQ // tb),
        out_shape=jax.ShapeDtypeStruct((rows, SSM_CONV_CH), BF16),
        grid=(nblk, SSM_CONV_CH // tc),
        in_specs=[
            pl.BlockSpec((CONV_HALO, tc), lambda i, c: (jnp.maximum(i * hpb - 1, 0), c)),
            pl.BlockSpec((tb, tc), lambda i, c: (i, c)),
            pl.BlockSpec((CONV_HALO, tc),
                         lambda i, c: (jnp.minimum((i + 1) * hpb, nblk * hpb - 1), c)),
            pl.BlockSpec(shift.shape, lambda i, c: (0, 0)),
            pl.BlockSpec(edge.shape, lambda i, c: (0, 0)),
            pl.BlockSpec((SSM_CONV, tc), lambda i, c: (0, c)),
            pl.BlockSpec((1, tc), lambda i, c: (0, c)),
        ],
        out_specs=pl.BlockSpec((tb, tc), lambda i, c: (i, c)),
        compiler_params=_cp(("arbitrary", "arbitrary")),
        name="conv_silu",
    )(big, big, big, shift, edge, conv_w.reshape(SSM_CONV, SSM_CONV_CH),
      conv_b.reshape(1, SSM_CONV_CH))


NCC = CTX // CHUNK
NLC = SEQ // CHUNK


def _chunk_block(b, s, d, nctx):
    cc = s if d == 0 else NCC - 1 - s
    lc = s - NCC if d == 0 else NLC + NCC - 1 - s
    return jnp.where(s < NCC, NCC * b + cc, nctx // CHUNK + NLC * b + lc)


def _latent_chunk_block(b, s, d):
    lc = s - NCC if d == 0 else NLC + NCC - 1 - s
    return NLC * b + jnp.clip(lc, 0, NLC - 1)


def _split3(a):
    hi = a.astype(BF16)
    r1 = a - hi.astype(F32)
    mid = r1.astype(BF16)
    lo = (r1 - mid.astype(F32)).astype(BF16)
    return hi, mid, lo


def _softplus(x):
    return jnp.maximum(x, 0.0) + jnp.log(1.0 + jnp.exp(-jnp.abs(x)))


def _ssd_direction(d, x_ref, b_ref, c_ref, dt_ref, dtt_ref, bias_ref, biast_ref, alog_ref,
                   alogt_ref, y_ref, s_ref):
    q = CHUNK
    ii = lax.broadcasted_iota(jnp.int32, (q, q), 0)
    jj = lax.broadcasted_iota(jnp.int32, (q, q), 1)
    causal = (jj <= ii) if d == 0 else (jj >= ii)
    tri = jnp.where(causal, 1.0, 0.0).astype(BF16)
    trit = jnp.where((ii <= jj) if d == 0 else (ii >= jj), 1.0, 0.0).astype(BF16)

    dt = _softplus(dt_ref[0] + bias_ref[d])
    a = dt * (-jnp.exp(alog_ref[d]))
    dtt = _softplus(dtt_ref[0] + biast_ref[d])
    at = dtt * (-jnp.exp(alogt_ref[d]))
    acum = sum(jnp.dot(tri, p, preferred_element_type=F32) for p in _split3(a))
    acumt = sum(jnp.dot(p, trit, preferred_element_type=F32) for p in _split3(at))
    atot = jnp.sum(at, axis=1, keepdims=True)
    wrow = dtt * jnp.exp(atot - acumt)
    keep = jnp.exp(jnp.broadcast_to(atot, (SSM_HEADS, LANES)))

    first = lax.broadcasted_iota(jnp.int32, (1, LANES), 1) < HD
    ngroups = b_ref.shape[1] // SSM_STATE
    pairs_per_group = SSM_HEADS // 2 // ngroups
    for g in range(ngroups):
        bg = b_ref[:, g * SSM_STATE:(g + 1) * SSM_STATE]
        cg = c_ref[:, g * SSM_STATE:(g + 1) * SSM_STATE]
        cb = _nt(cg, bg)
        cgf = cg.astype(F32)
        bt = bg.astype(F32).T
        for pp in range(pairs_per_group):
            p = g * pairs_per_group + pp
            x = x_ref[:, p * LANES:(p + 1) * LANES]
            st = s_ref[d, p]
            rhs = jnp.concatenate([x, st.astype(BF16)], axis=0)
            ys, us = [], []
            for h in (2 * p, 2 * p + 1):
                col = jnp.broadcast_to(acum[:, h:h + 1], (q, LANES))
                dec = jnp.exp(jnp.where(causal, col - acumt[h:h + 1, :], NEG)) * (cb * dtt[h:h + 1, :])
                lhs = jnp.concatenate([dec, jnp.exp(col) * cgf], axis=1).astype(BF16)
                ys.append(jnp.dot(lhs, rhs, preferred_element_type=F32))
                us.append(jnp.dot((bt * wrow[h:h + 1, :]).astype(BF16), x,
                                  preferred_element_type=F32))
            y_ref[:, p * LANES:(p + 1) * LANES] = jnp.where(first, ys[0], ys[1]).astype(y_ref.dtype)
            keep_p = jnp.where(first, keep[2 * p:2 * p + 1, :], keep[2 * p + 1:2 * p + 2, :])
            s_ref[d, p] = st * keep_p + jnp.where(first, us[0], us[1])


def _ssd_kernel(*refs):
    fwd, bwd, (bias_ref, biast_ref, alog_ref, alogt_ref), (yf_ref, yb_ref, s_ref) = (
        refs[0:5], refs[5:10], refs[10:14], refs[14:17])
    assert CHUNK == LANES

    @pl.when(pl.program_id(1) == 0)
    def _():
        s_ref[...] = jnp.zeros_like(s_ref)

    for d, ins, y_ref in ((0, fwd, yf_ref), (1, bwd, yb_ref)):
        _ssd_direction(d, *ins, bias_ref, biast_ref, alog_ref, alogt_ref, y_ref, s_ref)


def _ssd_scan(xbc, dt_raw, dt_bias, a_log, nb, nctx):
    rows = xbc.shape[0]
    dtd = dt_raw.reshape(rows, 2, SSM_HEADS).transpose(1, 0, 2)
    dtt = dtd.transpose(0, 2, 1)

    def direction_specs(d):
        blk = functools.partial(_chunk_block, d=d, nctx=nctx)
        return [
            pl.BlockSpec((CHUNK, SSM_INNER), lambda b, s: (blk(b, s), 0)),
            pl.BlockSpec((CHUNK, 2 * SSM_STATE), lambda b, s: (blk(b, s), 4)),
            pl.BlockSpec((CHUNK, 2 * SSM_STATE), lambda b, s: (blk(b, s), 5)),
            pl.BlockSpec((1, CHUNK, SSM_HEADS), lambda b, s: (d, blk(b, s), 0)),
            pl.BlockSpec((1, SSM_HEADS, CHUNK), lambda b, s: (d, 0, blk(b, s))),
        ]

    whole = lambda shape: pl.BlockSpec(shape, lambda b, s: (0, 0, 0))
    out_spec = lambda d: pl.BlockSpec(
        (CHUNK, SSM_INNER), lambda b, s: (_chunk_block(b, s, d, nctx), 0))
    y_shape = jax.ShapeDtypeStruct((rows, SSM_INNER), BF16)
    return pl.pallas_call(
        _ssd_kernel,
        out_shape=(y_shape, y_shape),
        grid=(nb, NCC + NLC),
        in_specs=direction_specs(0) + direction_specs(1) + [
            whole((2, 1, SSM_HEADS)), whole((2, SSM_HEADS, 1)),
            whole((2, 1, SSM_HEADS)), whole((2, SSM_HEADS, 1)),
        ],
        out_specs=(out_spec(0), out_spec(1)),
        scratch_shapes=[pltpu.VMEM((2, SSM_HEADS // 2, SSM_STATE, LANES), F32)],
        compiler_params=_cp(("arbitrary", "arbitrary")),
        name="ssd_scan",
    )(*([xbc, xbc, xbc, dtd, dtt] * 2),
      dt_bias.reshape(2, 1, SSM_HEADS), dt_bias.reshape(2, SSM_HEADS, 1),
      a_log.reshape(2, 1, SSM_HEADS), a_log.reshape(2, SSM_HEADS, 1))


def _ret_geometry(d):
    q = CHUNK
    ii = lax.broadcasted_iota(jnp.int32, (q, q), 0)
    jj = lax.broadcasted_iota(jnp.int32, (q, q), 1)
    dist = (ii - jj) if d == 0 else (jj - ii)
    ri = lax.broadcasted_iota(jnp.int32, (q, LANES), 0)
    cj = lax.broadcasted_iota(jnp.int32, (1, q), 1)
    steps_in = ((ri + 1) if d == 0 else (q - ri)).astype(F32)
    steps_out = ((q - 1 - cj) if d == 0 else cj).astype(F32)
    return dist >= 0, dist.astype(F32), steps_in, steps_out


def _ret_kernel(*refs):
    fwd, bwd, ld_ref, (yf_ref, yb_ref, s_ref) = refs[0:3], refs[3:6], refs[6], refs[7:10]
    q = CHUNK
    step = pl.program_id(1)

    @pl.when(step == 0)
    def _():
        s_ref[...] = jnp.zeros_like(s_ref)

    scale = jnp.asarray(HD ** -0.5, BF16)
    dirs = ((0, fwd, yf_ref), (1, bwd, yb_ref))

    def log_gamma(d, h):
        lg = jnp.log(1.0 - jnp.exp(ld_ref[d, h]))
        return jnp.broadcast_to(lg[None], (q // 8, 8, LANES)).reshape(q, LANES)

    qk = {}
    for d, (c_ref, b_ref, x_ref), y_ref in dirs:
        for h in range(RET_HEADS):
            pb, a = h // 2, h % 2
            ch = _head_mask(c_ref[:, pb * LANES:(pb + 1) * LANES], a) * scale
            bh = _head_mask(b_ref[:, pb * LANES:(pb + 1) * LANES], a)
            qk[d, h] = (ch, _nt(ch, bh))

    for d, (c_ref, b_ref, x_ref), y_ref in dirs:
        causal, distf, steps_in, _ = _ret_geometry(d)
        for h in range(RET_HEADS):
            lg = log_gamma(d, h)
            ch, cb = qk[d, h]
            dec = jnp.exp(jnp.where(causal, distf * lg, NEG)) * cb
            grow = jnp.exp(steps_in * lg) * ch.astype(F32)
            lhs = jnp.concatenate([dec, grow], axis=1).astype(BF16)
            x = x_ref[:, h * LANES:(h + 1) * LANES]
            rhs = jnp.concatenate([x, s_ref[d, h].astype(BF16)], axis=0)
            y_ref[:, h * LANES:(h + 1) * LANES] = jnp.dot(
                lhs, rhs, preferred_element_type=F32).astype(y_ref.dtype)

    for d, (c_ref, b_ref, x_ref), y_ref in dirs:
        _, _, _, steps_out = _ret_geometry(d)
        for h in range(RET_HEADS):
            lg = log_gamma(d, h)
            pb, a = h // 2, h % 2
            bt = _head_mask(b_ref[:, pb * LANES:(pb + 1) * LANES], a).astype(F32).T
            bth = (bt * jnp.exp(steps_out * lg[0:1, :])).astype(BF16)
            x = x_ref[:, h * LANES:(h + 1) * LANES]
            s_ref[d, h] = s_ref[d, h] * jnp.exp(q * lg) + jnp.dot(
                bth, x, preferred_element_type=F32)


def _retention_scan(rp, pp, log_decay, nb, nctx):
    w = RET_HEADS * HD
    wv = RET_HEADS * LANES

    def direction_specs(d):
        blk = functools.partial(_chunk_block, d=d, nctx=nctx)
        return [
            pl.BlockSpec((CHUNK, w), lambda b, s: (blk(b, s), 1)),
            pl.BlockSpec((CHUNK, w), lambda b, s: (blk(b, s), 2)),
            pl.BlockSpec((CHUNK, wv), lambda b, s: (blk(b, s), 0)),
        ]

    out_spec = lambda d: pl.BlockSpec((CHUNK, wv), lambda b, s: (_latent_chunk_block(b, s, d), 0))
    y_shape = jax.ShapeDtypeStruct((nb * SEQ, wv), BF16)
    return pl.pallas_call(
        _ret_kernel,
        out_shape=(y_shape, y_shape),
        grid=(nb, NCC + NLC),
        in_specs=direction_specs(0) + direction_specs(1) + [
            pl.BlockSpec((2, RET_HEADS, 8, LANES), lambda b, s: (0, 0, 0, 0))],
        out_specs=(out_spec(0), out_spec(1)),
        scratch_shapes=[pltpu.VMEM((2, RET_HEADS, LANES, LANES), F32)],
        compiler_params=_cp(("arbitrary", "arbitrary")),
        name="retention_scan",
    )(*([rp, rp, pp] * 2),
      jnp.broadcast_to(log_decay.astype(F32)[:, :, None, None], (2, RET_HEADS, 8, LANES)))


def _ssm_gate_norm(yf_ref, yb_ref, xs_ref, z_ref, dsk_ref, g_ref):
    y = yf_ref[...].astype(F32) + yb_ref[...].astype(F32) + xs_ref[...].astype(F32) * dsk_ref[...]
    yz = y * _silu(z_ref[...].astype(F32))
    half = SSM_INNER // 2
    segs = []
    for g in range(2):
        seg = yz[:, g * half:(g + 1) * half]
        ms = jnp.mean(seg * seg, axis=-1, keepdims=True)
        segs.append((seg * lax.rsqrt(ms + EPS) * g_ref[:, g * half:(g + 1) * half]).astype(BF16))
    return jnp.concatenate(segs, axis=1)


def _ret_gate_norm(rf_ref, rb_ref, rg_ref, g_ref, b_ref):
    y = rf_ref[...].astype(F32) + rb_ref[...].astype(F32)
    segs = []
    for h in range(RET_HEADS):
        sl = slice(h * LANES, (h + 1) * LANES)
        seg = y[:, sl]
        mu = jnp.mean(seg, axis=-1, keepdims=True)
        cen = seg - mu
        var = jnp.mean(cen * cen, axis=-1, keepdims=True)
        yn = cen * lax.rsqrt(var + EPS) * g_ref[:, sl] + b_ref[:, sl]
        segs.append((_silu(rg_ref[:, sl].astype(F32)) * yn).astype(BF16))
    return jnp.concatenate(segs, axis=1)


def _outproj_kernel(*refs, n_ctx_tiles, split_x, split_a1, mixer_fn):
    refs = list(refs)
    x_ref = refs.pop(0)
    xc_ref = refs.pop(0) if split_x else None
    a1_ref = refs.pop(0)
    a1c_ref = refs.pop(0) if split_a1 else None
    w1_ref, w2_ref, mod_ref, o_ref = refs[-4:]
    is_ctx = pl.program_id(0) < n_ctx_tiles
    a1 = jnp.where(is_ctx, a1c_ref[...], a1_ref[...]) if split_a1 else a1_ref[...]
    x = jnp.where(is_ctx, xc_ref[...], x_ref[...]) if split_x else x_ref[...]
    a2 = mixer_fn(*refs[:-4])
    y = (jnp.dot(a1, w1_ref[...], preferred_element_type=F32)
         + jnp.dot(a2, w2_ref[...], preferred_element_type=F32))
    o_ref[...] = x + mod_ref[0][2:3] * y


def _outproj(x, a1, mixer_fn, mixer_args, mixer_specs, w_out, mod, nctx, tm, rows, x_off=0,
             a1_ctx=None, x_ctx=None):
    k1 = a1.shape[1]
    k2 = w_out.shape[0] - k1
    w1 = w_out[:k1].astype(BF16)
    w2 = w_out[k1:].astype(BF16)
    nct = nctx // tm if (a1_ctx is not None or x_ctx is not None) else 0
    lat = lambda i: (jnp.maximum(i - nct, 0), 0)
    ctx_ = lambda i: (jnp.minimum(i, nct - 1), 0)
    in_specs, args = [], []
    if x_ctx is None:
        in_specs.append(pl.BlockSpec((tm, D), lambda i: (i + x_off, 0)))
        args.append(x)
    else:
        in_specs += [pl.BlockSpec((tm, D), lat), pl.BlockSpec((tm, D), ctx_)]
        args += [x, x_ctx]
    in_specs.append(pl.BlockSpec((tm, k1), lat))
    args.append(a1)
    if a1_ctx is not None:
        in_specs.append(pl.BlockSpec((tm, k1), ctx_))
        args.append(a1_ctx)
    in_specs += list(mixer_specs) + [
        pl.BlockSpec((k1, D), lambda i: (0, 0)),
        pl.BlockSpec((k2, D), lambda i: (0, 0)),
        pl.BlockSpec((1, 6, D), lambda i: (_mod_row((i + x_off) * tm, nctx), 0, 0)),
    ]
    args += list(mixer_args) + [w1, w2, mod]
    return pl.pallas_call(
        functools.partial(_outproj_kernel, n_ctx_tiles=nct, split_x=x_ctx is not None,
                          split_a1=a1_ctx is not None, mixer_fn=mixer_fn),
        out_shape=jax.ShapeDtypeStruct((rows, D), F32),
        grid=(rows // tm,),
        in_specs=in_specs,
        out_specs=pl.BlockSpec((tm, D), lambda i: (i, 0)),
        compiler_params=_cp(("parallel",)),
        name="outproj",
    )(*args)


def _top2_gates(logits):
    lane = lax.broadcasted_iota(jnp.int32, logits.shape, 1)
    valid = lane < N_EXPERTS
    l0 = jnp.where(valid, logits, NEG)
    m1 = jnp.max(l0, axis=-1, keepdims=True)
    i1 = jnp.min(jnp.where(l0 == m1, lane, LANES), axis=-1, keepdims=True)
    l1 = jnp.where(lane == i1, NEG, l0)
    m2 = jnp.max(l1, axis=-1, keepdims=True)
    i2 = jnp.min(jnp.where(l1 == m2, lane, LANES), axis=-1, keepdims=True)
    e2 = jnp.exp(m2 - m1)
    g1 = 1.0 / (1.0 + e2)
    g2 = e2 / (1.0 + e2)
    return jnp.where(lane == i1, g1, 0.0) + jnp.where(lane == i2, g2, 0.0)


def _ffn_kernel(x_ref, g_ref, mod_ref, *rest, routed, final, sub):
    rest = list(rest)
    router_ref = rest.pop(0) if routed else None
    w1_ref, w3_ref, w2_ref = rest[:3]
    rest = rest[3:]
    fg_ref = rest.pop(0) if final else None
    o_ref, h_ref, acc_ref = rest[:3]
    gate_ref = rest[3] if routed else None
    e = pl.program_id(1)
    f = pl.program_id(2)
    first = jnp.logical_and(e == 0, f == 0)
    last = jnp.logical_and(e == pl.num_programs(1) - 1, f == pl.num_programs(2) - 1)

    @pl.when(first)
    def _():
        h = _norm_mod(x_ref[...], g_ref[...], mod_ref[0], 3)
        h16 = h.astype(BF16)
        h_ref[...] = h16
        acc_ref[...] = jnp.zeros_like(acc_ref)
        if routed:
            hl = (h - h16.astype(F32)).astype(BF16)
            r = router_ref[...]
            rh = r.astype(BF16)
            rl = (r - rh.astype(F32)).astype(BF16)
            logits = (jnp.dot(h16, rh, preferred_element_type=F32)
                      + jnp.dot(h16, rl, preferred_element_type=F32)
                      + jnp.dot(hl, rh, preferred_element_type=F32))
            gate_ref[...] = _top2_gates(logits)

    h16 = h_ref[...]
    if routed:
        lane = lax.broadcasted_iota(jnp.int32, gate_ref.shape, 1)
        ge = jnp.sum(jnp.where(lane == e, gate_ref[...], 0.0), axis=-1, keepdims=True)
    for c in range(w1_ref.shape[2] // sub):
        cols = slice(c * sub, (c + 1) * sub)
        u = _silu(jnp.dot(h16, w1_ref[0, :, cols], preferred_element_type=F32)) * jnp.dot(
            h16, w3_ref[0, :, cols], preferred_element_type=F32)
        if routed:
            u = u * ge
        acc_ref[...] += jnp.dot(u.astype(BF16), w2_ref[0, cols, :], preferred_element_type=F32)

    @pl.when(last)
    def _():
        y = x_ref[...] + mod_ref[0][5:6] * acc_ref[...]
        if final:
            ms = jnp.mean(y * y, axis=-1, keepdims=True)
            y = y * lax.rsqrt(ms + EPS) * fg_ref[...]
        o_ref[...] = y


def _ffn(x, g, mod, w1, w3, w2, nctx, tm, tf, sub, x_off=0, rows=None, router=None,
         final_g=None):
    rows = x.shape[0] if rows is None else rows
    ne, _, dff = w1.shape
    routed = router is not None
    final = final_g is not None
    in_specs = [
        pl.BlockSpec((tm, D), lambda i, e, f: (i + x_off, 0)),
        pl.BlockSpec((1, D), lambda i, e, f: (0, 0)),
        pl.BlockSpec((1, 6, D), lambda i, e, f: (_mod_row((i + x_off) * tm, nctx), 0, 0)),
    ]
    args = [x, g.reshape(1, D), mod]
    if routed:
        in_specs.append(pl.BlockSpec((D, LANES), lambda i, e, f: (0, 0)))
        args.append(jnp.pad(router.astype(F32), ((0, 0), (0, LANES - ne))))
    in_specs += [
        pl.BlockSpec((1, D, tf), lambda i, e, f: (e, 0, f)),
        pl.BlockSpec((1, D, tf), lambda i, e, f: (e, 0, f)),
        pl.BlockSpec((1, tf, D), lambda i, e, f: (e, f, 0)),
    ]
    args += [w1, w3, w2]
    if final:
        in_specs.append(pl.BlockSpec((1, D), lambda i, e, f: (0, 0)))
        args.append(final_g.reshape(1, D))
    scratch = [pltpu.VMEM((tm, D), BF16), pltpu.VMEM((tm, D), F32)]
    if routed:
        scratch.append(pltpu.VMEM((tm, LANES), F32))
    return pl.pallas_call(
        functools.partial(_ffn_kernel, routed=routed, final=final, sub=sub),
        out_shape=jax.ShapeDtypeStruct((rows, D), F32),
        grid=(rows // tm, ne, dff // tf),
        in_specs=in_specs,
        out_specs=pl.BlockSpec((tm, D), lambda i, e, f: (i, 0)),
        scratch_shapes=scratch,
        compiler_params=_cp(("parallel", "arbitrary", "arbitrary")),
        name="moe_ffn" if routed else "dense_ffn",
    )(*args)


def _rope_tables(tm):
    t = np.arange(SEQ)
    n_freq = HD // 4
    inv = ROPE_BASE ** (-np.arange(n_freq, dtype=np.float64) / n_freq)
    ang = np.concatenate([(t // GRID_W)[:, None] * inv, (t % GRID_W)[:, None] * inv], axis=-1)
    cos = np.tile(np.cos(ang), (1, 4))
    sin = np.tile(np.concatenate([-np.sin(ang), np.sin(ang)], axis=-1), (1, 2))
    cos = np.concatenate([np.ones((tm, LANES)), cos], axis=0)
    sin = np.concatenate([np.zeros((tm, LANES)), sin], axis=0)
    return jnp.asarray(cos, F32), jnp.asarray(sin, F32)


def _dup_heads(w, nheads):
    w = w.reshape(D, nheads, 1, HD)
    return jnp.broadcast_to(w, (D, nheads, 2, HD)).reshape(D, nheads * 2 * HD)


def kernel(x, c, ctx, c_ctx, ada_w, ada_b, norm_attn_g, norm_ffn_g, ev_w_in, ev_w_out, na_rpb,
           ssm_conv_w, ssm_conv_b, ssm_dt_bias, ssm_a_log, ssm_d, ssm_norm_g, ffn_w1, ffn_w3, ffn_w2,
           od_w_in, od_w_out, swa_sink, ret_log_decay, ret_gn_g, ret_gn_b, moe_router, moe_w1,
           moe_w3, moe_w2, final_g):
    nb = x.shape[0]
    nctx = -(-(nb * CTX) // SEQ) * SEQ
    tm = 1024
    tmo = tm // 2
    x_lat = x.reshape(nb * SEQ, D).astype(F32)
    x_ctx = jnp.pad(ctx.reshape(nb * CTX, D).astype(F32), ((0, nctx - nb * CTX), (0, 0)))

    rp = -(-(nb + 1) // 8) * 8
    cin = jnp.zeros((rp, D), F32).at[0].set(c_ctx).at[1:nb + 1].set(c)
    mod = _modulation(cin, ada_w, ada_b).reshape(2, rp, 6, D)

    w_in = ev_w_in[0]
    q_, k_, v_, z_, xbc_, dt_ = jnp.split(w_in, [512, 1024, 1536, 2560, 4096], axis=1)
    w_big = jnp.concatenate([xbc_, q_, k_, v_, z_], axis=1).astype(BF16)
    big, dt_raw = _proj(x_lat, norm_attn_g[0], mod[0], w_big, BF16, nctx, tm, 2048, x_ctx=x_ctx,
                        w_side=dt_.astype(BF16), name="even_in_proj")
    attn = _na_attention(big, _na_bias_blocks(na_rpb[0]), nb, nctx)
    attn_ctx = _ctx_attention(big, nctx)
    xbc = _conv_silu(big, ssm_conv_w[0], ssm_conv_b[0], nctx)
    yf, yb = _ssd_scan(xbc, dt_raw, ssm_dt_bias[0], ssm_a_log[0], nb, nctx)
    wide = lambda col: pl.BlockSpec((tmo, SSM_INNER), lambda i: (i, col))
    vec = pl.BlockSpec((1, SSM_INNER), lambda i: (0, 0))
    dsk = jnp.repeat(ssm_d[0].astype(F32), HD).reshape(1, SSM_INNER)
    xs = _outproj(x_lat, attn, _ssm_gate_norm,
                  [yf, yb, xbc, big, dsk, ssm_norm_g[0].reshape(1, SSM_INNER)],
                  [wide(0), wide(0), wide(0), wide(EV_Z_BLK), vec, vec],
                  ev_w_out[0], mod[0], nctx, tmo, nctx + nb * SEQ, a1_ctx=attn_ctx, x_ctx=x_ctx)
    xs = _ffn(xs, norm_ffn_g[0], mod[0], ffn_w1.astype(BF16), ffn_w3.astype(BF16),
              ffn_w2.astype(BF16), nctx, tm // 2, D_FF, 256)

    w_in = od_w_in[0]
    q_, k_, v_, rq_, rk_, rv_, rg_ = jnp.split(w_in, [512, 640, 768, 1280, 1792, 2816], axis=1)
    w_rope = jnp.concatenate([q_, rq_, rk_, _dup_heads(k_, 2)], axis=1).astype(BF16)
    w_plain = jnp.concatenate([rv_, rg_, _dup_heads(v_, 2)], axis=1).astype(BF16)
    rope_tabs = _rope_tables(tm)
    rpj = _proj(xs, norm_attn_g[1], mod[1], w_rope, BF16, nctx, tm, w_rope.shape[1],
                rope=rope_tabs, name="odd_rope_proj")
    ppj = _proj(xs, norm_attn_g[1], mod[1], w_plain, BF16, nctx, tm, w_plain.shape[1],
                name="odd_plain_proj")
    yw = _swa_attention(rpj, ppj, swa_sink[0], nb, nctx)
    rf, rb = _retention_scan(rpj, ppj, ret_log_decay[0], nb, nctx)
    x_off = nctx // tmo
    rg_spec = pl.BlockSpec((tmo, SSM_INNER), lambda i: (i + x_off, 1))
    xl = _outproj(xs, yw, _ret_gate_norm,
                  [rf, rb, ppj, ret_gn_g[0].reshape(1, SSM_INNER), ret_gn_b[0].reshape(1, SSM_INNER)],
                  [wide(0), wide(0), rg_spec, vec, vec],
                  od_w_out[0], mod[1], nctx, tmo, nb * SEQ, x_off=x_off)
    out = _ffn(xl, norm_ffn_g[1], mod[1], moe_w1[0].astype(BF16), moe_w3[0].astype(BF16),
               moe_w2[0].astype(BF16), 0, tm, D_FF_EXPERT // 2, 256, router=moe_router[0],
               final_g=final_g)
    return out.reshape(nb, SEQ, D).astype(x.dtype)
```

```python
import functools

import numpy as np
import jax
import jax.numpy as jnp
from jax import lax
from jax.experimental import pallas as pl
from jax.experimental.pallas import tpu as pltpu

F32 = jnp.float32
BF16 = jnp.bfloat16

D = 1024
SEQ = 2048
CTX = 256
GRID_W = 64
HD = 64
EPS = 1e-6
ROPE_BASE = 10000.0
NEG = -1e30

NA_HEADS = 8
NA_ROWS = 8
NA_COLS = 16
NA_QROWS = 4
NA_KROWS = 12
SSM_HEADS = 16
SSM_INNER = 1024
SSM_STATE = 128
SSM_CONV = 5
SSM_CONV_CH = 1536
CHUNK = 128
SWA_HEADS = 8
SWA_WINDOW = 128
SWA_QT = 512
SWA_SUB = 128
SWA_KT = SWA_SUB + 2 * SWA_WINDOW
RET_HEADS = 8
D_FF = 2816
N_EXPERTS = 8
D_FF_EXPERT = 3584

EV_Q_BLK, EV_K_BLK, EV_V_BLK = 12, 16, 20
EV_Z_BLK = 3

LANES = 128
VMEM_LIMIT = 56 * 1024 * 1024


def _cp(sem, vmem=VMEM_LIMIT):
    return pltpu.CompilerParams(dimension_semantics=sem, vmem_limit_bytes=vmem)


def _sigmoid(x):
    return 1.0 / (1.0 + jnp.exp(-x))


def _silu(x):
    return x * _sigmoid(x)


def _mod_row(start, nctx):
    return jnp.where(start < nctx, 0, 1 + (start - nctx) // SEQ)


def _mod_kernel(c_ref, w_ref, b_ref, o_ref):
    c = c_ref[...]
    h = _silu(c).astype(BF16)
    o_ref[0] = jnp.dot(h, w_ref[0].astype(BF16), preferred_element_type=F32) + b_ref[0]


def _modulation(cin, ada_w, ada_b):
    depth, _, n6 = ada_w.shape
    rp = cin.shape[0]
    tn = 1024
    return pl.pallas_call(
        _mod_kernel,
        out_shape=jax.ShapeDtypeStruct((depth, rp, n6), F32),
        grid=(depth, n6 // tn),
        in_specs=[
            pl.BlockSpec((rp, D), lambda l, j: (0, 0)),
            pl.BlockSpec((1, D, tn), lambda l, j: (l, 0, j)),
            pl.BlockSpec((1, 1, tn), lambda l, j: (l, 0, j)),
        ],
        out_specs=pl.BlockSpec((1, rp, tn), lambda l, j: (l, 0, j)),
        compiler_params=_cp(("arbitrary", "arbitrary")),
        name="adaln_mod",
    )(cin, ada_w, ada_b.reshape(depth, 1, n6))


def _norm_mod(x, g, m, k):
    ms = jnp.mean(x * x, axis=-1, keepdims=True)
    y = x * lax.rsqrt(ms + EPS) * g
    return y * (1.0 + m[k + 1:k + 2]) + m[k:k + 1]


def _swap32(r):
    lane = lax.broadcasted_iota(jnp.int32, r.shape, 1)
    return jnp.where((lane % 64) < 32, pltpu.roll(r, 96, 1), pltpu.roll(r, 32, 1))


def _proj_kernel(x_ref, *rest, rope, n_ctx_tiles, side):
    rest = list(rest)
    xc_ref = rest.pop(0) if n_ctx_tiles else None
    g_ref, mod_ref, w_ref = rest[:3]
    rest = rest[3:]
    ws_ref = rest.pop(0) if side else None
    if rope:
        c_ref, s_ref = rest[:2]
        rest = rest[2:]
    o_ref = rest.pop(0)
    os_ref = rest.pop(0) if side else None
    h_ref, = rest

    @pl.when(pl.program_id(1) == 0)
    def _():
        x = x_ref[...]
        if n_ctx_tiles:
            x = jnp.where(pl.program_id(0) < n_ctx_tiles, xc_ref[...], x)
        h = _norm_mod(x, g_ref[...], mod_ref[0], 0).astype(BF16)
        h_ref[...] = h
        if side:
            os_ref[...] = jnp.dot(h, ws_ref[...], preferred_element_type=F32)

    r = jnp.dot(h_ref[...], w_ref[...], preferred_element_type=F32)
    if rope:
        c = c_ref[...]
        s = s_ref[...]
        for k in range(r.shape[1] // LANES):
            rk = r[:, k * LANES:(k + 1) * LANES]
            o_ref[:, k * LANES:(k + 1) * LANES] = (rk * c + _swap32(rk) * s).astype(o_ref.dtype)
    else:
        o_ref[...] = r.astype(o_ref.dtype)


def _proj(x, g, mod, w, out_dtype, nctx, tm, tn, rope=None, x_ctx=None, w_side=None, name="proj"):
    n = w.shape[1]
    nct = nctx // tm
    if x_ctx is None:
        rows = x.shape[0]
        in_specs = [pl.BlockSpec((tm, D), lambda i, j: (i, 0))]
        args = [x]
    else:
        rows = x.shape[0] + x_ctx.shape[0]
        in_specs = [pl.BlockSpec((tm, D), lambda i, j: (jnp.maximum(i - nct, 0), 0)),
                    pl.BlockSpec((tm, D), lambda i, j: (jnp.minimum(i, nct - 1), 0))]
        args = [x, x_ctx]
    in_specs += [
        pl.BlockSpec((1, D), lambda i, j: (0, 0)),
        pl.BlockSpec((1, 6, D), lambda i, j: (_mod_row(i * tm, nctx), 0, 0)),
        pl.BlockSpec((D, tn), lambda i, j: (0, j)),
    ]
    args += [g.reshape(1, D), mod, w]
    out_shape = jax.ShapeDtypeStruct((rows, n), out_dtype)
    out_specs = pl.BlockSpec((tm, tn), lambda i, j: (i, j))
    if w_side is not None:
        ns = w_side.shape[1]
        in_specs.append(pl.BlockSpec((D, ns), lambda i, j: (0, 0)))
        args.append(w_side)
        out_shape = (out_shape, jax.ShapeDtypeStruct((rows, ns), F32))
        out_specs = (out_specs, pl.BlockSpec((tm, ns), lambda i, j: (i, 0)))
    if rope is not None:
        per = SEQ // tm

        def tab_map(i, j):
            return (jnp.where(i < nct, 0, 1 + (i - nct) % per), 0)
        in_specs += [pl.BlockSpec((tm, LANES), tab_map), pl.BlockSpec((tm, LANES), tab_map)]
        args += list(rope)
    return pl.pallas_call(
        functools.partial(_proj_kernel, rope=rope is not None,
                          n_ctx_tiles=0 if x_ctx is None else nct, side=w_side is not None),
        out_shape=out_shape,
        grid=(rows // tm, n // tn),
        in_specs=in_specs,
        out_specs=out_specs,
        scratch_shapes=[pltpu.VMEM((tm, D), BF16)],
        compiler_params=_cp(("parallel", "arbitrary")),
        name=name,
    )(*args)


def _head_mask(x, a):
    lane = lax.broadcasted_iota(jnp.int32, (1, LANES), 1)
    keep = (lane < HD) if a == 0 else (lane >= HD)
    return jnp.where(keep, x, jnp.zeros_like(x))


def _nt(a, b):
    return lax.dot_general(a, b, (((1,), (1,)), ((), ())), preferred_element_type=F32)


def _two_part_attention(q, kc, vc, kl, vl, bias_fn, sink_fn):
    lane = lax.broadcasted_iota(jnp.int32, (1, LANES), 1)
    outs = []
    for a in range(2):
        qa = _head_mask(q, a) * jnp.asarray(HD ** -0.5, q.dtype)
        s_c = _nt(qa, kc)
        s_l = bias_fn(a, _nt(qa, kl))
        m = jnp.maximum(jnp.max(s_c, axis=-1, keepdims=True), jnp.max(s_l, axis=-1, keepdims=True))
        sink = sink_fn(a)
        if sink is not None:
            m = jnp.maximum(m, sink)
        p_c = jnp.exp(s_c - m)
        p_l = jnp.exp(s_l - m)
        den = jnp.sum(p_c, axis=-1, keepdims=True) + jnp.sum(p_l, axis=-1, keepdims=True)
        if sink is not None:
            den = den + jnp.exp(sink - m)
        o = (jnp.dot(p_c.astype(BF16), vc, preferred_element_type=F32)
             + jnp.dot(p_l.astype(BF16), vl, preferred_element_type=F32))
        outs.append(o / den)
    return jnp.where(lane < HD, outs[0], outs[1])


def _na_window_start(t):
    return jnp.clip(NA_QROWS * t - NA_ROWS // 2, 0, SEQ // GRID_W - NA_KROWS)


NA_NDROW = 2 * NA_ROWS - 1


def _na_bias_tile(toe_ref, a, t):
    lane = lax.broadcasted_iota(jnp.int32, (1, LANES), 1)
    nrows = SEQ // GRID_W
    w0r = _na_window_start(t)
    tile_rows = []
    for rq in range(NA_QROWS):
        r = NA_QROWS * t + rq
        r0 = jnp.clip(r - NA_ROWS // 2, 0, nrows - NA_ROWS)
        blocks = []
        for j in range(NA_KROWS // 2):
            kr = w0r + 2 * j
            e = jnp.clip(kr - r + NA_ROWS, 0, NA_NDROW)
            pen0 = jnp.where(jnp.logical_and(kr >= r0, kr < r0 + NA_ROWS), 0.0, NEG)
            pen1 = jnp.where(jnp.logical_and(kr + 1 >= r0, kr + 1 < r0 + NA_ROWS), 0.0, NEG)
            blocks.append(toe_ref[a, e] + jnp.where(lane < GRID_W, pen0, pen1))
        tile_rows.append(jnp.concatenate(blocks, axis=1))
    return jnp.concatenate(tile_rows, axis=0)


def _na_kernel(q_ref, kl_ref, vl_ref, kc_ref, vc_ref, toe_ref, o_ref):
    t = pl.program_id(1)
    w0 = pl.multiple_of(_na_window_start(t) * GRID_W, GRID_W)
    nk = NA_KROWS * GRID_W
    kl = kl_ref[pl.ds(w0, nk), :]
    vl = vl_ref[pl.ds(w0, nk), :]
    o = _two_part_attention(q_ref[...], kc_ref[...], vc_ref[...], kl, vl,
                            lambda a, s: s + _na_bias_tile(toe_ref, a, t), lambda a: None)
    o_ref[...] = o.astype(o_ref.dtype)


def _na_bias_kernel(rpb_ref, o_ref):
    c = lax.broadcasted_iota(jnp.int32, (GRID_W, LANES), 0)
    lane = lax.broadcasted_iota(jnp.int32, (GRID_W, LANES), 1)
    kc = lane % GRID_W
    c0 = jnp.clip(c - NA_COLS // 2, 0, GRID_W - NA_COLS)
    col_ok = jnp.logical_and(kc >= c0, kc < c0 + NA_COLS)
    neg = jnp.full((GRID_W, LANES), NEG, F32)

    def toeplitz(d, shift):
        v = jnp.broadcast_to(rpb_ref[0, d:d + 1, :], (GRID_W, LANES))
        return pltpu.roll(v, shift % LANES, 1, stride=1, stride_axis=0)

    for e in range(NA_NDROW + 1):
        lo = toeplitz(e - 1, -(NA_COLS - 1)) if e >= 1 else neg
        hi = toeplitz(e, GRID_W - (NA_COLS - 1)) if e < NA_NDROW else neg
        o_ref[0, e] = jnp.where(col_ok, jnp.where(lane < GRID_W, lo, hi), NEG)


def _na_bias_blocks(rpb):
    nd, ncol = rpb.shape[1], rpb.shape[2]
    padded = jnp.pad(rpb.astype(F32), ((0, 0), (0, NA_NDROW + 1 - nd), (0, LANES - ncol)))
    return pl.pallas_call(
        _na_bias_kernel,
        out_shape=jax.ShapeDtypeStruct((NA_HEADS, NA_NDROW + 1, GRID_W, LANES), F32),
        grid=(NA_HEADS,),
        in_specs=[pl.BlockSpec((1, NA_NDROW + 1, LANES), lambda h: (h, 0, 0))],
        out_specs=pl.BlockSpec((1, NA_NDROW + 1, GRID_W, LANES), lambda h: (h, 0, 0, 0)),
        compiler_params=_cp(("arbitrary",)),
        name="na_bias_blocks",
    )(padded)


def _na_attention(big, bias, nb, nctx):
    tq = NA_QROWS * GRID_W
    ntile = SEQ // tq
    nct_q = nctx // tq
    lat0 = nctx // SEQ
    return pl.pallas_call(
        _na_kernel,
        out_shape=jax.ShapeDtypeStruct((nb * SEQ, NA_HEADS * HD), BF16),
        grid=(NA_HEADS // 2, ntile, nb),
        in_specs=[
            pl.BlockSpec((tq, LANES), lambda h, t, b: (nct_q + b * ntile + t, EV_Q_BLK + h)),
            pl.BlockSpec((SEQ, LANES), lambda h, t, b: (lat0 + b, EV_K_BLK + h)),
            pl.BlockSpec((SEQ, LANES), lambda h, t, b: (lat0 + b, EV_V_BLK + h)),
            pl.BlockSpec((CTX, LANES), lambda h, t, b: (b, EV_K_BLK + h)),
            pl.BlockSpec((CTX, LANES), lambda h, t, b: (b, EV_V_BLK + h)),
            pl.BlockSpec((2, NA_NDROW + 1, GRID_W, LANES), lambda h, t, b: (h, 0, 0, 0)),
        ],
        out_specs=pl.BlockSpec((tq, LANES), lambda h, t, b: (b * ntile + t, h)),
        compiler_params=_cp(("arbitrary", "arbitrary", "arbitrary")),
        name="na_attention",
    )(big, big, big, big, big, bias)


def _ctx_attn_kernel(q_ref, k_ref, v_ref, o_ref):
    lane = lax.broadcasted_iota(jnp.int32, (1, LANES), 1)
    q = q_ref[...]
    k = k_ref[...]
    v = v_ref[...]
    outs = []
    for a in range(2):
        qa = _head_mask(q, a) * jnp.asarray(HD ** -0.5, q.dtype)
        s = _nt(qa, k)
        m = jnp.max(s, axis=-1, keepdims=True)
        p = jnp.exp(s - m)
        den = jnp.sum(p, axis=-1, keepdims=True)
        outs.append(jnp.dot(p.astype(BF16), v, preferred_element_type=F32) / den)
    o_ref[...] = jnp.where(lane < HD, outs[0], outs[1]).astype(o_ref.dtype)


def _ctx_attention(big, nctx):
    return pl.pallas_call(
        _ctx_attn_kernel,
        out_shape=jax.ShapeDtypeStruct((nctx, NA_HEADS * HD), BF16),
        grid=(nctx // CTX, NA_HEADS // 2),
        in_specs=[
            pl.BlockSpec((CTX, LANES), lambda b, h: (b, EV_Q_BLK + h)),
            pl.BlockSpec((CTX, LANES), lambda b, h: (b, EV_K_BLK + h)),
            pl.BlockSpec((CTX, LANES), lambda b, h: (b, EV_V_BLK + h)),
        ],
        out_specs=pl.BlockSpec((CTX, LANES), lambda b, h: (b, h)),
        compiler_params=_cp(("arbitrary", "arbitrary")),
        name="ctx_attention",
    )(big, big, big)


def _swa_kernel(q_ref, kl_ref, vl_ref, kc_ref, vc_ref, sink_ref, o_ref):
    h2 = pl.program_id(1)
    n = pl.program_id(2)
    q = q_ref[...]
    kc = kc_ref[...]
    vc = vc_ref[...]
    lane = lax.broadcasted_iota(jnp.int32, (1, LANES), 1)
    nsub = SWA_QT // SWA_SUB
    rows = [slice(t * SWA_SUB, (t + 1) * SWA_SUB) for t in range(nsub)]

    starts, oks = [], []
    for t in range(nsub):
        q0 = SWA_QT * n + SWA_SUB * t
        start = pl.multiple_of(jnp.clip(q0 - SWA_WINDOW, 0, SEQ - SWA_KT), SWA_WINDOW)
        qpos = q0 + lax.broadcasted_iota(jnp.int32, (SWA_SUB, 1), 0)
        kpos = start + lax.broadcasted_iota(jnp.int32, (1, SWA_KT), 1)
        starts.append(start)
        oks.append(jnp.abs(kpos - qpos) <= SWA_WINDOW)

    s_ctx, s_loc = [], []
    for a in range(2):
        qa = _head_mask(q, a) * jnp.asarray(HD ** -0.5, q.dtype)
        s_ctx.append(_nt(qa, kc))
        s_loc.append([_nt(qa[rows[t]], kl_ref[pl.ds(starts[t], SWA_KT), :]) for t in range(nsub)])

    p_ctx, p_loc, dens = [], [], []
    for a in range(2):
        sink = sink_ref[pl.ds(2 * h2 + a, 1), :][:, :1]
        pc_a, pl_a, den_a = [], [], []
        for t in range(nsub):
            s_c = s_ctx[a][rows[t]]
            s_l = jnp.where(oks[t], s_loc[a][t], NEG)
            m = jnp.maximum(jnp.maximum(jnp.max(s_c, axis=-1, keepdims=True),
                                        jnp.max(s_l, axis=-1, keepdims=True)), sink)
            p_c = jnp.exp(s_c - m)
            p_l = jnp.exp(s_l - m)
            den_a.append(jnp.sum(p_c, axis=-1, keepdims=True)
                         + jnp.sum(p_l, axis=-1, keepdims=True) + jnp.exp(sink - m))
            pc_a.append(p_c.astype(BF16))
            pl_a.append(p_l.astype(BF16))
        p_ctx.append(jnp.concatenate(pc_a, axis=0))
        p_loc.append(pl_a)
        dens.append(jnp.concatenate(den_a, axis=0))

    outs = []
    for a in range(2):
        o = jnp.dot(p_ctx[a], vc, preferred_element_type=F32)
        o_loc = [jnp.dot(p_loc[a][t], vl_ref[pl.ds(starts[t], SWA_KT), :],
                         preferred_element_type=F32) for t in range(nsub)]
        outs.append((o + jnp.concatenate(o_loc, axis=0)) / dens[a])
    o_ref[...] = jnp.where(lane < HD, outs[0], outs[1]).astype(o_ref.dtype)


def _swa_attention(rp, pp, sink, nb, nctx):
    nq = SEQ // SWA_QT
    nct_q = nctx // SWA_QT
    lat0 = nctx // SEQ
    sinkv = jnp.broadcast_to(sink.astype(F32)[:, None], (SWA_HEADS, LANES))
    return pl.pallas_call(
        _swa_kernel,
        out_shape=jax.ShapeDtypeStruct((nb * SEQ, SWA_HEADS * HD), BF16),
        grid=(nb, SWA_HEADS // 2, nq),
        in_specs=[
            pl.BlockSpec((SWA_QT, LANES), lambda b, h, n: (nct_q + b * nq + n, h)),
            pl.BlockSpec((SEQ, LANES), lambda b, h, n: (lat0 + b, 12 + h // 2)),
            pl.BlockSpec((SEQ, LANES), lambda b, h, n: (lat0 + b, 16 + h // 2)),
            pl.BlockSpec((CTX, LANES), lambda b, h, n: (b, 12 + h // 2)),
            pl.BlockSpec((CTX, LANES), lambda b, h, n: (b, 16 + h // 2)),
            pl.BlockSpec((SWA_HEADS, LANES), lambda b, h, n: (0, 0)),
        ],
        out_specs=pl.BlockSpec((SWA_QT, LANES), lambda b, h, n: (b * nq + n, h)),
        compiler_params=_cp(("arbitrary", "arbitrary", "arbitrary")),
        name="swa_attention",
    )(rp, rp, pp, rp, pp, sinkv)


CONV_PAD = (SSM_CONV - 1) // 2
CONV_TAPS = tuple(s for s in range(-CONV_PAD, CONV_PAD + 1) if s != 0)
CONV_HALO = 16
CONV_EDGE = 8


def _conv_shift_tables(tb):
    nt = len(CONV_TAPS)
    shift = np.zeros((nt * tb, tb), np.float32)
    edge = np.zeros((nt * 2 * CONV_EDGE, 2 * CONV_HALO), np.float32)
    for t, s in enumerate(CONV_TAPS):
        for i in range(tb):
            j = i + s
            if 0 <= j < tb:
                shift[t * tb + i, j] = 1.0
            elif j < 0:
                edge[t * 2 * CONV_EDGE + i, CONV_HALO + j] = 1.0
            else:
                edge[t * 2 * CONV_EDGE + CONV_EDGE + i - (tb - CONV_EDGE), CONV_HALO + j - tb] = 1.0
    return jnp.asarray(shift, BF16), jnp.asarray(edge, BF16)


def _conv_kernel(xp_ref, x_ref, xn_ref, sh_ref, ed_ref, w_ref, b_ref, o_ref, *, nctx_blocks,
                 per_seq):
    i = pl.program_id(0)
    tb = x_ref.shape[0]
    k = i - nctx_blocks
    has_prev = jnp.logical_and(i >= nctx_blocks, k % per_seq != 0)
    has_next = jnp.logical_and(i >= nctx_blocks, k % per_seq != per_seq - 1)
    x = x_ref[...]
    zero = jnp.zeros((CONV_HALO, x.shape[1]), x.dtype)
    halo = jnp.concatenate([jnp.where(has_prev, xp_ref[...], zero),
                            jnp.where(has_next, xn_ref[...], zero)], axis=0)
    taps = jnp.dot(sh_ref[...], x, preferred_element_type=F32)
    edges = jnp.dot(ed_ref[...], halo, preferred_element_type=F32)
    acc = x.astype(F32) * w_ref[CONV_PAD:CONV_PAD + 1, :] + b_ref[...]
    for t, s in enumerate(CONV_TAPS):
        tap = taps[t * tb:(t + 1) * tb]
        e0 = 2 * CONV_EDGE * t
        tap = jnp.concatenate([tap[:CONV_EDGE] + edges[e0:e0 + CONV_EDGE],
                               tap[CONV_EDGE:tb - CONV_EDGE],
                               tap[tb - CONV_EDGE:] + edges[e0 + CONV_EDGE:e0 + 2 * CONV_EDGE]], axis=0)
        acc = acc + tap * w_ref[CONV_PAD + s:CONV_PAD + s + 1, :]
    o_ref[...] = _silu(acc).astype(o_ref.dtype)


def _conv_silu(big, conv_w, conv_b, nctx):
    rows = big.shape[0]
    tb = CTX
    tc = SSM_CONV_CH
    nblk = rows // tb
    hpb = tb // CONV_HALO
    shift, edge = _conv_shift_tables(tb)
    return pl.pallas_call(
        functools.partial(_conv_kernel, nctx_blocks=nctx // tb, per_seq=SEQ // tb),
        out_shape=jax.ShapeDtypeStruct((rows, SSM_CONV_CH), BF16),
        grid=(nblk, SSM_CONV_CH // tc),
        in_specs=[
            pl.BlockSpec((CONV_HALO, tc), lambda i, c: (jnp.maximum(i * hpb - 1, 0), c)),
            pl.BlockSpec((tb, tc), lambda i, c: (i, c)),
            pl.BlockSpec((CONV_HALO, tc),
                         lambda i, c: (jnp.minimum((i + 1) * hpb, nblk * hpb - 1), c)),
            pl.BlockSpec(shift.shape, lambda i, c: (0, 0)),
            pl.BlockSpec(edge.shape, lambda i, c: (0, 0)),
            pl.BlockSpec((SSM_CONV, tc), lambda i, c: (0, c)),
            pl.BlockSpec((1, tc), lambda i, c: (0, c)),
        ],
        out_specs=pl.BlockSpec((tb, tc), lambda i, c: (i, c)),
        compiler_params=_cp(("arbitrary", "arbitrary")),
        name="conv_silu",
    )(big, big, big, shift, edge, conv_w.reshape(SSM_CONV, SSM_CONV_CH),
      conv_b.reshape(1, SSM_CONV_CH))


NCC = CTX // CHUNK
NLC = SEQ // CHUNK


def _chunk_block(b, s, d, nctx):
    cc = s if d == 0 else NCC - 1 - s
    lc = s - NCC if d == 0 else NLC + NCC - 1 - s
    return jnp.where(s < NCC, NCC * b + cc, nctx // CHUNK + NLC * b + lc)


def _latent_chunk_block(b, s, d):
    lc = s - NCC if d == 0 else NLC + NCC - 1 - s
    return NLC * b + jnp.clip(lc, 0, NLC - 1)


def _split3(a):
    hi = a.astype(BF16)
    r1 = a - hi.astype(F32)
    mid = r1.astype(BF16)
    lo = (r1 - mid.astype(F32)).astype(BF16)
    return hi, mid, lo


def _softplus(x):
    return jnp.maximum(x, 0.0) + jnp.log(1.0 + jnp.exp(-jnp.abs(x)))


def _ssd_direction(d, x_ref, b_ref, c_ref, dt_ref, dtt_ref, bias_ref, biast_ref, alog_ref,
                   alogt_ref, y_ref, s_ref):
    q = CHUNK
    ii = lax.broadcasted_iota(jnp.int32, (q, q), 0)
    jj = lax.broadcasted_iota(jnp.int32, (q, q), 1)
    causal = (jj <= ii) if d == 0 else (jj >= ii)
    tri = jnp.where(causal, 1.0, 0.0).astype(BF16)
    trit = jnp.where((ii <= jj) if d == 0 else (ii >= jj), 1.0, 0.0).astype(BF16)

    dt = _softplus(dt_ref[0] + bias_ref[d])
    a = dt * (-jnp.exp(alog_ref[d]))
    dtt = _softplus(dtt_ref[0] + biast_ref[d])
    at = dtt * (-jnp.exp(alogt_ref[d]))
    acum = sum(jnp.dot(tri, p, preferred_element_type=F32) for p in _split3(a))
    acumt = sum(jnp.dot(p, trit, preferred_element_type=F32) for p in _split3(at))
    atot = jnp.sum(at, axis=1, keepdims=True)
    wrow = dtt * jnp.exp(atot - acumt)
    keep = jnp.exp(jnp.broadcast_to(atot, (SSM_HEADS, LANES)))

    first = lax.broadcasted_iota(jnp.int32, (1, LANES), 1) < HD
    ngroups = b_ref.shape[1] // SSM_STATE
    pairs_per_group = SSM_HEADS // 2 // ngroups
    for g in range(ngroups):
        bg = b_ref[:, g * SSM_STATE:(g + 1) * SSM_STATE]
        cg = c_ref[:, g * SSM_STATE:(g + 1) * SSM_STATE]
        cb = _nt(cg, bg)
        cgf = cg.astype(F32)
        bt = bg.astype(F32).T
        for pp in range(pairs_per_group):
            p = g * pairs_per_group + pp
            x = x_ref[:, p * LANES:(p + 1) * LANES]
            st = s_ref[d, p]
            rhs = jnp.concatenate([x, st.astype(BF16)], axis=0)
            ys, us = [], []
            for h in (2 * p, 2 * p + 1):
                col = jnp.broadcast_to(acum[:, h:h + 1], (q, LANES))
                dec = jnp.exp(jnp.where(causal, col - acumt[h:h + 1, :], NEG)) * (cb * dtt[h:h + 1, :])
                lhs = jnp.concatenate([dec, jnp.exp(col) * cgf], axis=1).astype(BF16)
                ys.append(jnp.dot(lhs, rhs, preferred_element_type=F32))
                us.append(jnp.dot((bt * wrow[h:h + 1, :]).astype(BF16), x,
                                  preferred_element_type=F32))
            y_ref[:, p * LANES:(p + 1) * LANES] = jnp.where(first, ys[0], ys[1]).astype(y_ref.dtype)
            keep_p = jnp.where(first, keep[2 * p:2 * p + 1, :], keep[2 * p + 1:2 * p + 2, :])
            s_ref[d, p] = st * keep_p + jnp.where(first, us[0], us[1])


def _ssd_kernel(*refs):
    fwd, bwd, (bias_ref, biast_ref, alog_ref, alogt_ref), (yf_ref, yb_ref, s_ref) = (
        refs[0:5], refs[5:10], refs[10:14], refs[14:17])
    assert CHUNK == LANES

    @pl.when(pl.program_id(1) == 0)
    def _():
        s_ref[...] = jnp.zeros_like(s_ref)

    for d, ins, y_ref in ((0, fwd, yf_ref), (1, bwd, yb_ref)):
        _ssd_direction(d, *ins, bias_ref, biast_ref, alog_ref, alogt_ref, y_ref, s_ref)


def _ssd_scan(xbc, dt_raw, dt_bias, a_log, nb, nctx):
    rows = xbc.shape[0]
    dtd = dt_raw.reshape(rows, 2, SSM_HEADS).transpose(1, 0, 2)
    dtt = dtd.transpose(0, 2, 1)

    def direction_specs(d):
        blk = functools.partial(_chunk_block, d=d, nctx=nctx)
        return [
            pl.BlockSpec((CHUNK, SSM_INNER), lambda b, s: (blk(b, s), 0)),
            pl.BlockSpec((CHUNK, 2 * SSM_STATE), lambda b, s: (blk(b, s), 4)),
            pl.BlockSpec((CHUNK, 2 * SSM_STATE), lambda b, s: (blk(b, s), 5)),
            pl.BlockSpec((1, CHUNK, SSM_HEADS), lambda b, s: (d, blk(b, s), 0)),
            pl.BlockSpec((1, SSM_HEADS, CHUNK), lambda b, s: (d, 0, blk(b, s))),
        ]

    whole = lambda shape: pl.BlockSpec(shape, lambda b, s: (0, 0, 0))
    out_spec = lambda d: pl.BlockSpec(
        (CHUNK, SSM_INNER), lambda b, s: (_chunk_block(b, s, d, nctx), 0))
    y_shape = jax.ShapeDtypeStruct((rows, SSM_INNER), BF16)
    return pl.pallas_call(
        _ssd_kernel,
        out_shape=(y_shape, y_shape),
        grid=(nb, NCC + NLC),
        in_specs=direction_specs(0) + direction_specs(1) + [
            whole((2, 1, SSM_HEADS)), whole((2, SSM_HEADS, 1)),
            whole((2, 1, SSM_HEADS)), whole((2, SSM_HEADS, 1)),
        ],
        out_specs=(out_spec(0), out_spec(1)),
        scratch_shapes=[pltpu.VMEM((2, SSM_HEADS // 2, SSM_STATE, LANES), F32)],
        compiler_params=_cp(("arbitrary", "arbitrary")),
        name="ssd_scan",
    )(*([xbc, xbc, xbc, dtd, dtt] * 2),
      dt_bias.reshape(2, 1, SSM_HEADS), dt_bias.reshape(2, SSM_HEADS, 1),
      a_log.reshape(2, 1, SSM_HEADS), a_log.reshape(2, SSM_HEADS, 1))


def _ret_geometry(d):
    q = CHUNK
    ii = lax.broadcasted_iota(jnp.int32, (q, q), 0)
    jj = lax.broadcasted_iota(jnp.int32, (q, q), 1)
    dist = (ii - jj) if d == 0 else (jj - ii)
    ri = lax.broadcasted_iota(jnp.int32, (q, LANES), 0)
    cj = lax.broadcasted_iota(jnp.int32, (1, q), 1)
    steps_in = ((ri + 1) if d == 0 else (q - ri)).astype(F32)
    steps_out = ((q - 1 - cj) if d == 0 else cj).astype(F32)
    return dist >= 0, dist.astype(F32), steps_in, steps_out


def _ret_kernel(*refs):
    fwd, bwd, ld_ref, (yf_ref, yb_ref, s_ref) = refs[0:3], refs[3:6], refs[6], refs[7:10]
    q = CHUNK
    step = pl.program_id(1)

    @pl.when(step == 0)
    def _():
        s_ref[...] = jnp.zeros_like(s_ref)

    scale = jnp.asarray(HD ** -0.5, BF16)
    dirs = ((0, fwd, yf_ref), (1, bwd, yb_ref))

    def log_gamma(d, h):
        lg = jnp.log(1.0 - jnp.exp(ld_ref[d, h]))
        return jnp.broadcast_to(lg[None], (q // 8, 8, LANES)).reshape(q, LANES)

    qk = {}
    for d, (c_ref, b_ref, x_ref), y_ref in dirs:
        for h in range(RET_HEADS):
            pb, a = h // 2, h % 2
            ch = _head_mask(c_ref[:, pb * LANES:(pb + 1) * LANES], a) * scale
            bh = _head_mask(b_ref[:, pb * LANES:(pb + 1) * LANES], a)
            qk[d, h] = (ch, _nt(ch, bh))

    for d, (c_ref, b_ref, x_ref), y_ref in dirs:
        causal, distf, steps_in, _ = _ret_geometry(d)
        for h in range(RET_HEADS):
            lg = log_gamma(d, h)
            ch, cb = qk[d, h]
            dec = jnp.exp(jnp.where(causal, distf * lg, NEG)) * cb
            grow = jnp.exp(steps_in * lg) * ch.astype(F32)
            lhs = jnp.concatenate([dec, grow], axis=1).astype(BF16)
            x = x_ref[:, h * LANES:(h + 1) * LANES]
            rhs = jnp.concatenate([x, s_ref[d, h].astype(BF16)], axis=0)
            y_ref[:, h * LANES:(h + 1) * LANES] = jnp.dot(
                lhs, rhs, preferred_element_type=F32).astype(y_ref.dtype)

    for d, (c_ref, b_ref, x_ref), y_ref in dirs:
        _, _, _, steps_out = _ret_geometry(d)
        for h in range(RET_HEADS):
            lg = log_gamma(d, h)
            pb, a = h // 2, h % 2
            bt = _head_mask(b_ref[:, pb * LANES:(pb + 1) * LANES], a).astype(F32).T
            bth = (bt * jnp.exp(steps_out * lg[0:1, :])).astype(BF16)
            x = x_ref[:, h * LANES:(h + 1) * LANES]
            s_ref[d, h] = s_ref[d, h] * jnp.exp(q * lg) + jnp.dot(
                bth, x, preferred_element_type=F32)


def _retention_scan(rp, pp, log_decay, nb, nctx):
    w = RET_HEADS * HD
    wv = RET_HEADS * LANES

    def direction_specs(d):
        blk = functools.partial(_chunk_block, d=d, nctx=nctx)
        return [
            pl.BlockSpec((CHUNK, w), lambda b, s: (blk(b, s), 1)),
            pl.BlockSpec((CHUNK, w), lambda b, s: (blk(b, s), 2)),
            pl.BlockSpec((CHUNK, wv), lambda b, s: (blk(b, s), 0)),
        ]

    out_spec = lambda d: pl.BlockSpec((CHUNK, wv), lambda b, s: (_latent_chunk_block(b, s, d), 0))
    y_shape = jax.ShapeDtypeStruct((nb * SEQ, wv), BF16)
    return pl.pallas_call(
        _ret_kernel,
        out_shape=(y_shape, y_shape),
        grid=(nb, NCC + NLC),
        in_specs=direction_specs(0) + direction_specs(1) + [
            pl.BlockSpec((2, RET_HEADS, 8, LANES), lambda b, s: (0, 0, 0, 0))],
        out_specs=(out_spec(0), out_spec(1)),
        scratch_shapes=[pltpu.VMEM((2, RET_HEADS, LANES, LANES), F32)],
        compiler_params=_cp(("arbitrary", "arbitrary")),
        name="retention_scan",
    )(*([rp, rp, pp] * 2),
      jnp.broadcast_to(log_decay.astype(F32)[:, :, None, None], (2, RET_HEADS, 8, LANES)))


def _ssm_gate_norm(yf_ref, yb_ref, xs_ref, z_ref, dsk_ref, g_ref):
    y = yf_ref[...].astype(F32) + yb_ref[...].astype(F32) + xs_ref[...].astype(F32) * dsk_ref[...]
    yz = y * _silu(z_ref[...].astype(F32))
    half = SSM_INNER // 2
    segs = []
    for g in range(2):
        seg = yz[:, g * half:(g + 1) * half]
        ms = jnp.mean(seg * seg, axis=-1, keepdims=True)
        segs.append((seg * lax.rsqrt(ms + EPS) * g_ref[:, g * half:(g + 1) * half]).astype(BF16))
    return jnp.concatenate(segs, axis=1)


def _ret_gate_norm(rf_ref, rb_ref, rg_ref, g_ref, b_ref):
    y = rf_ref[...].astype(F32) + rb_ref[...].astype(F32)
    segs = []
    for h in range(RET_HEADS):
        sl = slice(h * LANES, (h + 1) * LANES)
        seg = y[:, sl]
        mu = jnp.mean(seg, axis=-1, keepdims=True)
        cen = seg - mu
        var = jnp.mean(cen * cen, axis=-1, keepdims=True)
        yn = cen * lax.rsqrt(var + EPS) * g_ref[:, sl] + b_ref[:, sl]
        segs.append((_silu(rg_ref[:, sl].astype(F32)) * yn).astype(BF16))
    return jnp.concatenate(segs, axis=1)


def _outproj_kernel(*refs, n_ctx_tiles, split_x, split_a1, mixer_fn):
    refs = list(refs)
    x_ref = refs.pop(0)
    xc_ref = refs.pop(0) if split_x else None
    a1_ref = refs.pop(0)
    a1c_ref = refs.pop(0) if split_a1 else None
    w1_ref, w2_ref, mod_ref, o_ref = refs[-4:]
    is_ctx = pl.program_id(0) < n_ctx_tiles
    a1 = jnp.where(is_ctx, a1c_ref[...], a1_ref[...]) if split_a1 else a1_ref[...]
    x = jnp.where(is_ctx, xc_ref[...], x_ref[...]) if split_x else x_ref[...]
    a2 = mixer_fn(*refs[:-4])
    y = (jnp.dot(a1, w1_ref[...], preferred_element_type=F32)
         + jnp.dot(a2, w2_ref[...], preferred_element_type=F32))
    o_ref[...] = x + mod_ref[0][2:3] * y


def _outproj(x, a1, mixer_fn, mixer_args, mixer_specs, w_out, mod, nctx, tm, rows, x_off=0,
             a1_ctx=None, x_ctx=None):
    k1 = a1.shape[1]
    k2 = w_out.shape[0] - k1
    w1 = w_out[:k1].astype(BF16)
    w2 = w_out[k1:].astype(BF16)
    nct = nctx // tm if (a1_ctx is not None or x_ctx is not None) else 0
    lat = lambda i: (jnp.maximum(i - nct, 0), 0)
    ctx_ = lambda i: (jnp.minimum(i, nct - 1), 0)
    in_specs, args = [], []
    if x_ctx is None:
        in_specs.append(pl.BlockSpec((tm, D), lambda i: (i + x_off, 0)))
        args.append(x)
    else:
        in_specs += [pl.BlockSpec((tm, D), lat), pl.BlockSpec((tm, D), ctx_)]
        args += [x, x_ctx]
    in_specs.append(pl.BlockSpec((tm, k1), lat))
    args.append(a1)
    if a1_ctx is not None:
        in_specs.append(pl.BlockSpec((tm, k1), ctx_))
        args.append(a1_ctx)
    in_specs += list(mixer_specs) + [
        pl.BlockSpec((k1, D), lambda i: (0, 0)),
        pl.BlockSpec((k2, D), lambda i: (0, 0)),
        pl.BlockSpec((1, 6, D), lambda i: (_mod_row((i + x_off) * tm, nctx), 0, 0)),
    ]
    args += list(mixer_args) + [w1, w2, mod]
    return pl.pallas_call(
        functools.partial(_outproj_kernel, n_ctx_tiles=nct, split_x=x_ctx is not None,
                          split_a1=a1_ctx is not None, mixer_fn=mixer_fn),
        out_shape=jax.ShapeDtypeStruct((rows, D), F32),
        grid=(rows // tm,),
        in_specs=in_specs,
        out_specs=pl.BlockSpec((tm, D), lambda i: (i, 0)),
        compiler_params=_cp(("parallel",)),
        name="outproj",
    )(*args)


def _top2_gates(logits):
    lane = lax.broadcasted_iota(jnp.int32, logits.shape, 1)
    valid = lane < N_EXPERTS
    l0 = jnp.where(valid, logits, NEG)
    m1 = jnp.max(l0, axis=-1, keepdims=True)
    i1 = jnp.min(jnp.where(l0 == m1, lane, LANES), axis=-1, keepdims=True)
    l1 = jnp.where(lane == i1, NEG, l0)
    m2 = jnp.max(l1, axis=-1, keepdims=True)
    i2 = jnp.min(jnp.where(l1 == m2, lane, LANES), axis=-1, keepdims=True)
    e2 = jnp.exp(m2 - m1)
    g1 = 1.0 / (1.0 + e2)
    g2 = e2 / (1.0 + e2)
    return jnp.where(lane == i1, g1, 0.0) + jnp.where(lane == i2, g2, 0.0)


def _ffn_kernel(x_ref, g_ref, mod_ref, *rest, routed, final, sub):
    rest = list(rest)
    router_ref = rest.pop(0) if routed else None
    w1_ref, w3_ref, w2_ref = rest[:3]
    rest = rest[3:]
    fg_ref = rest.pop(0) if final else None
    o_ref, h_ref, acc_ref = rest[:3]
    gate_ref = rest[3] if routed else None
    e = pl.program_id(1)
    f = pl.program_id(2)
    first = jnp.logical_and(e == 0, f == 0)
    last = jnp.logical_and(e == pl.num_programs(1) - 1, f == pl.num_programs(2) - 1)

    @pl.when(first)
    def _():
        h = _norm_mod(x_ref[...], g_ref[...], mod_ref[0], 3)
        h16 = h.astype(BF16)
        h_ref[...] = h16
        acc_ref[...] = jnp.zeros_like(acc_ref)
        if routed:
            hl = (h - h16.astype(F32)).astype(BF16)
            r = router_ref[...]
            rh = r.astype(BF16)
            rl = (r - rh.astype(F32)).astype(BF16)
            logits = (jnp.dot(h16, rh, preferred_element_type=F32)
                      + jnp.dot(h16, rl, preferred_element_type=F32)
                      + jnp.dot(hl, rh, preferred_element_type=F32))
            gate_ref[...] = _top2_gates(logits)

    h16 = h_ref[...]
    if routed:
        lane = lax.broadcasted_iota(jnp.int32, gate_ref.shape, 1)
        ge = jnp.sum(jnp.where(lane == e, gate_ref[...], 0.0), axis=-1, keepdims=True)
    for c in range(w1_ref.shape[2] // sub):
        cols = slice(c * sub, (c + 1) * sub)
        u = _silu(jnp.dot(h16, w1_ref[0, :, cols], preferred_element_type=F32)) * jnp.dot(
            h16, w3_ref[0, :, cols], preferred_element_type=F32)
        if routed:
            u = u * ge
        acc_ref[...] += jnp.dot(u.astype(BF16), w2_ref[0, cols, :], preferred_element_type=F32)

    @pl.when(last)
    def _():
        y = x_ref[...] + mod_ref[0][5:6] * acc_ref[...]
        if final:
            ms = jnp.mean(y * y, axis=-1, keepdims=True)
            y = y * lax.rsqrt(ms + EPS) * fg_ref[...]
        o_ref[...] = y


def _ffn(x, g, mod, w1, w3, w2, nctx, tm, tf, sub, x_off=0, rows=None, router=None,
         final_g=None):
    rows = x.shape[0] if rows is None else rows
    ne, _, dff = w1.shape
    routed = router is not None
    final = final_g is not None
    in_specs = [
        pl.BlockSpec((tm, D), lambda i, e, f: (i + x_off, 0)),
        pl.BlockSpec((1, D), lambda i, e, f: (0, 0)),
        pl.BlockSpec((1, 6, D), lambda i, e, f: (_mod_row((i + x_off) * tm, nctx), 0, 0)),
    ]
    args = [x, g.reshape(1, D), mod]
    if routed:
        in_specs.append(pl.BlockSpec((D, LANES), lambda i, e, f: (0, 0)))
        args.append(jnp.pad(router.astype(F32), ((0, 0), (0, LANES - ne))))
    in_specs += [
        pl.BlockSpec((1, D, tf), lambda i, e, f: (e, 0, f)),
        pl.BlockSpec((1, D, tf), lambda i, e, f: (e, 0, f)),
        pl.BlockSpec((1, tf, D), lambda i, e, f: (e, f, 0)),
    ]
    args += [w1, w3, w2]
    if final:
        in_specs.append(pl.BlockSpec((1, D), lambda i, e, f: (0, 0)))
        args.append(final_g.reshape(1, D))
    scratch = [pltpu.VMEM((tm, D), BF16), pltpu.VMEM((tm, D), F32)]
    if routed:
        scratch.append(pltpu.VMEM((tm, LANES), F32))
    return pl.pallas_call(
        functools.partial(_ffn_kernel, routed=routed, final=final, sub=sub),
        out_shape=jax.ShapeDtypeStruct((rows, D), F32),
        grid=(rows // tm, ne, dff // tf),
        in_specs=in_specs,
        out_specs=pl.BlockSpec((tm, D), lambda i, e, f: (i, 0)),
        scratch_shapes=scratch,
        compiler_params=_cp(("parallel", "arbitrary", "arbitrary")),
        name="moe_ffn" if routed else "dense_ffn",
    )(*args)


def _rope_tables(tm):
    t = np.arange(SEQ)
    n_freq = HD // 4
    inv = ROPE_BASE ** (-np.arange(n_freq, dtype=np.float64) / n_freq)
    ang = np.concatenate([(t // GRID_W)[:, None] * inv, (t % GRID_W)[:, None] * inv], axis=-1)
    cos = np.tile(np.cos(ang), (1, 4))
    sin = np.tile(np.concatenate([-np.sin(ang), np.sin(ang)], axis=-1), (1, 2))
    cos = np.concatenate([np.ones((tm, LANES)), cos], axis=0)
    sin = np.concatenate([np.zeros((tm, LANES)), sin], axis=0)
    return jnp.asarray(cos, F32), jnp.asarray(sin, F32)


def _dup_heads(w, nheads):
    w = w.reshape(D, nheads, 1, HD)
    return jnp.broadcast_to(w, (D, nheads, 2, HD)).reshape(D, nheads * 2 * HD)


def kernel(x, c, ctx, c_ctx, ada_w, ada_b, norm_attn_g, norm_ffn_g, ev_w_in, ev_w_out, na_rpb,
           ssm_conv_w, ssm_conv_b, ssm_dt_bias, ssm_a_log, ssm_d, ssm_norm_g, ffn_w1, ffn_w3, ffn_w2,
           od_w_in, od_w_out, swa_sink, ret_log_decay, ret_gn_g, ret_gn_b, moe_router, moe_w1,
           moe_w3, moe_w2, final_g):
    nb = x.shape[0]
    nctx = -(-(nb * CTX) // SEQ) * SEQ
    tm = 1024
    tmo = tm // 2
    x_lat = x.reshape(nb * SEQ, D).astype(F32)
    x_ctx = jnp.pad(ctx.reshape(nb * CTX, D).astype(F32), ((0, nctx - nb * CTX), (0, 0)))

    rp = -(-(nb + 1) // 8) * 8
    cin = jnp.zeros((rp, D), F32).at[0].set(c_ctx).at[1:nb + 1].set(c)
    mod = _modulation(cin, ada_w, ada_b).reshape(2, rp, 6, D)

    w_in = ev_w_in[0]
    q_, k_, v_, z_, xbc_, dt_ = jnp.split(w_in, [512, 1024, 1536, 2560, 4096], axis=1)
    w_big = jnp.concatenate([xbc_, q_, k_, v_, z_], axis=1).astype(BF16)
    big, dt_raw = _proj(x_lat, norm_attn_g[0], mod[0], w_big, BF16, nctx, tm, 2048, x_ctx=x_ctx,
                        w_side=dt_.astype(BF16), name="even_in_proj")
    attn = _na_attention(big, _na_bias_blocks(na_rpb[0]), nb, nctx)
    attn_ctx = _ctx_attention(big, nctx)
    xbc = _conv_silu(big, ssm_conv_w[0], ssm_conv_b[0], nctx)
    yf, yb = _ssd_scan(xbc, dt_raw, ssm_dt_bias[0], ssm_a_log[0], nb, nctx)
    wide = lambda col: pl.BlockSpec((tmo, SSM_INNER), lambda i: (i, col))
    vec = pl.BlockSpec((1, SSM_INNER), lambda i: (0, 0))
    dsk = jnp.repeat(ssm_d[0].astype(F32), HD).reshape(1, SSM_INNER)
    xs = _outproj(x_lat, attn, _ssm_gate_norm,
                  [yf, yb, xbc, big, dsk, ssm_norm_g[0].reshape(1, SSM_INNER)],
                  [wide(0), wide(0), wide(0), wide(EV_Z_BLK), vec, vec],
                  ev_w_out[0], mod[0], nctx, tmo, nctx + nb * SEQ, a1_ctx=attn_ctx, x_ctx=x_ctx)
    xs = _ffn(xs, norm_ffn_g[0], mod[0], ffn_w1.astype(BF16), ffn_w3.astype(BF16),
              ffn_w2.astype(BF16), nctx, tm // 2, D_FF, 256)

    w_in = od_w_in[0]
    q_, k_, v_, rq_, rk_, rv_, rg_ = jnp.split(w_in, [512, 640, 768, 1280, 1792, 2816], axis=1)
    w_rope = jnp.concatenate([q_, rq_, rk_, _dup_heads(k_, 2)], axis=1).astype(BF16)
    w_plain = jnp.concatenate([rv_, rg_, _dup_heads(v_, 2)], axis=1).astype(BF16)
    rope_tabs = _rope_tables(tm)
    rpj = _proj(xs, norm_attn_g[1], mod[1], w_rope, BF16, nctx, tm, w_rope.shape[1],
                rope=rope_tabs, name="odd_rope_proj")
    ppj = _proj(xs, norm_attn_g[1], mod[1], w_plain, BF16, nctx, tm, w_plain.shape[1],
                name="odd_plain_proj")
    yw = _swa_attention(rpj, ppj, swa_sink[0], nb, nctx)
    rf, rb = _retention_scan(rpj, ppj, ret_log_decay[0], nb, nctx)
    x_off = nctx // tmo
    rg_spec = pl.BlockSpec((tmo, SSM_INNER), lambda i: (i + x_off, 1))
    xl = _outproj(xs, yw, _ret_gate_norm,
                  [rf, rb, ppj, ret_gn_g[0].reshape(1, SSM_INNER), ret_gn_b[0].reshape(1, SSM_INNER)],
                  [wide(0), wide(0), rg_spec, vec, vec],
                  od_w_out[0], mod[1], nctx, tmo, nb * SEQ, x_off=x_off)
    out = _ffn(xl, norm_ffn_g[1], mod[1], moe_w1[0].astype(BF16), moe_w3[0].astype(BF16),
               moe_w2[0].astype(BF16), 0, tm, D_FF_EXPERT // 2, 256, router=moe_router[0],
               final_g=final_g)
    return out.reshape(nb, SEQ, D).astype(x.dtype)
```

```python
import functools

import numpy as np
import jax
import jax.numpy as jnp
from jax import lax
from jax.experimental import pallas as pl
from jax.experimental.pallas import tpu as pltpu

F32 = jnp.float32
BF16 = jnp.bfloat16

D = 1024
SEQ = 2048
CTX = 256
GRID_W = 64
HD = 64
EPS = 1e-6
ROPE_BASE = 10000.0
NEG = -1e30

NA_HEADS = 8
NA_ROWS = 8
NA_COLS = 16
NA_QROWS = 8
NA_SUBROWS = 2
NA_KROWS = 10
SSM_HEADS = 16
SSM_INNER = 1024
SSM_STATE = 128
SSM_CONV = 5
SSM_CONV_CH = 1536
CHUNK = 128
SWA_HEADS = 8
SWA_WINDOW = 128
SWA_QT = 512
SWA_SUB = 128
SWA_KT = SWA_SUB + 2 * SWA_WINDOW
RET_HEADS = 8
D_FF = 2816
N_EXPERTS = 8
D_FF_EXPERT = 3584

EV_Q_BLK, EV_K_BLK, EV_V_BLK = 12, 16, 20
EV_Z_BLK = 3

LANES = 128
VMEM_LIMIT = 56 * 1024 * 1024


def _cp(sem, vmem=VMEM_LIMIT):
    return pltpu.CompilerParams(dimension_semantics=sem, vmem_limit_bytes=vmem)


def _sigmoid(x):
    return 1.0 / (1.0 + jnp.exp(-x))


def _silu(x):
    return x * _sigmoid(x)


def _mod_row(start, nctx):
    return jnp.where(start < nctx, 0, 1 + (start - nctx) // SEQ)


def _mod_kernel(c_ref, w_ref, b_ref, o_ref):
    c = c_ref[...]
    h = _silu(c).astype(BF16)
    o_ref[0] = jnp.dot(h, w_ref[0].astype(BF16), preferred_element_type=F32) + b_ref[0]


def _modulation(cin, ada_w, ada_b):
    depth, _, n6 = ada_w.shape
    rp = cin.shape[0]
    tn = 1024
    return pl.pallas_call(
        _mod_kernel,
        out_shape=jax.ShapeDtypeStruct((depth, rp, n6), F32),
        grid=(depth, n6 // tn),
        in_specs=[
            pl.BlockSpec((rp, D), lambda l, j: (0, 0)),
            pl.BlockSpec((1, D, tn), lambda l, j: (l, 0, j)),
            pl.BlockSpec((1, 1, tn), lambda l, j: (l, 0, j)),
        ],
        out_specs=pl.BlockSpec((1, rp, tn), lambda l, j: (l, 0, j)),
        compiler_params=_cp(("arbitrary", "arbitrary")),
        name="adaln_mod",
    )(cin, ada_w, ada_b.reshape(depth, 1, n6))


def _norm_mod(x, g, m, k):
    ms = jnp.mean(x * x, axis=-1, keepdims=True)
    y = x * lax.rsqrt(ms + EPS) * g
    return y * (1.0 + m[k + 1:k + 2]) + m[k:k + 1]


def _swap32(r):
    lane = lax.broadcasted_iota(jnp.int32, r.shape, 1)
    return jnp.where((lane % 64) < 32, pltpu.roll(r, 96, 1), pltpu.roll(r, 32, 1))


def _proj_kernel(x_ref, *rest, rope, n_ctx_tiles, side):
    rest = list(rest)
    xc_ref = rest.pop(0) if n_ctx_tiles else None
    g_ref, mod_ref, w_ref = rest[:3]
    rest = rest[3:]
    ws_ref = rest.pop(0) if side else None
    if rope:
        c_ref, s_ref = rest[:2]
        rest = rest[2:]
    o_ref = rest.pop(0)
    os_ref = rest.pop(0) if side else None
    h_ref, = rest

    @pl.when(pl.program_id(1) == 0)
    def _():
        x = x_ref[...]
        if n_ctx_tiles:
            x = jnp.where(pl.program_id(0) < n_ctx_tiles, xc_ref[...], x)
        h = _norm_mod(x, g_ref[...], mod_ref[0], 0).astype(BF16)
        h_ref[...] = h
        if side:
            os_ref[...] = jnp.dot(h, ws_ref[...], preferred_element_type=F32)

    r = jnp.dot(h_ref[...], w_ref[...], preferred_element_type=F32)
    if rope:
        c = c_ref[...]
        s = s_ref[...]
        for k in range(r.shape[1] // LANES):
            rk = r[:, k * LANES:(k + 1) * LANES]
            o_ref[:, k * LANES:(k + 1) * LANES] = (rk * c + _swap32(rk) * s).astype(o_ref.dtype)
    else:
        o_ref[...] = r.astype(o_ref.dtype)


def _proj(x, g, mod, w, out_dtype, nctx, tm, tn, rope=None, x_ctx=None, w_side=None, name="proj"):
    n = w.shape[1]
    nct = nctx // tm
    if x_ctx is None:
        rows = x.shape[0]
        in_specs = [pl.BlockSpec((tm, D), lambda i, j: (i, 0))]
        args = [x]
    else:
        rows = x.shape[0] + x_ctx.shape[0]
        in_specs = [pl.BlockSpec((tm, D), lambda i, j: (jnp.maximum(i - nct, 0), 0)),
                    pl.BlockSpec((tm, D), lambda i, j: (jnp.minimum(i, nct - 1), 0))]
        args = [x, x_ctx]
    in_specs += [
        pl.BlockSpec((1, D), lambda i, j: (0, 0)),
        pl.BlockSpec((1, 6, D), lambda i, j: (_mod_row(i * tm, nctx), 0, 0)),
        pl.BlockSpec((D, tn), lambda i, j: (0, j)),
    ]
    args += [g.reshape(1, D), mod, w]
    out_shape = jax.ShapeDtypeStruct((rows, n), out_dtype)
    out_specs = pl.BlockSpec((tm, tn), lambda i, j: (i, j))
    if w_side is not None:
        ns = w_side.shape[1]
        in_specs.append(pl.BlockSpec((D, ns), lambda i, j: (0, 0)))
        args.append(w_side)
        out_shape = (out_shape, jax.ShapeDtypeStruct((rows, ns), F32))
        out_specs = (out_specs, pl.BlockSpec((tm, ns), lambda i, j: (i, 0)))
    if rope is not None:
        per = SEQ // tm

        def tab_map(i, j):
            return (jnp.where(i < nct, 0, 1 + (i - nct) % per), 0)
        in_specs += [pl.BlockSpec((tm, LANES), tab_map), pl.BlockSpec((tm, LANES), tab_map)]
        args += list(rope)
    return pl.pallas_call(
        functools.partial(_proj_kernel, rope=rope is not None,
                          n_ctx_tiles=0 if x_ctx is None else nct, side=w_side is not None),
        out_shape=out_shape,
        grid=(rows // tm, n // tn),
        in_specs=in_specs,
        out_specs=out_specs,
        scratch_shapes=[pltpu.VMEM((tm, D), BF16)],
        compiler_params=_cp(("parallel", "arbitrary")),
        name=name,
    )(*args)


def _head_mask(x, a):
    lane = lax.broadcasted_iota(jnp.int32, (1, LANES), 1)
    keep = (lane < HD) if a == 0 else (lane >= HD)
    return jnp.where(keep, x, jnp.zeros_like(x))


def _nt(a, b):
    return lax.dot_general(a, b, (((1,), (1,)), ((), ())), preferred_element_type=F32)


def _windowed_attention(q, kc, vc, kl_ref, vl_ref, starts, nk, local_fn, sink_fn):
    lane = lax.broadcasted_iota(jnp.int32, (1, LANES), 1)
    nsub = len(starts)
    sq = q.shape[0] // nsub
    rows = [slice(t * sq, (t + 1) * sq) for t in range(nsub)]

    s_ctx, s_loc = [], []
    for a in range(2):
        qa = _head_mask(q, a) * jnp.asarray(HD ** -0.5, q.dtype)
        s_ctx.append(_nt(qa, kc))
        s_loc.append([local_fn(a, t, _nt(qa[rows[t]], kl_ref[pl.ds(starts[t], nk), :]))
                      for t in range(nsub)])

    p_ctx, p_loc, dens = [], [], []
    for a in range(2):
        sink = sink_fn(a)
        pc_a, pl_a, den_a = [], [], []
        for t in range(nsub):
            s_c = s_ctx[a][rows[t]]
            s_l = s_loc[a][t]
            m = jnp.maximum(jnp.max(s_c, axis=-1, keepdims=True),
                            jnp.max(s_l, axis=-1, keepdims=True))
            if sink is not None:
                m = jnp.maximum(m, sink)
            p_c = jnp.exp(s_c - m)
            p_l = jnp.exp(s_l - m)
            den = jnp.sum(p_c, axis=-1, keepdims=True) + jnp.sum(p_l, axis=-1, keepdims=True)
            if sink is not None:
                den = den + jnp.exp(sink - m)
            den_a.append(den)
            pc_a.append(p_c.astype(BF16))
            pl_a.append(p_l.astype(BF16))
        p_ctx.append(jnp.concatenate(pc_a, axis=0))
        p_loc.append(pl_a)
        dens.append(jnp.concatenate(den_a, axis=0))

    outs = []
    for a in range(2):
        o = jnp.dot(p_ctx[a], vc, preferred_element_type=F32)
        o_loc = [jnp.dot(p_loc[a][t], vl_ref[pl.ds(starts[t], nk), :],
                         preferred_element_type=F32) for t in range(nsub)]
        outs.append((o + jnp.concatenate(o_loc, axis=0)) / dens[a])
    return jnp.where(lane < HD, outs[0], outs[1])


def _na_window_start(r_first):
    return jnp.clip(r_first - NA_ROWS // 2, 0, SEQ // GRID_W - NA_KROWS)


NA_NDROW = 2 * NA_ROWS - 1


def _na_bias_tile(toe_ref, a, r_first, w0r):
    lane = lax.broadcasted_iota(jnp.int32, (1, LANES), 1)
    nrows = SEQ // GRID_W
    tile_rows = []
    for rq in range(NA_SUBROWS):
        r = r_first + rq
        r0 = jnp.clip(r - NA_ROWS // 2, 0, nrows - NA_ROWS)
        blocks = []
        for j in range(NA_KROWS // 2):
            kr = w0r + 2 * j
            e = jnp.clip(kr - r + NA_ROWS, 0, NA_NDROW)
            pen0 = jnp.where(jnp.logical_and(kr >= r0, kr < r0 + NA_ROWS), 0.0, NEG)
            pen1 = jnp.where(jnp.logical_and(kr + 1 >= r0, kr + 1 < r0 + NA_ROWS), 0.0, NEG)
            blocks.append(toe_ref[a, e] + jnp.where(lane < GRID_W, pen0, pen1))
        tile_rows.append(jnp.concatenate(blocks, axis=1))
    return jnp.concatenate(tile_rows, axis=0)


def _na_kernel(q_ref, kl_ref, vl_ref, kc_ref, vc_ref, toe_ref, o_ref):
    t = pl.program_id(1)
    firsts = [NA_QROWS * t + NA_SUBROWS * s for s in range(NA_QROWS // NA_SUBROWS)]
    w0rs = [_na_window_start(r) for r in firsts]
    starts = [pl.multiple_of(w * GRID_W, 2 * GRID_W) for w in w0rs]
    o = _windowed_attention(
        q_ref[...], kc_ref[...], vc_ref[...], kl_ref, vl_ref, starts, NA_KROWS * GRID_W,
        lambda a, s, logits: logits + _na_bias_tile(toe_ref, a, firsts[s], w0rs[s]),
        lambda a: None)
    o_ref[...] = o.astype(o_ref.dtype)


def _na_bias_kernel(rpb_ref, o_ref):
    c = lax.broadcasted_iota(jnp.int32, (GRID_W, LANES), 0)
    lane = lax.broadcasted_iota(jnp.int32, (GRID_W, LANES), 1)
    kc = lane % GRID_W
    c0 = jnp.clip(c - NA_COLS // 2, 0, GRID_W - NA_COLS)
    col_ok = jnp.logical_and(kc >= c0, kc < c0 + NA_COLS)
    neg = jnp.full((GRID_W, LANES), NEG, F32)

    def toeplitz(d, shift):
        v = jnp.broadcast_to(rpb_ref[0, d:d + 1, :], (GRID_W, LANES))
        return pltpu.roll(v, shift % LANES, 1, stride=1, stride_axis=0)

    for e in range(NA_NDROW + 1):
        lo = toeplitz(e - 1, -(NA_COLS - 1)) if e >= 1 else neg
        hi = toeplitz(e, GRID_W - (NA_COLS - 1)) if e < NA_NDROW else neg
        o_ref[0, e] = jnp.where(col_ok, jnp.where(lane < GRID_W, lo, hi), NEG)


def _na_bias_blocks(rpb):
    nd, ncol = rpb.shape[1], rpb.shape[2]
    padded = jnp.pad(rpb.astype(F32), ((0, 0), (0, NA_NDROW + 1 - nd), (0, LANES - ncol)))
    return pl.pallas_call(
        _na_bias_kernel,
        out_shape=jax.ShapeDtypeStruct((NA_HEADS, NA_NDROW + 1, GRID_W, LANES), F32),
        grid=(NA_HEADS,),
        in_specs=[pl.BlockSpec((1, NA_NDROW + 1, LANES), lambda h: (h, 0, 0))],
        out_specs=pl.BlockSpec((1, NA_NDROW + 1, GRID_W, LANES), lambda h: (h, 0, 0, 0)),
        compiler_params=_cp(("arbitrary",)),
        name="na_bias_blocks",
    )(padded)


def _na_attention(big, bias, nb, nctx):
    tq = NA_QROWS * GRID_W
    ntile = SEQ // tq
    nct_q = nctx // tq
    lat0 = nctx // SEQ
    return pl.pallas_call(
        _na_kernel,
        out_shape=jax.ShapeDtypeStruct((nb * SEQ, NA_HEADS * HD), BF16),
        grid=(NA_HEADS // 2, ntile, nb),
        in_specs=[
            pl.BlockSpec((tq, LANES), lambda h, t, b: (nct_q + b * ntile + t, EV_Q_BLK + h)),
            pl.BlockSpec((SEQ, LANES), lambda h, t, b: (lat0 + b, EV_K_BLK + h)),
            pl.BlockSpec((SEQ, LANES), lambda h, t, b: (lat0 + b, EV_V_BLK + h)),
            pl.BlockSpec((CTX, LANES), lambda h, t, b: (b, EV_K_BLK + h)),
            pl.BlockSpec((CTX, LANES), lambda h, t, b: (b, EV_V_BLK + h)),
            pl.BlockSpec((2, NA_NDROW + 1, GRID_W, LANES), lambda h, t, b: (h, 0, 0, 0)),
        ],
        out_specs=pl.BlockSpec((tq, LANES), lambda h, t, b: (b * ntile + t, h)),
        compiler_params=_cp(("arbitrary", "arbitrary", "arbitrary")),
        name="na_attention",
    )(big, big, big, big, big, bias)


def _ctx_attn_kernel(q_ref, k_ref, v_ref, o_ref):
    lane = lax.broadcasted_iota(jnp.int32, (1, LANES), 1)
    q = q_ref[...]
    k = k_ref[...]
    v = v_ref[...]
    outs = []
    for a in range(2):
        qa = _head_mask(q, a) * jnp.asarray(HD ** -0.5, q.dtype)
        s = _nt(qa, k)
        m = jnp.max(s, axis=-1, keepdims=True)
        p = jnp.exp(s - m)
        den = jnp.sum(p, axis=-1, keepdims=True)
        outs.append(jnp.dot(p.astype(BF16), v, preferred_element_type=F32) / den)
    o_ref[...] = jnp.where(lane < HD, outs[0], outs[1]).astype(o_ref.dtype)


def _ctx_attention(big, nctx):
    return pl.pallas_call(
        _ctx_attn_kernel,
        out_shape=jax.ShapeDtypeStruct((nctx, NA_HEADS * HD), BF16),
        grid=(nctx // CTX, NA_HEADS // 2),
        in_specs=[
            pl.BlockSpec((CTX, LANES), lambda b, h: (b, EV_Q_BLK + h)),
            pl.BlockSpec((CTX, LANES), lambda b, h: (b, EV_K_BLK + h)),
            pl.BlockSpec((CTX, LANES), lambda b, h: (b, EV_V_BLK + h)),
        ],
        out_specs=pl.BlockSpec((CTX, LANES), lambda b, h: (b, h)),
        compiler_params=_cp(("arbitrary", "arbitrary")),
        name="ctx_attention",
    )(big, big, big)


def _swa_kernel(q_ref, kl_ref, vl_ref, kc_ref, vc_ref, sink_ref, o_ref):
    h2 = pl.program_id(1)
    n = pl.program_id(2)
    starts, oks = [], []
    for t in range(SWA_QT // SWA_SUB):
        q0 = SWA_QT * n + SWA_SUB * t
        start = pl.multiple_of(jnp.clip(q0 - SWA_WINDOW, 0, SEQ - SWA_KT), SWA_WINDOW)
        qpos = q0 + lax.broadcasted_iota(jnp.int32, (SWA_SUB, 1), 0)
        kpos = start + lax.broadcasted_iota(jnp.int32, (1, SWA_KT), 1)
        starts.append(start)
        oks.append(jnp.abs(kpos - qpos) <= SWA_WINDOW)

    o = _windowed_attention(
        q_ref[...], kc_ref[...], vc_ref[...], kl_ref, vl_ref, starts, SWA_KT,
        lambda a, t, logits: jnp.where(oks[t], logits, NEG),
        lambda a: sink_ref[pl.ds(2 * h2 + a, 1), :][:, :1])
    o_ref[...] = o.astype(o_ref.dtype)


def _swa_attention(rp, pp, sink, nb, nctx):
    nq = SEQ // SWA_QT
    nct_q = nctx // SWA_QT
    lat0 = nctx // SEQ
    sinkv = jnp.broadcast_to(sink.astype(F32)[:, None], (SWA_HEADS, LANES))
    return pl.pallas_call(
        _swa_kernel,
        out_shape=jax.ShapeDtypeStruct((nb * SEQ, SWA_HEADS * HD), BF16),
        grid=(nb, SWA_HEADS // 2, nq),
        in_specs=[
            pl.BlockSpec((SWA_QT, LANES), lambda b, h, n: (nct_q + b * nq + n, h)),
            pl.BlockSpec((SEQ, LANES), lambda b, h, n: (lat0 + b, 12 + h // 2)),
            pl.BlockSpec((SEQ, LANES), lambda b, h, n: (lat0 + b, 16 + h // 2)),
            pl.BlockSpec((CTX, LANES), lambda b, h, n: (b, 12 + h // 2)),
            pl.BlockSpec((CTX, LANES), lambda b, h, n: (b, 16 + h // 2)),
            pl.BlockSpec((SWA_HEADS, LANES), lambda b, h, n: (0, 0)),
        ],
        out_specs=pl.BlockSpec((SWA_QT, LANES), lambda b, h, n: (b * nq + n, h)),
        compiler_params=_cp(("arbitrary", "arbitrary", "arbitrary")),
        name="swa_attention",
    )(rp, rp, pp, rp, pp, sinkv)


CONV_PAD = (SSM_CONV - 1) // 2
CONV_TAPS = tuple(s for s in range(-CONV_PAD, CONV_PAD + 1) if s != 0)
CONV_HALO = 16
CONV_EDGE = 8


def _conv_shift_tables(tb):
    nt = len(CONV_TAPS)
    shift = np.zeros((nt * tb, tb), np.float32)
    edge = np.zeros((nt * 2 * CONV_EDGE, 2 * CONV_HALO), np.float32)
    for t, s in enumerate(CONV_TAPS):
        for i in range(tb):
            j = i + s
            if 0 <= j < tb:
                shift[t * tb + i, j] = 1.0
            elif j < 0:
                edge[t * 2 * CONV_EDGE + i, CONV_HALO + j] = 1.0
            else:
                edge[t * 2 * CONV_EDGE + CONV_EDGE + i - (tb - CONV_EDGE), CONV_HALO + j - tb] = 1.0
    return jnp.asarray(shift, BF16), jnp.asarray(edge, BF16)


def _conv_kernel(xp_ref, x_ref, xn_ref, sh_ref, ed_ref, w_ref, b_ref, o_ref, *, nctx_blocks,
                 per_seq):
    i = pl.program_id(0)
    tb = x_ref.shape[0]
    k = i - nctx_blocks
    has_prev = jnp.logical_and(i >= nctx_blocks, k % per_seq != 0)
    has_next = jnp.logical_and(i >= nctx_blocks, k % per_seq != per_seq - 1)
    x = x_ref[...]
    zero = jnp.zeros((CONV_HALO, x.shape[1]), x.dtype)
    halo = jnp.concatenate([jnp.where(has_prev, xp_ref[...], zero),
                            jnp.where(has_next, xn_ref[...], zero)], axis=0)
    taps = jnp.dot(sh_ref[...], x, preferred_element_type=F32)
    edges = jnp.dot(ed_ref[...], halo, preferred_element_type=F32)
    acc = x.astype(F32) * w_ref[CONV_PAD:CONV_PAD + 1, :] + b_ref[...]
    for t, s in enumerate(CONV_TAPS):
        tap = taps[t * tb:(t + 1) * tb]
        e0 = 2 * CONV_EDGE * t
        tap = jnp.concatenate([tap[:CONV_EDGE] + edges[e0:e0 + CONV_EDGE],
                               tap[CONV_EDGE:tb - CONV_EDGE],
                               tap[tb - CONV_EDGE:] + edges[e0 + CONV_EDGE:e0 + 2 * CONV_EDGE]], axis=0)
        acc = acc + tap * w_ref[CONV_PAD + s:CONV_PAD + s + 1, :]
    o_ref[...] = _silu(acc).astype(o_ref.dtype)


def _conv_silu(big, conv_w, conv_b, nctx):
    rows = big.shape[0]
    tb = CTX
    tc = SSM_CONV_CH
    nblk = rows // tb
    hpb = tb // CONV_HALO
    shift, edge = _conv_shift_tables(tb)
    return pl.pallas_call(
        functools.partial(_conv_kernel, nctx_blocks=nctx // tb, per_seq=SEQ // tb),
        out_shape=jax.ShapeDtypeStruct((rows, SSM_CONV_CH), BF16),
        grid=(nblk, SSM_CONV_CH // tc),
        in_specs=[
            pl.BlockSpec((CONV_HALO, tc), lambda i, c: (jnp.maximum(i * hpb - 1, 0), c)),
            pl.BlockSpec((tb, tc), lambda i, c: (i, c)),
            pl.BlockSpec((CONV_HALO, tc),
                         lambda i, c: (jnp.minimum((i + 1) * hpb, nblk * hpb - 1), c)),
            pl.BlockSpec(shift.shape, lambda i, c: (0, 0)),
            pl.BlockSpec(edge.shape, lambda i, c: (0, 0)),
            pl.BlockSpec((SSM_CONV, tc), lambda i, c: (0, c)),
            pl.BlockSpec((1, tc), lambda i, c: (0, c)),
        ],
        out_specs=pl.BlockSpec((tb, tc), lambda i, c: (i, c)),
        compiler_params=_cp(("arbitrary", "arbitrary")),
        name="conv_silu",
    )(big, big, big, shift, edge, conv_w.reshape(SSM_CONV, SSM_CONV_CH),
      conv_b.reshape(1, SSM_CONV_CH))


NCC = CTX // CHUNK
NLC = SEQ // CHUNK


def _chunk_block(b, s, d, nctx):
    cc = s if d == 0 else NCC - 1 - s
    lc = s - NCC if d == 0 else NLC + NCC - 1 - s
    return jnp.where(s < NCC, NCC * b + cc, nctx // CHUNK + NLC * b + lc)


def _latent_chunk_block(b, s, d):
    lc = s - NCC if d == 0 else NLC + NCC - 1 - s
    return NLC * b + jnp.clip(lc, 0, NLC - 1)


def _split3(a):
    hi = a.astype(BF16)
    r1 = a - hi.astype(F32)
    mid = r1.astype(BF16)
    lo = (r1 - mid.astype(F32)).astype(BF16)
    return hi, mid, lo


def _softplus(x):
    return jnp.maximum(x, 0.0) + jnp.log(1.0 + jnp.exp(-jnp.abs(x)))


def _ssd_direction(d, x_ref, b_ref, c_ref, dt_ref, dtt_ref, bias_ref, biast_ref, alog_ref,
                   alogt_ref, y_ref, s_ref):
    q = CHUNK
    ii = lax.broadcasted_iota(jnp.int32, (q, q), 0)
    jj = lax.broadcasted_iota(jnp.int32, (q, q), 1)
    causal = (jj <= ii) if d == 0 else (jj >= ii)
    tri = jnp.where(causal, 1.0, 0.0).astype(BF16)
    trit = jnp.where((ii <= jj) if d == 0 else (ii >= jj), 1.0, 0.0).astype(BF16)

    dt = _softplus(dt_ref[0] + bias_ref[d])
    a = dt * (-jnp.exp(alog_ref[d]))
    dtt = _softplus(dtt_ref[0] + biast_ref[d])
    at = dtt * (-jnp.exp(alogt_ref[d]))
    acum = sum(jnp.dot(tri, p, preferred_element_type=F32) for p in _split3(a))
    acumt = sum(jnp.dot(p, trit, preferred_element_type=F32) for p in _split3(at))
    atot = jnp.sum(at, axis=1, keepdims=True)
    wrow = dtt * jnp.exp(atot - acumt)
    keep = jnp.exp(jnp.broadcast_to(atot, (SSM_HEADS, LANES)))

    first = lax.broadcasted_iota(jnp.int32, (1, LANES), 1) < HD
    ngroups = b_ref.shape[1] // SSM_STATE
    pairs_per_group = SSM_HEADS // 2 // ngroups
    for g in range(ngroups):
        bg = b_ref[:, g * SSM_STATE:(g + 1) * SSM_STATE]
        cg = c_ref[:, g * SSM_STATE:(g + 1) * SSM_STATE]
        cb = _nt(cg, bg)
        cgf = cg.astype(F32)
        bt = bg.astype(F32).T
        for pp in range(pairs_per_group):
            p = g * pairs_per_group + pp
            x = x_ref[:, p * LANES:(p + 1) * LANES]
            st = s_ref[d, p]
            rhs = jnp.concatenate([x, st.astype(BF16)], axis=0)
            ys, us = [], []
            for h in (2 * p, 2 * p + 1):
                col = jnp.broadcast_to(acum[:, h:h + 1], (q, LANES))
                dec = jnp.exp(jnp.where(causal, col - acumt[h:h + 1, :], NEG)) * (cb * dtt[h:h + 1, :])
                lhs = jnp.concatenate([dec, jnp.exp(col) * cgf], axis=1).astype(BF16)
                ys.append(jnp.dot(lhs, rhs, preferred_element_type=F32))
                us.append(jnp.dot((bt * wrow[h:h + 1, :]).astype(BF16), x,
                                  preferred_element_type=F32))
            y_ref[:, p * LANES:(p + 1) * LANES] = jnp.where(first, ys[0], ys[1]).astype(y_ref.dtype)
            keep_p = jnp.where(first, keep[2 * p:2 * p + 1, :], keep[2 * p + 1:2 * p + 2, :])
            s_ref[d, p] = st * keep_p + jnp.where(first, us[0], us[1])


def _ssd_kernel(*refs):
    fwd, bwd, (bias_ref, biast_ref, alog_ref, alogt_ref), (yf_ref, yb_ref, s_ref) = (
        refs[0:5], refs[5:10], refs[10:14], refs[14:17])
    assert CHUNK == LANES

    @pl.when(pl.program_id(1) == 0)
    def _():
        s_ref[...] = jnp.zeros_like(s_ref)

    for d, ins, y_ref in ((0, fwd, yf_ref), (1, bwd, yb_ref)):
        _ssd_direction(d, *ins, bias_ref, biast_ref, alog_ref, alogt_ref, y_ref, s_ref)


def _ssd_scan(xbc, dt_raw, dt_bias, a_log, nb, nctx):
    rows = xbc.shape[0]
    dtd = dt_raw.reshape(rows, 2, SSM_HEADS).transpose(1, 0, 2)
    dtt = dtd.transpose(0, 2, 1)

    def direction_specs(d):
        blk = functools.partial(_chunk_block, d=d, nctx=nctx)
        return [
            pl.BlockSpec((CHUNK, SSM_INNER), lambda b, s: (blk(b, s), 0)),
            pl.BlockSpec((CHUNK, 2 * SSM_STATE), lambda b, s: (blk(b, s), 4)),
            pl.BlockSpec((CHUNK, 2 * SSM_STATE), lambda b, s: (blk(b, s), 5)),
            pl.BlockSpec((1, CHUNK, SSM_HEADS), lambda b, s: (d, blk(b, s), 0)),
            pl.BlockSpec((1, SSM_HEADS, CHUNK), lambda b, s: (d, 0, blk(b, s))),
        ]

    whole = lambda shape: pl.BlockSpec(shape, lambda b, s: (0, 0, 0))
    out_spec = lambda d: pl.BlockSpec(
        (CHUNK, SSM_INNER), lambda b, s: (_chunk_block(b, s, d, nctx), 0))
    y_shape = jax.ShapeDtypeStruct((rows, SSM_INNER), BF16)
    return pl.pallas_call(
        _ssd_kernel,
        out_shape=(y_shape, y_shape),
        grid=(nb, NCC + NLC),
        in_specs=direction_specs(0) + direction_specs(1) + [
            whole((2, 1, SSM_HEADS)), whole((2, SSM_HEADS, 1)),
            whole((2, 1, SSM_HEADS)), whole((2, SSM_HEADS, 1)),
        ],
        out_specs=(out_spec(0), out_spec(1)),
        scratch_shapes=[pltpu.VMEM((2, SSM_HEADS // 2, SSM_STATE, LANES), F32)],
        compiler_params=_cp(("arbitrary", "arbitrary")),
        name="ssd_scan",
    )(*([xbc, xbc, xbc, dtd, dtt] * 2),
      dt_bias.reshape(2, 1, SSM_HEADS), dt_bias.reshape(2, SSM_HEADS, 1),
      a_log.reshape(2, 1, SSM_HEADS), a_log.reshape(2, SSM_HEADS, 1))


def _ret_geometry(d):
    q = CHUNK
    ii = lax.broadcasted_iota(jnp.int32, (q, q), 0)
    jj = lax.broadcasted_iota(jnp.int32, (q, q), 1)
    dist = (ii - jj) if d == 0 else (jj - ii)
    ri = lax.broadcasted_iota(jnp.int32, (q, LANES), 0)
    cj = lax.broadcasted_iota(jnp.int32, (1, q), 1)
    steps_in = ((ri + 1) if d == 0 else (q - ri)).astype(F32)
    steps_out = ((q - 1 - cj) if d == 0 else cj).astype(F32)
    return dist >= 0, dist.astype(F32), steps_in, steps_out


def _ret_kernel(*refs):
    fwd, bwd, ld_ref, (yf_ref, yb_ref, s_ref) = refs[0:3], refs[3:6], refs[6], refs[7:10]
    q = CHUNK
    step = pl.program_id(1)

    @pl.when(step == 0)
    def _():
        s_ref[...] = jnp.zeros_like(s_ref)

    scale = jnp.asarray(HD ** -0.5, BF16)
    dirs = ((0, fwd, yf_ref), (1, bwd, yb_ref))

    def log_gamma(d, h):
        lg = jnp.log(1.0 - jnp.exp(ld_ref[d, h]))
        return jnp.broadcast_to(lg[None], (q // 8, 8, LANES)).reshape(q, LANES)

    qk = {}
    for d, (c_ref, b_ref, x_ref), y_ref in dirs:
        for h in range(RET_HEADS):
            pb, a = h // 2, h % 2
            ch = _head_mask(c_ref[:, pb * LANES:(pb + 1) * LANES], a) * scale
            bh = _head_mask(b_ref[:, pb * LANES:(pb + 1) * LANES], a)
            qk[d, h] = (ch, _nt(ch, bh))

    for d, (c_ref, b_ref, x_ref), y_ref in dirs:
        causal, distf, steps_in, _ = _ret_geometry(d)
        for h in range(RET_HEADS):
            lg = log_gamma(d, h)
            ch, cb = qk[d, h]
            dec = jnp.exp(jnp.where(causal, distf * lg, NEG)) * cb
            grow = jnp.exp(steps_in * lg) * ch.astype(F32)
            lhs = jnp.concatenate([dec, grow], axis=1).astype(BF16)
            x = x_ref[:, h * LANES:(h + 1) * LANES]
            rhs = jnp.concatenate([x, s_ref[d, h].astype(BF16)], axis=0)
            y_ref[:, h * LANES:(h + 1) * LANES] = jnp.dot(
                lhs, rhs, preferred_element_type=F32).astype(y_ref.dtype)

    for d, (c_ref, b_ref, x_ref), y_ref in dirs:
        _, _, _, steps_out = _ret_geometry(d)
        for h in range(RET_HEADS):
            lg = log_gamma(d, h)
            pb, a = h // 2, h % 2
            bt = _head_mask(b_ref[:, pb * LANES:(pb + 1) * LANES], a).astype(F32).T
            bth = (bt * jnp.exp(steps_out * lg[0:1, :])).astype(BF16)
            x = x_ref[:, h * LANES:(h + 1) * LANES]
            s_ref[d, h] = s_ref[d, h] * jnp.exp(q * lg) + jnp.dot(
                bth, x, preferred_element_type=F32)


def _retention_scan(rp, pp, log_decay, nb, nctx):
    w = RET_HEADS * HD
    wv = RET_HEADS * LANES

    def direction_specs(d):
        blk = functools.partial(_chunk_block, d=d, nctx=nctx)
        return [
            pl.BlockSpec((CHUNK, w), lambda b, s: (blk(b, s), 1)),
            pl.BlockSpec((CHUNK, w), lambda b, s: (blk(b, s), 2)),
            pl.BlockSpec((CHUNK, wv), lambda b, s: (blk(b, s), 0)),
        ]

    out_spec = lambda d: pl.BlockSpec((CHUNK, wv), lambda b, s: (_latent_chunk_block(b, s, d), 0))
    y_shape = jax.ShapeDtypeStruct((nb * SEQ, wv), BF16)
    return pl.pallas_call(
        _ret_kernel,
        out_shape=(y_shape, y_shape),
        grid=(nb, NCC + NLC),
        in_specs=direction_specs(0) + direction_specs(1) + [
            pl.BlockSpec((2, RET_HEADS, 8, LANES), lambda b, s: (0, 0, 0, 0))],
        out_specs=(out_spec(0), out_spec(1)),
        scratch_shapes=[pltpu.VMEM((2, RET_HEADS, LANES, LANES), F32)],
        compiler_params=_cp(("arbitrary", "arbitrary")),
        name="retention_scan",
    )(*([rp, rp, pp] * 2),
      jnp.broadcast_to(log_decay.astype(F32)[:, :, None, None], (2, RET_HEADS, 8, LANES)))


def _ssm_gate_norm(yf_ref, yb_ref, xs_ref, z_ref, dsk_ref, g_ref):
    y = yf_ref[...].astype(F32) + yb_ref[...].astype(F32) + xs_ref[...].astype(F32) * dsk_ref[...]
    yz = y * _silu(z_ref[...].astype(F32))
    half = SSM_INNER // 2
    segs = []
    for g in range(2):
        seg = yz[:, g * half:(g + 1) * half]
        ms = jnp.mean(seg * seg, axis=-1, keepdims=True)
        segs.append((seg * lax.rsqrt(ms + EPS) * g_ref[:, g * half:(g + 1) * half]).astype(BF16))
    return jnp.concatenate(segs, axis=1)


def _ret_gate_norm(rf_ref, rb_ref, rg_ref, g_ref, b_ref):
    y = rf_ref[...].astype(F32) + rb_ref[...].astype(F32)
    segs = []
    for h in range(RET_HEADS):
        sl = slice(h * LANES, (h + 1) * LANES)
        seg = y[:, sl]
        mu = jnp.mean(seg, axis=-1, keepdims=True)
        cen = seg - mu
        var = jnp.mean(cen * cen, axis=-1, keepdims=True)
        yn = cen * lax.rsqrt(var + EPS) * g_ref[:, sl] + b_ref[:, sl]
        segs.append((_silu(rg_ref[:, sl].astype(F32)) * yn).astype(BF16))
    return jnp.concatenate(segs, axis=1)


def _outproj_kernel(*refs, n_ctx_tiles, split_x, split_a1, mixer_fn):
    refs = list(refs)
    x_ref = refs.pop(0)
    xc_ref = refs.pop(0) if split_x else None
    a1_ref = refs.pop(0)
    a1c_ref = refs.pop(0) if split_a1 else None
    w1_ref, w2_ref, mod_ref, o_ref = refs[-4:]
    is_ctx = pl.program_id(0) < n_ctx_tiles
    a1 = jnp.where(is_ctx, a1c_ref[...], a1_ref[...]) if split_a1 else a1_ref[...]
    x = jnp.where(is_ctx, xc_ref[...], x_ref[...]) if split_x else x_ref[...]
    a2 = mixer_fn(*refs[:-4])
    y = (jnp.dot(a1, w1_ref[...], preferred_element_type=F32)
         + jnp.dot(a2, w2_ref[...], preferred_element_type=F32))
    o_ref[...] = x + mod_ref[0][2:3] * y


def _outproj(x, a1, mixer_fn, mixer_args, mixer_specs, w_out, mod, nctx, tm, rows, x_off=0,
             a1_ctx=None, x_ctx=None):
    k1 = a1.shape[1]
    k2 = w_out.shape[0] - k1
    w1 = w_out[:k1].astype(BF16)
    w2 = w_out[k1:].astype(BF16)
    nct = nctx // tm if (a1_ctx is not None or x_ctx is not None) else 0
    lat = lambda i: (jnp.maximum(i - nct, 0), 0)
    ctx_ = lambda i: (jnp.minimum(i, nct - 1), 0)
    in_specs, args = [], []
    if x_ctx is None:
        in_specs.append(pl.BlockSpec((tm, D), lambda i: (i + x_off, 0)))
        args.append(x)
    else:
        in_specs += [pl.BlockSpec((tm, D), lat), pl.BlockSpec((tm, D), ctx_)]
        args += [x, x_ctx]
    in_specs.append(pl.BlockSpec((tm, k1), lat))
    args.append(a1)
    if a1_ctx is not None:
        in_specs.append(pl.BlockSpec((tm, k1), ctx_))
        args.append(a1_ctx)
    in_specs += list(mixer_specs) + [
        pl.BlockSpec((k1, D), lambda i: (0, 0)),
        pl.BlockSpec((k2, D), lambda i: (0, 0)),
        pl.BlockSpec((1, 6, D), lambda i: (_mod_row((i + x_off) * tm, nctx), 0, 0)),
    ]
    args += list(mixer_args) + [w1, w2, mod]
    return pl.pallas_call(
        functools.partial(_outproj_kernel, n_ctx_tiles=nct, split_x=x_ctx is not None,
                          split_a1=a1_ctx is not None, mixer_fn=mixer_fn),
        out_shape=jax.ShapeDtypeStruct((rows, D), F32),
        grid=(rows // tm,),
        in_specs=in_specs,
        out_specs=pl.BlockSpec((tm, D), lambda i: (i, 0)),
        compiler_params=_cp(("parallel",)),
        name="outproj",
    )(*args)


def _top2_gates(logits):
    lane = lax.broadcasted_iota(jnp.int32, logits.shape, 1)
    valid = lane < N_EXPERTS
    l0 = jnp.where(valid, logits, NEG)
    m1 = jnp.max(l0, axis=-1, keepdims=True)
    i1 = jnp.min(jnp.where(l0 == m1, lane, LANES), axis=-1, keepdims=True)
    l1 = jnp.where(lane == i1, NEG, l0)
    m2 = jnp.max(l1, axis=-1, keepdims=True)
    i2 = jnp.min(jnp.where(l1 == m2, lane, LANES), axis=-1, keepdims=True)
    e2 = jnp.exp(m2 - m1)
    g1 = 1.0 / (1.0 + e2)
    g2 = e2 / (1.0 + e2)
    return jnp.where(lane == i1, g1, 0.0) + jnp.where(lane == i2, g2, 0.0)


def _ffn_kernel(x_ref, g_ref, mod_ref, *rest, routed, final, sub):
    rest = list(rest)
    router_ref = rest.pop(0) if routed else None
    w1_ref, w3_ref, w2_ref = rest[:3]
    rest = rest[3:]
    fg_ref = rest.pop(0) if final else None
    o_ref, h_ref, acc_ref = rest[:3]
    gate_ref = rest[3] if routed else None
    e = pl.program_id(1)
    f = pl.program_id(2)
    first = jnp.logical_and(e == 0, f == 0)
    last = jnp.logical_and(e == pl.num_programs(1) - 1, f == pl.num_programs(2) - 1)

    @pl.when(first)
    def _():
        h = _norm_mod(x_ref[...], g_ref[...], mod_ref[0], 3)
        h16 = h.astype(BF16)
        h_ref[...] = h16
        acc_ref[...] = jnp.zeros_like(acc_ref)
        if routed:
            hl = (h - h16.astype(F32)).astype(BF16)
            r = router_ref[...]
            rh = r.astype(BF16)
            rl = (r - rh.astype(F32)).astype(BF16)
            logits = (jnp.dot(h16, rh, preferred_element_type=F32)
                      + jnp.dot(h16, rl, preferred_element_type=F32)
                      + jnp.dot(hl, rh, preferred_element_type=F32))
            gate_ref[...] = _top2_gates(logits)

    h16 = h_ref[...]
    if routed:
        lane = lax.broadcasted_iota(jnp.int32, gate_ref.shape, 1)
        ge = jnp.sum(jnp.where(lane == e, gate_ref[...], 0.0), axis=-1, keepdims=True)
    for c in range(w1_ref.shape[2] // sub):
        cols = slice(c * sub, (c + 1) * sub)
        u = _silu(jnp.dot(h16, w1_ref[0, :, cols], preferred_element_type=F32)) * jnp.dot(
            h16, w3_ref[0, :, cols], preferred_element_type=F32)
        if routed:
            u = u * ge
        acc_ref[...] += jnp.dot(u.astype(BF16), w2_ref[0, cols, :], preferred_element_type=F32)

    @pl.when(last)
    def _():
        y = x_ref[...] + mod_ref[0][5:6] * acc_ref[...]
        if final:
            ms = jnp.mean(y * y, axis=-1, keepdims=True)
            y = y * lax.rsqrt(ms + EPS) * fg_ref[...]
        o_ref[...] = y


def _ffn(x, g, mod, w1, w3, w2, nctx, tm, tf, sub, x_off=0, rows=None, router=None,
         final_g=None):
    rows = x.shape[0] if rows is None else rows
    ne, _, dff = w1.shape
    routed = router is not None
    final = final_g is not None
    in_specs = [
        pl.BlockSpec((tm, D), lambda i, e, f: (i + x_off, 0)),
        pl.BlockSpec((1, D), lambda i, e, f: (0, 0)),
        pl.BlockSpec((1, 6, D), lambda i, e, f: (_mod_row((i + x_off) * tm, nctx), 0, 0)),
    ]
    args = [x, g.reshape(1, D), mod]
    if routed:
        in_specs.append(pl.BlockSpec((D, LANES), lambda i, e, f: (0, 0)))
        args.append(jnp.pad(router.astype(F32), ((0, 0), (0, LANES - ne))))
    in_specs += [
        pl.BlockSpec((1, D, tf), lambda i, e, f: (e, 0, f)),
        pl.BlockSpec((1, D, tf), lambda i, e, f: (e, 0, f)),
        pl.BlockSpec((1, tf, D), lambda i, e, f: (e, f, 0)),
    ]
    args += [w1, w3, w2]
    if final:
        in_specs.append(pl.BlockSpec((1, D), lambda i, e, f: (0, 0)))
        args.append(final_g.reshape(1, D))
    scratch = [pltpu.VMEM((tm, D), BF16), pltpu.VMEM((tm, D), F32)]
    if routed:
        scratch.append(pltpu.VMEM((tm, LANES), F32))
    return pl.pallas_call(
        functools.partial(_ffn_kernel, routed=routed, final=final, sub=sub),
        out_shape=jax.ShapeDtypeStruct((rows, D), F32),
        grid=(rows // tm, ne, dff // tf),
        in_specs=in_specs,
        out_specs=pl.BlockSpec((tm, D), lambda i, e, f: (i, 0)),
        scratch_shapes=scratch,
        compiler_params=_cp(("parallel", "arbitrary", "arbitrary")),
        name="moe_ffn" if routed else "dense_ffn",
    )(*args)


def _rope_tables(tm):
    t = np.arange(SEQ)
    n_freq = HD // 4
    inv = ROPE_BASE ** (-np.arange(n_freq, dtype=np.float64) / n_freq)
    ang = np.concatenate([(t // GRID_W)[:, None] * inv, (t % GRID_W)[:, None] * inv], axis=-1)
    cos = np.tile(np.cos(ang), (1, 4))
    sin = np.tile(np.concatenate([-np.sin(ang), np.sin(ang)], axis=-1), (1, 2))
    cos = np.concatenate([np.ones((tm, LANES)), cos], axis=0)
    sin = np.concatenate([np.zeros((tm, LANES)), sin], axis=0)
    return jnp.asarray(cos, F32), jnp.asarray(sin, F32)


def _dup_heads(w, nheads):
    w = w.reshape(D, nheads, 1, HD)
    return jnp.broadcast_to(w, (D, nheads, 2, HD)).reshape(D, nheads * 2 * HD)


def kernel(x, c, ctx, c_ctx, ada_w, ada_b, norm_attn_g, norm_ffn_g, ev_w_in, ev_w_out, na_rpb,
           ssm_conv_w, ssm_conv_b, ssm_dt_bias, ssm_a_log, ssm_d, ssm_norm_g, ffn_w1, ffn_w3, ffn_w2,
           od_w_in, od_w_out, swa_sink, ret_log_decay, ret_gn_g, ret_gn_b, moe_router, moe_w1,
           moe_w3, moe_w2, final_g):
    nb = x.shape[0]
    nctx = -(-(nb * CTX) // SEQ) * SEQ
    tm = 1024
    tmo = tm // 2
    x_lat = x.reshape(nb * SEQ, D).astype(F32)
    x_ctx = jnp.pad(ctx.reshape(nb * CTX, D).astype(F32), ((0, nctx - nb * CTX), (0, 0)))

    rp = -(-(nb + 1) // 8) * 8
    cin = jnp.zeros((rp, D), F32).at[0].set(c_ctx).at[1:nb + 1].set(c)
    mod = _modulation(cin, ada_w, ada_b).reshape(2, rp, 6, D)

    w_in = ev_w_in[0]
    q_, k_, v_, z_, xbc_, dt_ = jnp.split(w_in, [512, 1024, 1536, 2560, 4096], axis=1)
    w_big = jnp.concatenate([xbc_, q_, k_, v_, z_], axis=1).astype(BF16)
    big, dt_raw = _proj(x_lat, norm_attn_g[0], mod[0], w_big, BF16, nctx, tm, 2048, x_ctx=x_ctx,
                        w_side=dt_.astype(BF16), name="even_in_proj")
    attn = _na_attention(big, _na_bias_blocks(na_rpb[0]), nb, nctx)
    attn_ctx = _ctx_attention(big, nctx)
    xbc = _conv_silu(big, ssm_conv_w[0], ssm_conv_b[0], nctx)
    yf, yb = _ssd_scan(xbc, dt_raw, ssm_dt_bias[0], ssm_a_log[0], nb, nctx)
    wide = lambda col: pl.BlockSpec((tmo, SSM_INNER), lambda i: (i, col))
    vec = pl.BlockSpec((1, SSM_INNER), lambda i: (0, 0))
    dsk = jnp.repeat(ssm_d[0].astype(F32), HD).reshape(1, SSM_INNER)
    xs = _outproj(x_lat, attn, _ssm_gate_norm,
                  [yf, yb, xbc, big, dsk, ssm_norm_g[0].reshape(1, SSM_INNER)],
                  [wide(0), wide(0), wide(0), wide(EV_Z_BLK), vec, vec],
                  ev_w_out[0], mod[0], nctx, tmo, nctx + nb * SEQ, a1_ctx=attn_ctx, x_ctx=x_ctx)
    xs = _ffn(xs, norm_ffn_g[0], mod[0], ffn_w1.astype(BF16), ffn_w3.astype(BF16),
              ffn_w2.astype(BF16), nctx, tm // 2, D_FF, 256)

    w_in = od_w_in[0]
    q_, k_, v_, rq_, rk_, rv_, rg_ = jnp.split(w_in, [512, 640, 768, 1280, 1792, 2816], axis=1)
    w_rope = jnp.concatenate([q_, rq_, rk_, _dup_heads(k_, 2)], axis=1).astype(BF16)
    w_plain = jnp.concatenate([rv_, rg_, _dup_heads(v_, 2)], axis=1).astype(BF16)
    rope_tabs = _rope_tables(tm)
    rpj = _proj(xs, norm_attn_g[1], mod[1], w_rope, BF16, nctx, tm, w_rope.shape[1],
                rope=rope_tabs, name="odd_rope_proj")
    ppj = _proj(xs, norm_attn_g[1], mod[1], w_plain, BF16, nctx, tm, w_plain.shape[1],
                name="odd_plain_proj")
    yw = _swa_attention(rpj, ppj, swa_sink[0], nb, nctx)
    rf, rb = _retention_scan(rpj, ppj, ret_log_decay[0], nb, nctx)
    x_off = nctx // tmo
    rg_spec = pl.BlockSpec((tmo, SSM_INNER), lambda i: (i + x_off, 1))
    xl = _outproj(xs, yw, _ret_gate_norm,
                  [rf, rb, ppj, ret_gn_g[0].reshape(1, SSM_INNER), ret_gn_b[0].reshape(1, SSM_INNER)],
                  [wide(0), wide(0), rg_spec, vec, vec],
                  od_w_out[0], mod[1], nctx, tmo, nb * SEQ, x_off=x_off)
    out = _ffn(xl, norm_ffn_g[1], mod[1], moe_w1[0].astype(BF16), moe_w3[0].astype(BF16),
               moe_w2[0].astype(BF16), 0, tm, D_FF_EXPERT // 2, 256, router=moe_router[0],
               final_g=final_g)
    return out.reshape(nb, SEQ, D).astype(x.dtype)
```

```python
import functools

import numpy as np
import jax
import jax.numpy as jnp
from jax import lax
from jax.experimental import pallas as pl
from jax.experimental.pallas import tpu as pltpu

F32 = jnp.float32
BF16 = jnp.bfloat16

D = 1024
SEQ = 2048
CTX = 256
GRID_W = 64
HD = 64
EPS = 1e-6
ROPE_BASE = 10000.0
NEG = -1e30

NA_HEADS = 8
NA_ROWS = 8
NA_COLS = 16
NA_QROWS = 8
NA_SUBROWS = 2
NA_KROWS = 10
SSM_HEADS = 16
SSM_INNER = 1024
SSM_STATE = 128
SSM_CONV = 5
SSM_CONV_CH = 1536
CHUNK = 128
SWA_HEADS = 8
SWA_WINDOW = 128
SWA_QT = 512
SWA_SUB = 128
SWA_KT = SWA_SUB + 2 * SWA_WINDOW
RET_HEADS = 8
D_FF = 2816
N_EXPERTS = 8
D_FF_EXPERT = 3584

EV_Q_BLK, EV_K_BLK, EV_V_BLK = 12, 16, 20
EV_Z_BLK = 3

LANES = 128
VMEM_LIMIT = 56 * 1024 * 1024


def _cp(sem, vmem=VMEM_LIMIT):
    return pltpu.CompilerParams(dimension_semantics=sem, vmem_limit_bytes=vmem)


def _sigmoid(x):
    return 1.0 / (1.0 + jnp.exp(-x))


def _silu(x):
    return x * _sigmoid(x)


def _mod_row(start, nctx):
    return jnp.where(start < nctx, 0, 1 + (start - nctx) // SEQ)


def _mod_kernel(c_ref, w_ref, b_ref, o_ref):
    c = c_ref[...]
    h = _silu(c).astype(BF16)
    o_ref[0] = jnp.dot(h, w_ref[0].astype(BF16), preferred_element_type=F32) + b_ref[0]


def _modulation(cin, ada_w, ada_b):
    depth, _, n6 = ada_w.shape
    rp = cin.shape[0]
    tn = 1024
    return pl.pallas_call(
        _mod_kernel,
        out_shape=jax.ShapeDtypeStruct((depth, rp, n6), F32),
        grid=(depth, n6 // tn),
        in_specs=[
            pl.BlockSpec((rp, D), lambda l, j: (0, 0)),
            pl.BlockSpec((1, D, tn), lambda l, j: (l, 0, j)),
            pl.BlockSpec((1, 1, tn), lambda l, j: (l, 0, j)),
        ],
        out_specs=pl.BlockSpec((1, rp, tn), lambda l, j: (l, 0, j)),
        compiler_params=_cp(("arbitrary", "arbitrary")),
        name="adaln_mod",
    )(cin, ada_w, ada_b.reshape(depth, 1, n6))


def _norm_mod(x, g, m, k):
    ms = jnp.mean(x * x, axis=-1, keepdims=True)
    y = x * lax.rsqrt(ms + EPS) * g
    return y * (1.0 + m[k + 1:k + 2]) + m[k:k + 1]


def _swap32(r):
    lane = lax.broadcasted_iota(jnp.int32, r.shape, 1)
    return jnp.where((lane % 64) < 32, pltpu.roll(r, 96, 1), pltpu.roll(r, 32, 1))


def _proj_kernel(x_ref, *rest, rope, n_ctx_tiles, side):
    rest = list(rest)
    xc_ref = rest.pop(0) if n_ctx_tiles else None
    g_ref, mod_ref, w_ref = rest[:3]
    rest = rest[3:]
    ws_ref = rest.pop(0) if side else None
    if rope:
        c_ref, s_ref = rest[:2]
        rest = rest[2:]
    o_ref = rest.pop(0)
    os_ref = rest.pop(0) if side else None
    h_ref, = rest

    @pl.when(pl.program_id(1) == 0)
    def _():
        x = x_ref[...]
        if n_ctx_tiles:
            x = jnp.where(pl.program_id(0) < n_ctx_tiles, xc_ref[...], x)
        h = _norm_mod(x, g_ref[...], mod_ref[0], 0).astype(BF16)
        h_ref[...] = h
        if side:
            os_ref[...] = jnp.dot(h, ws_ref[...], preferred_element_type=F32)

    r = jnp.dot(h_ref[...], w_ref[...], preferred_element_type=F32)
    if rope:
        c = c_ref[...]
        s = s_ref[...]
        for k in range(r.shape[1] // LANES):
            rk = r[:, k * LANES:(k + 1) * LANES]
            o_ref[:, k * LANES:(k + 1) * LANES] = (rk * c + _swap32(rk) * s).astype(o_ref.dtype)
    else:
        o_ref[...] = r.astype(o_ref.dtype)


def _proj(x, g, mod, w, out_dtype, nctx, tm, tn, rope=None, x_ctx=None, w_side=None, name="proj"):
    n = w.shape[1]
    nct = nctx // tm
    if x_ctx is None:
        rows = x.shape[0]
        in_specs = [pl.BlockSpec((tm, D), lambda i, j: (i, 0))]
        args = [x]
    else:
        rows = x.shape[0] + x_ctx.shape[0]
        in_specs = [pl.BlockSpec((tm, D), lambda i, j: (jnp.maximum(i - nct, 0), 0)),
                    pl.BlockSpec((tm, D), lambda i, j: (jnp.minimum(i, nct - 1), 0))]
        args = [x, x_ctx]
    in_specs += [
        pl.BlockSpec((1, D), lambda i, j: (0, 0)),
        pl.BlockSpec((1, 6, D), lambda i, j: (_mod_row(i * tm, nctx), 0, 0)),
        pl.BlockSpec((D, tn), lambda i, j: (0, j)),
    ]
    args += [g.reshape(1, D), mod, w]
    out_shape = jax.ShapeDtypeStruct((rows, n), out_dtype)
    out_specs = pl.BlockSpec((tm, tn), lambda i, j: (i, j))
    if w_side is not None:
        ns = w_side.shape[1]
        in_specs.append(pl.BlockSpec((D, ns), lambda i, j: (0, 0)))
        args.append(w_side)
        out_shape = (out_shape, jax.ShapeDtypeStruct((rows, ns), F32))
        out_specs = (out_specs, pl.BlockSpec((tm, ns), lambda i, j: (i, 0)))
    if rope is not None:
        per = SEQ // tm

        def tab_map(i, j):
            return (jnp.where(i < nct, 0, 1 + (i - nct) % per), 0)
        in_specs += [pl.BlockSpec((tm, LANES), tab_map), pl.BlockSpec((tm, LANES), tab_map)]
        args += list(rope)
    return pl.pallas_call(
        functools.partial(_proj_kernel, rope=rope is not None,
                          n_ctx_tiles=0 if x_ctx is None else nct, side=w_side is not None),
        out_shape=out_shape,
        grid=(rows // tm, n // tn),
        in_specs=in_specs,
        out_specs=out_specs,
        scratch_shapes=[pltpu.VMEM((tm, D), BF16)],
        compiler_params=_cp(("parallel", "arbitrary")),
        name=name,
    )(*args)


def _head_mask(x, a):
    lane = lax.broadcasted_iota(jnp.int32, (1, LANES), 1)
    keep = (lane < HD) if a == 0 else (lane >= HD)
    return jnp.where(keep, x, jnp.zeros_like(x))


def _nt(a, b):
    return lax.dot_general(a, b, (((1,), (1,)), ((), ())), preferred_element_type=F32)


def _windowed_attention(q, kc, vc, kl_ref, vl_ref, starts, nk, local_fn, sink_fn):
    lane = lax.broadcasted_iota(jnp.int32, (1, LANES), 1)
    nsub = len(starts)
    sq = q.shape[0] // nsub
    rows = [slice(t * sq, (t + 1) * sq) for t in range(nsub)]

    s_ctx, s_loc = [], []
    for a in range(2):
        qa = _head_mask(q, a) * jnp.asarray(HD ** -0.5, q.dtype)
        s_ctx.append(_nt(qa, kc))
        s_loc.append([local_fn(a, t, _nt(qa[rows[t]], kl_ref[pl.ds(starts[t], nk), :]))
                      for t in range(nsub)])

    p_ctx, p_loc, dens = [], [], []
    for a in range(2):
        sink = sink_fn(a)
        pc_a, pl_a, den_a = [], [], []
        for t in range(nsub):
            s_c = s_ctx[a][rows[t]]
            s_l = s_loc[a][t]
            m = jnp.maximum(jnp.max(s_c, axis=-1, keepdims=True),
                            jnp.max(s_l, axis=-1, keepdims=True))
            if sink is not None:
                m = jnp.maximum(m, sink)
            p_c = jnp.exp(s_c - m)
            p_l = jnp.exp(s_l - m)
            den = jnp.sum(p_c, axis=-1, keepdims=True) + jnp.sum(p_l, axis=-1, keepdims=True)
            if sink is not None:
                den = den + jnp.exp(sink - m)
            den_a.append(den)
            pc_a.append(p_c.astype(BF16))
            pl_a.append(p_l.astype(BF16))
        p_ctx.append(jnp.concatenate(pc_a, axis=0))
        p_loc.append(pl_a)
        dens.append(jnp.concatenate(den_a, axis=0))

    outs = []
    for a in range(2):
        o = jnp.dot(p_ctx[a], vc, preferred_element_type=F32)
        o_loc = [jnp.dot(p_loc[a][t], vl_ref[pl.ds(starts[t], nk), :],
                         preferred_element_type=F32) for t in range(nsub)]
        outs.append((o + jnp.concatenate(o_loc, axis=0)) / dens[a])
    return jnp.where(lane < HD, outs[0], outs[1])


def _na_window_start(r_first):
    return jnp.clip(r_first - NA_ROWS // 2, 0, SEQ // GRID_W - NA_KROWS)


NA_NDROW = 2 * NA_ROWS - 1


def _na_bias_tile(toe_ref, a, r_first, w0r):
    lane = lax.broadcasted_iota(jnp.int32, (1, LANES), 1)
    nrows = SEQ // GRID_W
    tile_rows = []
    for rq in range(NA_SUBROWS):
        r = r_first + rq
        r0 = jnp.clip(r - NA_ROWS // 2, 0, nrows - NA_ROWS)
        blocks = []
        for j in range(NA_KROWS // 2):
            kr = w0r + 2 * j
            e = jnp.clip(kr - r + NA_ROWS, 0, NA_NDROW)
            pen0 = jnp.where(jnp.logical_and(kr >= r0, kr < r0 + NA_ROWS), 0.0, NEG)
            pen1 = jnp.where(jnp.logical_and(kr + 1 >= r0, kr + 1 < r0 + NA_ROWS), 0.0, NEG)
            blocks.append(toe_ref[a, e] + jnp.where(lane < GRID_W, pen0, pen1))
        tile_rows.append(jnp.concatenate(blocks, axis=1))
    return jnp.concatenate(tile_rows, axis=0)


def _na_kernel(q_ref, kl_ref, vl_ref, kc_ref, vc_ref, toe_ref, o_ref):
    t = pl.program_id(1)
    firsts = [NA_QROWS * t + NA_SUBROWS * s for s in range(NA_QROWS // NA_SUBROWS)]
    w0rs = [_na_window_start(r) for r in firsts]
    starts = [pl.multiple_of(w * GRID_W, 2 * GRID_W) for w in w0rs]
    o = _windowed_attention(
        q_ref[...], kc_ref[...], vc_ref[...], kl_ref, vl_ref, starts, NA_KROWS * GRID_W,
        lambda a, s, logits: logits + _na_bias_tile(toe_ref, a, firsts[s], w0rs[s]),
        lambda a: None)
    o_ref[...] = o.astype(o_ref.dtype)


def _na_bias_kernel(rpb_ref, o_ref):
    c = lax.broadcasted_iota(jnp.int32, (GRID_W, LANES), 0)
    lane = lax.broadcasted_iota(jnp.int32, (GRID_W, LANES), 1)
    kc = lane % GRID_W
    c0 = jnp.clip(c - NA_COLS // 2, 0, GRID_W - NA_COLS)
    col_ok = jnp.logical_and(kc >= c0, kc < c0 + NA_COLS)
    neg = jnp.full((GRID_W, LANES), NEG, F32)

    def toeplitz(d, shift):
        v = jnp.broadcast_to(rpb_ref[0, d:d + 1, :], (GRID_W, LANES))
        return pltpu.roll(v, shift % LANES, 1, stride=1, stride_axis=0)

    for e in range(NA_NDROW + 1):
        lo = toeplitz(e - 1, -(NA_COLS - 1)) if e >= 1 else neg
        hi = toeplitz(e, GRID_W - (NA_COLS - 1)) if e < NA_NDROW else neg
        o_ref[0, e] = jnp.where(col_ok, jnp.where(lane < GRID_W, lo, hi), NEG)


def _na_bias_blocks(rpb):
    nd, ncol = rpb.shape[1], rpb.shape[2]
    padded = jnp.pad(rpb.astype(F32), ((0, 0), (0, NA_NDROW + 1 - nd), (0, LANES - ncol)))
    return pl.pallas_call(
        _na_bias_kernel,
        out_shape=jax.ShapeDtypeStruct((NA_HEADS, NA_NDROW + 1, GRID_W, LANES), F32),
        grid=(NA_HEADS,),
        in_specs=[pl.BlockSpec((1, NA_NDROW + 1, LANES), lambda h: (h, 0, 0))],
        out_specs=pl.BlockSpec((1, NA_NDROW + 1, GRID_W, LANES), lambda h: (h, 0, 0, 0)),
        compiler_params=_cp(("arbitrary",)),
        name="na_bias_blocks",
    )(padded)


def _na_attention(big, bias, nb, nctx):
    tq = NA_QROWS * GRID_W
    ntile = SEQ // tq
    nct_q = nctx // tq
    lat0 = nctx // SEQ
    return pl.pallas_call(
        _na_kernel,
        out_shape=jax.ShapeDtypeStruct((nb * SEQ, NA_HEADS * HD), BF16),
        grid=(NA_HEADS // 2, ntile, nb),
        in_specs=[
            pl.BlockSpec((tq, LANES), lambda h, t, b: (nct_q + b * ntile + t, EV_Q_BLK + h)),
            pl.BlockSpec((SEQ, LANES), lambda h, t, b: (lat0 + b, EV_K_BLK + h)),
            pl.BlockSpec((SEQ, LANES), lambda h, t, b: (lat0 + b, EV_V_BLK + h)),
            pl.BlockSpec((CTX, LANES), lambda h, t, b: (b, EV_K_BLK + h)),
            pl.BlockSpec((CTX, LANES), lambda h, t, b: (b, EV_V_BLK + h)),
            pl.BlockSpec((2, NA_NDROW + 1, GRID_W, LANES), lambda h, t, b: (h, 0, 0, 0)),
        ],
        out_specs=pl.BlockSpec((tq, LANES), lambda h, t, b: (b * ntile + t, h)),
        compiler_params=_cp(("arbitrary", "arbitrary", "arbitrary")),
        name="na_attention",
    )(big, big, big, big, big, bias)


def _ctx_attn_kernel(q_ref, k_ref, v_ref, o_ref):
    lane = lax.broadcasted_iota(jnp.int32, (1, LANES), 1)
    q = q_ref[...]
    k = k_ref[...]
    v = v_ref[...]
    outs = []
    for a in range(2):
        qa = _head_mask(q, a) * jnp.asarray(HD ** -0.5, q.dtype)
        s = _nt(qa, k)
        m = jnp.max(s, axis=-1, keepdims=True)
        p = jnp.exp(s - m)
        den = jnp.sum(p, axis=-1, keepdims=True)
        outs.append(jnp.dot(p.astype(BF16), v, preferred_element_type=F32) / den)
    o_ref[...] = jnp.where(lane < HD, outs[0], outs[1]).astype(o_ref.dtype)


def _ctx_attention(big, nctx):
    return pl.pallas_call(
        _ctx_attn_kernel,
        out_shape=jax.ShapeDtypeStruct((nctx, NA_HEADS * HD), BF16),
        grid=(nctx // CTX, NA_HEADS // 2),
        in_specs=[
            pl.BlockSpec((CTX, LANES), lambda b, h: (b, EV_Q_BLK + h)),
            pl.BlockSpec((CTX, LANES), lambda b, h: (b, EV_K_BLK + h)),
            pl.BlockSpec((CTX, LANES), lambda b, h: (b, EV_V_BLK + h)),
        ],
        out_specs=pl.BlockSpec((CTX, LANES), lambda b, h: (b, h)),
        compiler_params=_cp(("arbitrary", "arbitrary")),
        name="ctx_attention",
    )(big, big, big)


def _swa_kernel(q_ref, kl_ref, vl_ref, kc_ref, vc_ref, sink_ref, o_ref):
    h2 = pl.program_id(1)
    n = pl.program_id(2)
    starts, oks = [], []
    for t in range(SWA_QT // SWA_SUB):
        q0 = SWA_QT * n + SWA_SUB * t
        start = pl.multiple_of(jnp.clip(q0 - SWA_WINDOW, 0, SEQ - SWA_KT), SWA_WINDOW)
        qpos = q0 + lax.broadcasted_iota(jnp.int32, (SWA_SUB, 1), 0)
        kpos = start + lax.broadcasted_iota(jnp.int32, (1, SWA_KT), 1)
        starts.append(start)
        oks.append(jnp.abs(kpos - qpos) <= SWA_WINDOW)

    o = _windowed_attention(
        q_ref[...], kc_ref[...], vc_ref[...], kl_ref, vl_ref, starts, SWA_KT,
        lambda a, t, logits: jnp.where(oks[t], logits, NEG),
        lambda a: sink_ref[pl.ds(2 * h2 + a, 1), :][:, :1])
    o_ref[...] = o.astype(o_ref.dtype)


def _swa_attention(rp, pp, sink, nb, nctx):
    nq = SEQ // SWA_QT
    nct_q = nctx // SWA_QT
    lat0 = nctx // SEQ
    sinkv = jnp.broadcast_to(sink.astype(F32)[:, None], (SWA_HEADS, LANES))
    return pl.pallas_call(
        _swa_kernel,
        out_shape=jax.ShapeDtypeStruct((nb * SEQ, SWA_HEADS * HD), BF16),
        grid=(nb, SWA_HEADS // 2, nq),
        in_specs=[
            pl.BlockSpec((SWA_QT, LANES), lambda b, h, n: (nct_q + b * nq + n, h)),
            pl.BlockSpec((SEQ, LANES), lambda b, h, n: (lat0 + b, 12 + h // 2)),
            pl.BlockSpec((SEQ, LANES), lambda b, h, n: (lat0 + b, 16 + h // 2)),
            pl.BlockSpec((CTX, LANES), lambda b, h, n: (b, 12 + h // 2)),
            pl.BlockSpec((CTX, LANES), lambda b, h, n: (b, 16 + h // 2)),
            pl.BlockSpec((SWA_HEADS, LANES), lambda b, h, n: (0, 0)),
        ],
        out_specs=pl.BlockSpec((SWA_QT, LANES), lambda b, h, n: (b * nq + n, h)),
        compiler_params=_cp(("arbitrary", "arbitrary", "arbitrary")),
        name="swa_attention",
    )(rp, rp, pp, rp, pp, sinkv)


CONV_PAD = (SSM_CONV - 1) // 2
CONV_TAPS = tuple(s for s in range(-CONV_PAD, CONV_PAD + 1) if s != 0)
CONV_HALO = 16
CONV_EDGE = 8


def _conv_shift_tables(tb):
    nt = len(CONV_TAPS)
    shift = np.zeros((nt * tb, tb), np.float32)
    edge = np.zeros((nt * 2 * CONV_EDGE, 2 * CONV_HALO), np.float32)
    for t, s in enumerate(CONV_TAPS):
        for i in range(tb):
            j = i + s
            if 0 <= j < tb:
                shift[t * tb + i, j] = 1.0
            elif j < 0:
                edge[t * 2 * CONV_EDGE + i, CONV_HALO + j] = 1.0
            else:
                edge[t * 2 * CONV_EDGE + CONV_EDGE + i - (tb - CONV_EDGE), CONV_HALO + j - tb] = 1.0
    return jnp.asarray(shift, BF16), jnp.asarray(edge, BF16)


def _conv_kernel(xp_ref, x_ref, xn_ref, sh_ref, ed_ref, w_ref, b_ref, o_ref, *, nctx_blocks,
                 per_seq):
    i = pl.program_id(0)
    tb = x_ref.shape[0]
    k = i - nctx_blocks
    has_prev = jnp.logical_and(i >= nctx_blocks, k % per_seq != 0)
    has_next = jnp.logical_and(i >= nctx_blocks, k % per_seq != per_seq - 1)
    x = x_ref[...]
    zero = jnp.zeros((CONV_HALO, x.shape[1]), x.dtype)
    halo = jnp.concatenate([jnp.where(has_prev, xp_ref[...], zero),
                            jnp.where(has_next, xn_ref[...], zero)], axis=0)
    taps = jnp.dot(sh_ref[...], x, preferred_element_type=F32)
    edges = jnp.dot(ed_ref[...], halo, preferred_element_type=F32)
    acc = x.astype(F32) * w_ref[CONV_PAD:CONV_PAD + 1, :] + b_ref[...]
    for t, s in enumerate(CONV_TAPS):
        tap = taps[t * tb:(t + 1) * tb]
        e0 = 2 * CONV_EDGE * t
        tap = jnp.concatenate([tap[:CONV_EDGE] + edges[e0:e0 + CONV_EDGE],
                               tap[CONV_EDGE:tb - CONV_EDGE],
                               tap[tb - CONV_EDGE:] + edges[e0 + CONV_EDGE:e0 + 2 * CONV_EDGE]], axis=0)
        acc = acc + tap * w_ref[CONV_PAD + s:CONV_PAD + s + 1, :]
    o_ref[...] = _silu(acc).astype(o_ref.dtype)


def _conv_silu(big, conv_w, conv_b, nctx):
    rows = big.shape[0]
    tb = CTX
    tc = SSM_CONV_CH
    nblk = rows // tb
    hpb = tb // CONV_HALO
    shift, edge = _conv_shift_tables(tb)
    return pl.pallas_call(
        functools.partial(_conv_kernel, nctx_blocks=nctx // tb, per_seq=SEQ // tb),
        out_shape=jax.ShapeDtypeStruct((rows, SSM_CONV_CH), BF16),
        grid=(nblk, SSM_CONV_CH // tc),
        in_specs=[
            pl.BlockSpec((CONV_HALO, tc), lambda i, c: (jnp.maximum(i * hpb - 1, 0), c)),
            pl.BlockSpec((tb, tc), lambda i, c: (i, c)),
            pl.BlockSpec((CONV_HALO, tc),
                         lambda i, c: (jnp.minimum((i + 1) * hpb, nblk * hpb - 1), c)),
            pl.BlockSpec(shift.shape, lambda i, c: (0, 0)),
            pl.BlockSpec(edge.shape, lambda i, c: (0, 0)),
            pl.BlockSpec((SSM_CONV, tc), lambda i, c: (0, c)),
            pl.BlockSpec((1, tc), lambda i, c: (0, c)),
        ],
        out_specs=pl.BlockSpec((tb, tc), lambda i, c: (i, c)),
        compiler_params=_cp(("arbitrary", "arbitrary")),
        name="conv_silu",
    )(big, big, big, shift, edge, conv_w.reshape(SSM_CONV, SSM_CONV_CH),
      conv_b.reshape(1, SSM_CONV_CH))


NCC = CTX // CHUNK
NLC = SEQ // CHUNK


def _chunk_block(b, s, d, nctx):
    cc = s if d == 0 else NCC - 1 - s
    lc = s - NCC if d == 0 else NLC + NCC - 1 - s
    return jnp.where(s < NCC, NCC * b + cc, nctx // CHUNK + NLC * b + lc)


def _latent_chunk_block(b, s, d):
    lc = s - NCC if d == 0 else NLC + NCC - 1 - s
    return NLC * b + jnp.clip(lc, 0, NLC - 1)


def _split3(a):
    hi = a.astype(BF16)
    r1 = a - hi.astype(F32)
    mid = r1.astype(BF16)
    lo = (r1 - mid.astype(F32)).astype(BF16)
    return hi, mid, lo


def _softplus(x):
    return jnp.maximum(x, 0.0) + jnp.log(1.0 + jnp.exp(-jnp.abs(x)))


def _ssd_direction(d, x_ref, b_ref, c_ref, dt_ref, dtt_ref, bias_ref, biast_ref, alog_ref,
                   alogt_ref, y_ref, s_ref):
    q = CHUNK
    ii = lax.broadcasted_iota(jnp.int32, (q, q), 0)
    jj = lax.broadcasted_iota(jnp.int32, (q, q), 1)
    causal = (jj <= ii) if d == 0 else (jj >= ii)
    tri = jnp.where(causal, 1.0, 0.0).astype(BF16)
    trit = jnp.where((ii <= jj) if d == 0 else (ii >= jj), 1.0, 0.0).astype(BF16)

    dt = _softplus(dt_ref[0] + bias_ref[d])
    a = dt * (-jnp.exp(alog_ref[d]))
    dtt = _softplus(dtt_ref[0] + biast_ref[d])
    at = dtt * (-jnp.exp(alogt_ref[d]))
    acum = sum(jnp.dot(tri, p, preferred_element_type=F32) for p in _split3(a))
    acumt = sum(jnp.dot(p, trit, preferred_element_type=F32) for p in _split3(at))
    atot = jnp.sum(at, axis=1, keepdims=True)
    wrow = dtt * jnp.exp(atot - acumt)
    keep = jnp.exp(jnp.broadcast_to(atot, (SSM_HEADS, LANES)))

    first = lax.broadcasted_iota(jnp.int32, (1, LANES), 1) < HD
    ngroups = b_ref.shape[1] // SSM_STATE
    pairs_per_group = SSM_HEADS // 2 // ngroups
    for g in range(ngroups):
        bg = b_ref[:, g * SSM_STATE:(g + 1) * SSM_STATE]
        cg = c_ref[:, g * SSM_STATE:(g + 1) * SSM_STATE]
        cb = _nt(cg, bg)
        cgf = cg.astype(F32)
        bt = bg.astype(F32).T
        for pp in range(pairs_per_group):
            p = g * pairs_per_group + pp
            x = x_ref[:, p * LANES:(p + 1) * LANES]
            st = s_ref[d, p]
            rhs = jnp.concatenate([x, st.astype(BF16)], axis=0)
            ys, us = [], []
            for h in (2 * p, 2 * p + 1):
                col = jnp.broadcast_to(acum[:, h:h + 1], (q, LANES))
                dec = jnp.exp(jnp.where(causal, col - acumt[h:h + 1, :], NEG)) * (cb * dtt[h:h + 1, :])
                lhs = jnp.concatenate([dec, jnp.exp(col) * cgf], axis=1).astype(BF16)
                ys.append(jnp.dot(lhs, rhs, preferred_element_type=F32))
                us.append(jnp.dot((bt * wrow[h:h + 1, :]).astype(BF16), x,
                                  preferred_element_type=F32))
            y_ref[:, p * LANES:(p + 1) * LANES] = jnp.where(first, ys[0], ys[1]).astype(y_ref.dtype)
            keep_p = jnp.where(first, keep[2 * p:2 * p + 1, :], keep[2 * p + 1:2 * p + 2, :])
            s_ref[d, p] = st * keep_p + jnp.where(first, us[0], us[1])


def _ssd_kernel(*refs):
    fwd, bwd, (bias_ref, biast_ref, alog_ref, alogt_ref), (yf_ref, yb_ref, s_ref) = (
        refs[0:5], refs[5:10], refs[10:14], refs[14:17])
    assert CHUNK == LANES

    @pl.when(pl.program_id(1) == 0)
    def _():
        s_ref[...] = jnp.zeros_like(s_ref)

    for d, ins, y_ref in ((0, fwd, yf_ref), (1, bwd, yb_ref)):
        _ssd_direction(d, *ins, bias_ref, biast_ref, alog_ref, alogt_ref, y_ref, s_ref)


def _ssd_scan(xbc, dt_raw, dt_bias, a_log, nb, nctx):
    rows = xbc.shape[0]
    dtd = dt_raw.reshape(rows, 2, SSM_HEADS).transpose(1, 0, 2)
    dtt = dtd.transpose(0, 2, 1)

    def direction_specs(d):
        blk = functools.partial(_chunk_block, d=d, nctx=nctx)
        return [
            pl.BlockSpec((CHUNK, SSM_INNER), lambda b, s: (blk(b, s), 0)),
            pl.BlockSpec((CHUNK, 2 * SSM_STATE), lambda b, s: (blk(b, s), 4)),
            pl.BlockSpec((CHUNK, 2 * SSM_STATE), lambda b, s: (blk(b, s), 5)),
            pl.BlockSpec((1, CHUNK, SSM_HEADS), lambda b, s: (d, blk(b, s), 0)),
            pl.BlockSpec((1, SSM_HEADS, CHUNK), lambda b, s: (d, 0, blk(b, s))),
        ]

    whole = lambda shape: pl.BlockSpec(shape, lambda b, s: (0, 0, 0))
    out_spec = lambda d: pl.BlockSpec(
        (CHUNK, SSM_INNER), lambda b, s: (_chunk_block(b, s, d, nctx), 0))
    y_shape = jax.ShapeDtypeStruct((rows, SSM_INNER), BF16)
    return pl.pallas_call(
        _ssd_kernel,
        out_shape=(y_shape, y_shape),
        grid=(nb, NCC + NLC),
        in_specs=direction_specs(0) + direction_specs(1) + [
            whole((2, 1, SSM_HEADS)), whole((2, SSM_HEADS, 1)),
            whole((2, 1, SSM_HEADS)), whole((2, SSM_HEADS, 1)),
        ],
        out_specs=(out_spec(0), out_spec(1)),
        scratch_shapes=[pltpu.VMEM((2, SSM_HEADS // 2, SSM_STATE, LANES), F32)],
        compiler_params=_cp(("arbitrary", "arbitrary")),
        name="ssd_scan",
    )(*([xbc, xbc, xbc, dtd, dtt] * 2),
      dt_bias.reshape(2, 1, SSM_HEADS), dt_bias.reshape(2, SSM_HEADS, 1),
      a_log.reshape(2, 1, SSM_HEADS), a_log.reshape(2, SSM_HEADS, 1))


def _ret_geometry(d):
    q = CHUNK
    ii = lax.broadcasted_iota(jnp.int32, (q, q), 0)
    jj = lax.broadcasted_iota(jnp.int32, (q, q), 1)
    dist = (ii - jj) if d == 0 else (jj - ii)
    ri = lax.broadcasted_iota(jnp.int32, (q, LANES), 0)
    cj = lax.broadcasted_iota(jnp.int32, (1, q), 1)
    steps_in = ((ri + 1) if d == 0 else (q - ri)).astype(F32)
    steps_out = ((q - 1 - cj) if d == 0 else cj).astype(F32)
    return dist >= 0, dist.astype(F32), steps_in, steps_out


def _ret_kernel(*refs):
    fwd, bwd, ld_ref, (yf_ref, yb_ref, s_ref) = refs[0:3], refs[3:6], refs[6], refs[7:10]
    q = CHUNK
    step = pl.program_id(1)

    @pl.when(step == 0)
    def _():
        s_ref[...] = jnp.zeros_like(s_ref)

    scale = jnp.asarray(HD ** -0.5, BF16)
    dirs = ((0, fwd, yf_ref), (1, bwd, yb_ref))

    def log_gamma(d, h):
        lg = jnp.log(1.0 - jnp.exp(ld_ref[d, h]))
        return jnp.broadcast_to(lg[None], (q // 8, 8, LANES)).reshape(q, LANES)

    qk = {}
    for d, (c_ref, b_ref, x_ref), y_ref in dirs:
        for h in range(RET_HEADS):
            pb, a = h // 2, h % 2
            ch = _head_mask(c_ref[:, pb * LANES:(pb + 1) * LANES], a) * scale
            bh = _head_mask(b_ref[:, pb * LANES:(pb + 1) * LANES], a)
            qk[d, h] = (ch, _nt(ch, bh))

    for d, (c_ref, b_ref, x_ref), y_ref in dirs:
        causal, distf, steps_in, _ = _ret_geometry(d)
        for h in range(RET_HEADS):
            lg = log_gamma(d, h)
            ch, cb = qk[d, h]
            dec = jnp.exp(jnp.where(causal, distf * lg, NEG)) * cb
            grow = jnp.exp(steps_in * lg) * ch.astype(F32)
            lhs = jnp.concatenate([dec, grow], axis=1).astype(BF16)
            x = x_ref[:, h * LANES:(h + 1) * LANES]
            rhs = jnp.concatenate([x, s_ref[d, h].astype(BF16)], axis=0)
            y_ref[:, h * LANES:(h + 1) * LANES] = jnp.dot(
                lhs, rhs, preferred_element_type=F32).astype(y_ref.dtype)

    for d, (c_ref, b_ref, x_ref), y_ref in dirs:
        _, _, _, steps_out = _ret_geometry(d)
        for h in range(RET_HEADS):
            lg = log_gamma(d, h)
            pb, a = h // 2, h % 2
            bt = _head_mask(b_ref[:, pb * LANES:(pb + 1) * LANES], a).astype(F32).T
            bth = (bt * jnp.exp(steps_out * lg[0:1, :])).astype(BF16)
            x = x_ref[:, h * LANES:(h + 1) * LANES]
            s_ref[d, h] = s_ref[d, h] * jnp.exp(q * lg) + jnp.dot(
                bth, x, preferred_element_type=F32)


def _retention_scan(rp, pp, log_decay, nb, nctx):
    w = RET_HEADS * HD
    wv = RET_HEADS * LANES

    def direction_specs(d):
        blk = functools.partial(_chunk_block, d=d, nctx=nctx)
        return [
            pl.BlockSpec((CHUNK, w), lambda b, s: (blk(b, s), 1)),
            pl.BlockSpec((CHUNK, w), lambda b, s: (blk(b, s), 2)),
            pl.BlockSpec((CHUNK, wv), lambda b, s: (blk(b, s), 0)),
        ]

    out_spec = lambda d: pl.BlockSpec((CHUNK, wv), lambda b, s: (_latent_chunk_block(b, s, d), 0))
    y_shape = jax.ShapeDtypeStruct((nb * SEQ, wv), BF16)
    return pl.pallas_call(
        _ret_kernel,
        out_shape=(y_shape, y_shape),
        grid=(nb, NCC + NLC),
        in_specs=direction_specs(0) + direction_specs(1) + [
            pl.BlockSpec((2, RET_HEADS, 8, LANES), lambda b, s: (0, 0, 0, 0))],
        out_specs=(out_spec(0), out_spec(1)),
        scratch_shapes=[pltpu.VMEM((2, RET_HEADS, LANES, LANES), F32)],
        compiler_params=_cp(("arbitrary", "arbitrary")),
        name="retention_scan",
    )(*([rp, rp, pp] * 2),
      jnp.broadcast_to(log_decay.astype(F32)[:, :, None, None], (2, RET_HEADS, 8, LANES)))


def _ssm_gate_norm(yf_ref, yb_ref, xs_ref, z_ref, dsk_ref, g_ref):
    y = yf_ref[...].astype(F32) + yb_ref[...].astype(F32) + xs_ref[...].astype(F32) * dsk_ref[...]
    yz = y * _silu(z_ref[...].astype(F32))
    half = SSM_INNER // 2
    segs = []
    for g in range(2):
        seg = yz[:, g * half:(g + 1) * half]
        ms = jnp.mean(seg * seg, axis=-1, keepdims=True)
        segs.append((seg * lax.rsqrt(ms + EPS) * g_ref[:, g * half:(g + 1) * half]).astype(BF16))
    return jnp.concatenate(segs, axis=1)


def _ret_gate_norm(rf_ref, rb_ref, rg_ref, g_ref, b_ref):
    y = rf_ref[...].astype(F32) + rb_ref[...].astype(F32)
    segs = []
    for h in range(RET_HEADS):
        sl = slice(h * LANES, (h + 1) * LANES)
        seg = y[:, sl]
        mu = jnp.mean(seg, axis=-1, keepdims=True)
        cen = seg - mu
        var = jnp.mean(cen * cen, axis=-1, keepdims=True)
        yn = cen * lax.rsqrt(var + EPS) * g_ref[:, sl] + b_ref[:, sl]
        segs.append((_silu(rg_ref[:, sl].astype(F32)) * yn).astype(BF16))
    return jnp.concatenate(segs, axis=1)


def _outproj_kernel(*refs, n_ctx_tiles, split_x, split_a1, mixer_fn):
    refs = list(refs)
    x_ref = refs.pop(0)
    xc_ref = refs.pop(0) if split_x else None
    a1_ref = refs.pop(0)
    a1c_ref = refs.pop(0) if split_a1 else None
    w1_ref, w2_ref, mod_ref, o_ref = refs[-4:]
    is_ctx = pl.program_id(0) < n_ctx_tiles
    a1 = jnp.where(is_ctx, a1c_ref[...], a1_ref[...]) if split_a1 else a1_ref[...]
    x = jnp.where(is_ctx, xc_ref[...], x_ref[...]) if split_x else x_ref[...]
    a2 = mixer_fn(*refs[:-4])
    y = (jnp.dot(a1, w1_ref[...], preferred_element_type=F32)
         + jnp.dot(a2, w2_ref[...], preferred_element_type=F32))
    o_ref[...] = x + mod_ref[0][2:3] * y


def _outproj(x, a1, mixer_fn, mixer_args, mixer_specs, w_out, mod, nctx, tm, rows, x_off=0,
             a1_ctx=None, x_ctx=None):
    k1 = a1.shape[1]
    k2 = w_out.shape[0] - k1
    w1 = w_out[:k1].astype(BF16)
    w2 = w_out[k1:].astype(BF16)
    nct = nctx // tm if (a1_ctx is not None or x_ctx is not None) else 0
    lat = lambda i: (jnp.maximum(i - nct, 0), 0)
    ctx_ = lambda i: (jnp.minimum(i, nct - 1), 0)
    in_specs, args = [], []
    if x_ctx is None:
        in_specs.append(pl.BlockSpec((tm, D), lambda i: (i + x_off, 0)))
        args.append(x)
    else:
        in_specs += [pl.BlockSpec((tm, D), lat), pl.BlockSpec((tm, D), ctx_)]
        args += [x, x_ctx]
    in_specs.append(pl.BlockSpec((tm, k1), lat))
    args.append(a1)
    if a1_ctx is not None:
        in_specs.append(pl.BlockSpec((tm, k1), ctx_))
        args.append(a1_ctx)
    in_specs += list(mixer_specs) + [
        pl.BlockSpec((k1, D), lambda i: (0, 0)),
        pl.BlockSpec((k2, D), lambda i: (0, 0)),
        pl.BlockSpec((1, 6, D), lambda i: (_mod_row((i + x_off) * tm, nctx), 0, 0)),
    ]
    args += list(mixer_args) + [w1, w2, mod]
    return pl.pallas_call(
        functools.partial(_outproj_kernel, n_ctx_tiles=nct, split_x=x_ctx is not None,
                          split_a1=a1_ctx is not None, mixer_fn=mixer_fn),
        out_shape=jax.ShapeDtypeStruct((rows, D), F32),
        grid=(rows // tm,),
        in_specs=in_specs,
        out_specs=pl.BlockSpec((tm, D), lambda i: (i, 0)),
        compiler_params=_cp(("parallel",)),
        name="outproj",
    )(*args)


def _top2_gates(logits):
    lane = lax.broadcasted_iota(jnp.int32, logits.shape, 1)
    valid = lane < N_EXPERTS
    l0 = jnp.where(valid, logits, NEG)
    m1 = jnp.max(l0, axis=-1, keepdims=True)
    i1 = jnp.min(jnp.where(l0 == m1, lane, LANES), axis=-1, keepdims=True)
    l1 = jnp.where(lane == i1, NEG, l0)
    m2 = jnp.max(l1, axis=-1, keepdims=True)
    i2 = jnp.min(jnp.where(l1 == m2, lane, LANES), axis=-1, keepdims=True)
    e2 = jnp.exp(m2 - m1)
    g1 = 1.0 / (1.0 + e2)
    g2 = e2 / (1.0 + e2)
    return jnp.where(lane == i1, g1, 0.0) + jnp.where(lane == i2, g2, 0.0)


def _ffn_kernel(x_ref, g_ref, mod_ref, *rest, routed, final, sub):
    rest = list(rest)
    router_ref = rest.pop(0) if routed else None
    w1_ref, w3_ref, w2_ref = rest[:3]
    rest = rest[3:]
    fg_ref = rest.pop(0) if final else None
    o_ref, h_ref, acc_ref = rest[:3]
    gate_ref = rest[3] if routed else None
    e = pl.program_id(1)
    f = pl.program_id(2)
    first = jnp.logical_and(e == 0, f == 0)
    last = jnp.logical_and(e == pl.num_programs(1) - 1, f == pl.num_programs(2) - 1)

    @pl.when(first)
    def _():
        h = _norm_mod(x_ref[...], g_ref[...], mod_ref[0], 3)
        h16 = h.astype(BF16)
        h_ref[...] = h16
        acc_ref[...] = jnp.zeros_like(acc_ref)
        if routed:
            hl = (h - h16.astype(F32)).astype(BF16)
            r = router_ref[...]
            rh = r.astype(BF16)
            rl = (r - rh.astype(F32)).astype(BF16)
            logits = (jnp.dot(h16, rh, preferred_element_type=F32)
                      + jnp.dot(h16, rl, preferred_element_type=F32)
                      + jnp.dot(hl, rh, preferred_element_type=F32))
            gate_ref[...] = _top2_gates(logits)

    h16 = h_ref[...]
    if routed:
        lane = lax.broadcasted_iota(jnp.int32, gate_ref.shape, 1)
        ge = jnp.sum(jnp.where(lane == e, gate_ref[...], 0.0), axis=-1, keepdims=True)
    for c in range(w1_ref.shape[2] // sub):
        cols = slice(c * sub, (c + 1) * sub)
        u = _silu(jnp.dot(h16, w1_ref[0, :, cols], preferred_element_type=F32)) * jnp.dot(
            h16, w3_ref[0, :, cols], preferred_element_type=F32)
        if routed:
            u = u * ge
        acc_ref[...] += jnp.dot(u.astype(BF16), w2_ref[0, cols, :], preferred_element_type=F32)

    @pl.when(last)
    def _():
        y = x_ref[...] + mod_ref[0][5:6] * acc_ref[...]
        if final:
            ms = jnp.mean(y * y, axis=-1, keepdims=True)
            y = y * lax.rsqrt(ms + EPS) * fg_ref[...]
        o_ref[...] = y


def _ffn(x, g, mod, w1, w3, w2, nctx, tm, tf, sub, x_off=0, rows=None, router=None,
         final_g=None):
    rows = x.shape[0] if rows is None else rows
    ne, _, dff = w1.shape
    routed = router is not None
    final = final_g is not None
    in_specs = [
        pl.BlockSpec((tm, D), lambda i, e, f: (i + x_off, 0)),
        pl.BlockSpec((1, D), lambda i, e, f: (0, 0)),
        pl.BlockSpec((1, 6, D), lambda i, e, f: (_mod_row((i + x_off) * tm, nctx), 0, 0)),
    ]
    args = [x, g.reshape(1, D), mod]
    if routed:
        in_specs.append(pl.BlockSpec((D, LANES), lambda i, e, f: (0, 0)))
        args.append(jnp.pad(router.astype(F32), ((0, 0), (0, LANES - ne))))
    in_specs += [
        pl.BlockSpec((1, D, tf), lambda i, e, f: (e, 0, f)),
        pl.BlockSpec((1, D, tf), lambda i, e, f: (e, 0, f)),
        pl.BlockSpec((1, tf, D), lambda i, e, f: (e, f, 0)),
    ]
    args += [w1, w3, w2]
    if final:
        in_specs.append(pl.BlockSpec((1, D), lambda i, e, f: (0, 0)))
        args.append(final_g.reshape(1, D))
    scratch = [pltpu.VMEM((tm, D), BF16), pltpu.VMEM((tm, D), F32)]
    if routed:
        scratch.append(pltpu.VMEM((tm, LANES), F32))
    return pl.pallas_call(
        functools.partial(_ffn_kernel, routed=routed, final=final, sub=sub),
        out_shape=jax.ShapeDtypeStruct((rows, D), F32),
        grid=(rows // tm, ne, dff // tf),
        in_specs=in_specs,
        out_specs=pl.BlockSpec((tm, D), lambda i, e, f: (i, 0)),
        scratch_shapes=scratch,
        compiler_params=_cp(("parallel", "arbitrary", "arbitrary")),
        name="moe_ffn" if routed else "dense_ffn",
    )(*args)


def _rope_tables(tm):
    t = np.arange(SEQ)
    n_freq = HD // 4
    inv = ROPE_BASE ** (-np.arange(n_freq, dtype=np.float64) / n_freq)
    ang = np.concatenate([(t // GRID_W)[:, None] * inv, (t % GRID_W)[:, None] * inv], axis=-1)
    cos = np.tile(np.cos(ang), (1, 4))
    sin = np.tile(np.concatenate([-np.sin(ang), np.sin(ang)], axis=-1), (1, 2))
    cos = np.concatenate([np.ones((tm, LANES)), cos], axis=0)
    sin = np.concatenate([np.zeros((tm, LANES)), sin], axis=0)
    return jnp.asarray(cos, F32), jnp.asarray(sin, F32)


def _dup_heads(w, nheads):
    w = w.reshape(D, nheads, 1, HD)
    return jnp.broadcast_to(w, (D, nheads, 2, HD)).reshape(D, nheads * 2 * HD)


def kernel(x, c, ctx, c_ctx, ada_w, ada_b, norm_attn_g, norm_ffn_g, ev_w_in, ev_w_out, na_rpb,
           ssm_conv_w, ssm_conv_b, ssm_dt_bias, ssm_a_log, ssm_d, ssm_norm_g, ffn_w1, ffn_w3, ffn_w2,
           od_w_in, od_w_out, swa_sink, ret_log_decay, ret_gn_g, ret_gn_b, moe_router, moe_w1,
           moe_w3, moe_w2, final_g):
    nb = x.shape[0]
    nctx = -(-(nb * CTX) // SEQ) * SEQ
    tm = 1024
    tmo = tm // 2
    x_lat = x.reshape(nb * SEQ, D).astype(F32)
    x_ctx = jnp.pad(ctx.reshape(nb * CTX, D).astype(F32), ((0, nctx - nb * CTX), (0, 0)))

    rp = -(-(nb + 1) // 8) * 8
    cin = jnp.zeros((rp, D), F32).at[0].set(c_ctx).at[1:nb + 1].set(c)
    mod = _modulation(cin, ada_w, ada_b).reshape(2, rp, 6, D)

    w_in = ev_w_in[0]
    q_, k_, v_, z_, xbc_, dt_ = jnp.split(w_in, [512, 1024, 1536, 2560, 4096], axis=1)
    w_big = jnp.concatenate([xbc_, q_, k_, v_, z_], axis=1).astype(BF16)
    big, dt_raw = _proj(x_lat, norm_attn_g[0], mod[0], w_big, BF16, nctx, tm, 4096, x_ctx=x_ctx,
                        w_side=dt_.astype(BF16), name="even_in_proj")
    attn = _na_attention(big, _na_bias_blocks(na_rpb[0]), nb, nctx)
    attn_ctx = _ctx_attention(big, nctx)
    xbc = _conv_silu(big, ssm_conv_w[0], ssm_conv_b[0], nctx)
    yf, yb = _ssd_scan(xbc, dt_raw, ssm_dt_bias[0], ssm_a_log[0], nb, nctx)
    wide = lambda col: pl.BlockSpec((tmo, SSM_INNER), lambda i: (i, col))
    vec = pl.BlockSpec((1, SSM_INNER), lambda i: (0, 0))
    dsk = jnp.repeat(ssm_d[0].astype(F32), HD).reshape(1, SSM_INNER)
    xs = _outproj(x_lat, attn, _ssm_gate_norm,
                  [yf, yb, xbc, big, dsk, ssm_norm_g[0].reshape(1, SSM_INNER)],
                  [wide(0), wide(0), wide(0), wide(EV_Z_BLK), vec, vec],
                  ev_w_out[0], mod[0], nctx, tmo, nctx + nb * SEQ, a1_ctx=attn_ctx, x_ctx=x_ctx)
    xs = _ffn(xs, norm_ffn_g[0], mod[0], ffn_w1.astype(BF16), ffn_w3.astype(BF16),
              ffn_w2.astype(BF16), nctx, tm, D_FF, 256)

    w_in = od_w_in[0]
    q_, k_, v_, rq_, rk_, rv_, rg_ = jnp.split(w_in, [512, 640, 768, 1280, 1792, 2816], axis=1)
    w_rope = jnp.concatenate([q_, rq_, rk_, _dup_heads(k_, 2)], axis=1).astype(BF16)
    w_plain = jnp.concatenate([rv_, rg_, _dup_heads(v_, 2)], axis=1).astype(BF16)
    rope_tabs = _rope_tables(tm)
    rpj = _proj(xs, norm_attn_g[1], mod[1], w_rope, BF16, nctx, tm, w_rope.shape[1],
                rope=rope_tabs, name="odd_rope_proj")
    ppj = _proj(xs, norm_attn_g[1], mod[1], w_plain, BF16, nctx, tm, w_plain.shape[1],
                name="odd_plain_proj")
    yw = _swa_attention(rpj, ppj, swa_sink[0], nb, nctx)
    rf, rb = _retention_scan(rpj, ppj, ret_log_decay[0], nb, nctx)
    x_off = nctx // tmo
    rg_spec = pl.BlockSpec((tmo, SSM_INNER), lambda i: (i + x_off, 1))
    xl = _outproj(xs, yw, _ret_gate_norm,
                  [rf, rb, ppj, ret_gn_g[0].reshape(1, SSM_INNER), ret_gn_b[0].reshape(1, SSM_INNER)],
                  [wide(0), wide(0), rg_spec, vec, vec],
                  od_w_out[0], mod[1], nctx, tmo, nb * SEQ, x_off=x_off)
    out = _ffn(xl, norm_ffn_g[1], mod[1], moe_w1[0].astype(BF16), moe_w3[0].astype(BF16),
               moe_w2[0].astype(BF16), 0, tm, D_FF_EXPERT // 2, 256, router=moe_router[0],
               final_g=final_g)
    return out.reshape(nb, SEQ, D).astype(x.dtype)
```
